```python
import functools
import jax
import jax.numpy as jnp
from jax import lax
import numpy as np

D_MODEL = 1024
BATCH = 2
SEQ = 8192
DEPTH = 1
DEC_BATCH = 128
DEC_SEQ = 4
PAST_LEN = 2048
PAGE_SIZE = 128

MIX_WIDTH = D_MODEL
ATTN_WIDTH = MIX_WIDTH // 2
RET_WIDTH = MIX_WIDTH - ATTN_WIDTH
ATTN_HEAD_DIM = 64
ATTN_HEADS = ATTN_WIDTH // ATTN_HEAD_DIM
RET_HEAD_DIM = 128
RET_HEADS = RET_WIDTH // RET_HEAD_DIM
IN_WIDTH = 3 * ATTN_WIDTH + 4 * RET_WIDTH
MOBA_BLOCK = 256
MOBA_TOP_K = 3
MOBA_QUERY_BLOCK = 128
ROPE_THETA = 500000.0
ROPE_DIMS = ATTN_HEAD_DIM // 4
RET_CHUNK = 128
RET_ROT_BASE = 10000.0
D_FF = 2816
N_MODS = 9
EPS = 1e-6
NEG_INF = -1e30

kernel_name = 'moba_retnet_parallel_heads_macaron_adaln_step'


def rmsnorm(x, w):
    xf = x.astype(jnp.float32)
    xf = xf * lax.rsqrt(jnp.mean(xf * xf, axis=-1, keepdims=True) + EPS)
    return xf.astype(x.dtype) * w


def head_rmsnorm(x):
    xf = x.astype(jnp.float32)
    return (xf * lax.rsqrt(jnp.mean(xf * xf, axis=-1, keepdims=True) + EPS)).astype(x.dtype)


def partial_rope(x, pos):
    half = ROPE_DIMS // 2
    inv = ROPE_THETA ** (-jnp.arange(half, dtype=jnp.float32) * 2.0 / ROPE_DIMS)
    ang = pos.astype(jnp.float32)[:, None] * inv[None, :]
    cos = jnp.cos(ang)[:, None, :].astype(x.dtype)
    sin = jnp.sin(ang)[:, None, :].astype(x.dtype)
    x1 = x[..., :half]
    x2 = x[..., half:ROPE_DIMS]
    return jnp.concatenate([x1 * cos - x2 * sin, x2 * cos + x1 * sin, x[..., ROPE_DIMS:]], axis=-1)


def retnet_rotate(x, pos):
    d = x.shape[-1]
    inv = 1.0 / (RET_ROT_BASE ** jnp.linspace(0.0, 1.0, d // 2, dtype=jnp.float32))
    ang = pos.astype(jnp.float32)[:, None] * inv[None, :]
    cos = jnp.cos(ang)[:, None, :].astype(x.dtype)
    sin = jnp.sin(ang)[:, None, :].astype(x.dtype)
    xp = x.reshape(x.shape[:-1] + (d // 2, 2))
    a = xp[..., 0]
    b = xp[..., 1]
    return jnp.stack([a * cos - b * sin, b * cos + a * sin], axis=-1).reshape(x.shape)


def retention_log_decay():
    return jnp.log(1.0 - 2.0 ** (-5.0 - jnp.arange(RET_HEADS, dtype=jnp.float32)))


def adaln_mods(c, w_ada, b_ada):
    m = jax.nn.silu(c) @ w_ada + b_ada
    m = m.reshape(c.shape[0], N_MODS, 1, -1)
    return [m[:, i] for i in range(N_MODS)]


def swiglu(x, w_gate, w_up, w_down):
    return (jax.nn.silu(x @ w_gate) * (x @ w_up)) @ w_down


def macaron_half_ffn(x, norm_w, shift, scale, gate, w_gate, w_up, w_down):
    xn = rmsnorm(x, norm_w) * (1.0 + scale) + shift
    return x + 0.5 * gate * swiglu(xn, w_gate, w_up, w_down)


def block_sparse_attend(q, k_sel, v_sel, sel_valid, k_own, v_own, own_valid):
    B, Q, H, K, L, d = k_sel.shape
    scale = d ** -0.5
    s_sel = jnp.einsum('bqhd,bqhkld->bqhkl', q, k_sel).astype(jnp.float32) * scale
    s_sel = jnp.where(sel_valid[..., None], s_sel, NEG_INF).reshape(B, Q, H, K * L)
    s_own = jnp.einsum('bqhd,blhd->bqhl', q, k_own).astype(jnp.float32) * scale
    s_own = jnp.where(own_valid[None, :, None, :], s_own, NEG_INF)
    p = jax.nn.softmax(jnp.concatenate([s_sel, s_own], axis=-1), axis=-1).astype(v_own.dtype)
    p_sel = p[..., :K * L].reshape(B, Q, H, K, L)
    p_own = p[..., K * L:]
    return jnp.einsum('bqhkl,bqhkld->bqhd', p_sel, v_sel) + jnp.einsum('bqhl,blhd->bqhd', p_own, v_own)


def moba_prompt(q, k, v):
    B, S, H, d = q.shape
    nb = -(-S // MOBA_BLOCK)
    pad = nb * MOBA_BLOCK - S
    padw = ((0, 0), (0, pad), (0, 0), (0, 0))
    kb = jnp.pad(k, padw).reshape(B, nb, MOBA_BLOCK, H, d)
    vb = jnp.pad(v, padw).reshape(B, nb, MOBA_BLOCK, H, d)
    k_mean = jnp.mean(kb.astype(jnp.float32), axis=2)
    kbh = kb.transpose(0, 3, 1, 2, 4)
    vbh = vb.transpose(0, 3, 1, 2, 4)
    n_sel = min(MOBA_TOP_K, nb)
    nqb = S // MOBA_QUERY_BLOCK
    q_blocks = q.reshape(B, nqb, MOBA_QUERY_BLOCK, H, d).transpose(1, 0, 2, 3, 4)
    bi = jnp.arange(B)[:, None, None, None]
    hi = jnp.arange(H)[None, None, :, None]
    blk_ids = jnp.arange(nb)

    def step(args):
        qb, iq = args
        start = iq * MOBA_QUERY_BLOCK
        own = start // MOBA_BLOCK
        q_pos = start + jnp.arange(MOBA_QUERY_BLOCK)
        gate = jnp.einsum('bqhd,bnhd->bqhn', qb.astype(jnp.float32), k_mean)
        gate = jnp.where(blk_ids < own, gate, NEG_INF)
        _, idx = lax.top_k(gate, n_sel)
        sel_valid = idx < own
        k_sel = kbh[bi, hi, idx]
        v_sel = vbh[bi, hi, idx]
        k_own = lax.dynamic_index_in_dim(kb, own, axis=1, keepdims=False)
        v_own = lax.dynamic_index_in_dim(vb, own, axis=1, keepdims=False)
        own_valid = (own * MOBA_BLOCK + jnp.arange(MOBA_BLOCK))[None, :] <= q_pos[:, None]
        return block_sparse_attend(qb, k_sel, v_sel, sel_valid, k_own, v_own, own_valid)

    out = lax.map(step, (q_blocks, jnp.arange(nqb)))
    return out.transpose(1, 0, 2, 3, 4).reshape(B, S, H, d)


def moba_sample(q, k, v, cache_k, cache_v, page_table):
    DB, T, H, d = q.shape
    page = cache_k.shape[1]
    past = page_table.shape[1] * page
    ppb = MOBA_BLOCK // page
    n_full = past // MOBA_BLOCK
    n_rem = past - n_full * MOBA_BLOCK
    rem_pages = page_table[:, n_full * ppb:]
    k_own = jnp.concatenate([cache_k[rem_pages].reshape(DB, n_rem, H, d), k], axis=1)
    v_own = jnp.concatenate([cache_v[rem_pages].reshape(DB, n_rem, H, d), v], axis=1)
    n_own = n_rem + T
    n_sel = min(MOBA_TOP_K, n_full)
    bi = jnp.arange(DB)[:, None, None, None, None]
    hi = jnp.arange(H)[None, None, :, None, None]
    if n_full > 0:
        k_hist = cache_k[page_table[:, :n_full * ppb]].reshape(DB, n_full, MOBA_BLOCK, H, d)
        k_mean = jnp.mean(k_hist.astype(jnp.float32), axis=2)

    def step(args):
        qt, it = args
        qq = qt[:, None]
        own_valid = (jnp.arange(n_own) < n_rem + it + 1)[None, :]
        if n_full > 0:
            gate = jnp.einsum('bqhd,bnhd->bqhn', qq.astype(jnp.float32), k_mean)
            _, idx = lax.top_k(gate, n_sel)
            logical = idx[..., None] * ppb + jnp.arange(ppb)
            phys = page_table[bi, logical]
            k_sel = cache_k[phys, :, hi].reshape(DB, 1, H, n_sel, MOBA_BLOCK, d)
            v_sel = cache_v[phys, :, hi].reshape(DB, 1, H, n_sel, MOBA_BLOCK, d)
            sel_valid = jnp.ones((DB, 1, H, n_sel), dtype=bool)
        else:
            k_sel = jnp.zeros((DB, 1, H, 0, MOBA_BLOCK, d), q.dtype)
            v_sel = jnp.zeros((DB, 1, H, 0, MOBA_BLOCK, d), v.dtype)
            sel_valid = jnp.zeros((DB, 1, H, 0), dtype=bool)
        return block_sparse_attend(qq, k_sel, v_sel, sel_valid, k_own, v_own, own_valid)[:, 0]

    out = lax.map(step, (q.transpose(1, 0, 2, 3), jnp.arange(T)))
    return out.transpose(1, 0, 2, 3)


def retention_chunk(state, q, k, v):
    L = q.shape[2]
    lg = retention_log_decay()[:, None]
    i = jnp.arange(L, dtype=jnp.float32)
    diff = i[:, None] - i[None, :]
    decay = jnp.where(diff >= 0, jnp.exp(jnp.maximum(diff, 0.0)[None] * lg[..., None]), 0.0).astype(q.dtype)
    q_dec = q * jnp.exp((i + 1.0)[None] * lg).astype(q.dtype)[:, :, None]
    k_dec = k * jnp.exp((L - 1.0 - i)[None] * lg).astype(k.dtype)[:, :, None]
    scores = jnp.einsum('bhid,bhjd->bhij', q, k) * decay
    out = jnp.einsum('bhij,bhje->bhie', scores, v) + jnp.einsum('bhid,bhde->bhie', q_dec, state)
    new_state = jnp.exp(L * lg)[..., None].astype(state.dtype) * state + jnp.einsum('bhjd,bhje->bhde', k_dec, v)
    return out, new_state


def retention_prompt(q, k, v):
    B, S, H, dk = q.shape
    dv = v.shape[-1]
    nc = S // RET_CHUNK

    def to_chunks(t):
        return t.reshape(B, nc, RET_CHUNK, H, t.shape[-1]).transpose(1, 0, 3, 2, 4)

    def step(state, xs):
        out, new_state = retention_chunk(state, *xs)
        return new_state, out

    s0 = jnp.zeros((B, H, dk, dv), q.dtype)
    s_final, out = lax.scan(step, s0, (to_chunks(q), to_chunks(k), to_chunks(v)))
    return out.transpose(1, 0, 3, 2, 4).reshape(B, S, H, dv), s_final


def retention_step(q, k, v, state):
    tr = lambda t: t.transpose(0, 2, 1, 3)
    out, new_state = retention_chunk(state, tr(q), tr(k), tr(v))
    return tr(out), new_state


def mixer_inputs(hn, pos, w_in, q_norm_w, k_norm_w):
    B, T, _ = hn.shape
    proj = hn @ w_in
    A, R = ATTN_WIDTH, RET_WIDTH

    def heads(lo, width, n, d):
        return proj[..., lo:lo + width].reshape(B, T, n, d)

    qa = heads(0, A, ATTN_HEADS, ATTN_HEAD_DIM)
    ka = heads(A, A, ATTN_HEADS, ATTN_HEAD_DIM)
    va = heads(2 * A, A, ATTN_HEADS, ATTN_HEAD_DIM)
    o = 3 * A
    qr = heads(o, R, RET_HEADS, RET_HEAD_DIM)
    kr = heads(o + R, R, RET_HEADS, RET_HEAD_DIM)
    vr = heads(o + 2 * R, R, RET_HEADS, RET_HEAD_DIM)
    gr = proj[..., o + 3 * R:o + 4 * R]
    qa = partial_rope(rmsnorm(qa, q_norm_w), pos)
    ka = partial_rope(rmsnorm(ka, k_norm_w), pos)
    qr = retnet_rotate(qr, pos)
    kr = retnet_rotate(kr, pos) * (RET_HEAD_DIM ** -0.5)
    return qa, ka, va, qr, kr, vr, gr


def decoder_layer(x, c, pos, attend, retain, w_ada, b_ada, norm_ffn1_w, ffn1_w_gate, ffn1_w_up, ffn1_w_down,
                  norm_mix_w, w_in, q_norm_w, k_norm_w, w_out, norm_ffn2_w, ffn2_w_gate, ffn2_w_up, ffn2_w_down):
    B, T, _ = x.shape
    m = adaln_mods(c, w_ada, b_ada)
    h = macaron_half_ffn(x, norm_ffn1_w, m[0], m[1], m[2], ffn1_w_gate, ffn1_w_up, ffn1_w_down)
    hn = rmsnorm(h, norm_mix_w) * (1.0 + m[4]) + m[3]
    qa, ka, va, qr, kr, vr, gr = mixer_inputs(hn, pos, w_in, q_norm_w, k_norm_w)
    oa = attend(qa, ka, va)
    orr, ret_state = retain(qr, kr, vr)
    o_ret = head_rmsnorm(orr).reshape(B, T, RET_WIDTH) * jax.nn.silu(gr)
    mix = jnp.concatenate([oa.reshape(B, T, ATTN_WIDTH), o_ret], axis=-1) @ w_out
    h = h + m[5] * mix
    y = macaron_half_ffn(h, norm_ffn2_w, m[6], m[7], m[8], ffn2_w_gate, ffn2_w_up, ffn2_w_down)
    return y, ka, va, ret_state


def setup_inputs(seed: int = 0) -> dict:
    key = jax.random.key(seed)
    ks = iter(jax.random.split(key, 32))
    f32 = jnp.float32

    def nrm(shape, scale):
        return jax.random.normal(next(ks), shape, f32) * scale

    n_pages = PAST_LEN // PAGE_SIZE
    n_used = DEC_BATCH * n_pages
    n_phys = n_used + max(1, n_used // 4)
    page_table = jax.random.permutation(next(ks), n_phys)[:n_used].reshape(DEC_BATCH, n_pages).astype(jnp.int32)
    return {
        'x_prompt': nrm((BATCH, SEQ, D_MODEL), 1.0),
        'x_sample': nrm((DEC_BATCH, DEC_SEQ, D_MODEL), 1.0),
        'cache_k': nrm((DEPTH, n_phys, PAGE_SIZE, ATTN_HEADS, ATTN_HEAD_DIM), 1.0),
        'cache_v': nrm((DEPTH, n_phys, PAGE_SIZE, ATTN_HEADS, ATTN_HEAD_DIM), 1.0),
        'state_ret': nrm((DEPTH, DEC_BATCH, RET_HEADS, RET_HEAD_DIM, RET_HEAD_DIM), 0.5),
        'page_table': page_table,
        'c_prompt': nrm((BATCH, D_MODEL), 1.0),
        'c_sample': nrm((DEC_BATCH, D_MODEL), 1.0),
        'w_ada': nrm((DEPTH, D_MODEL, N_MODS * D_MODEL), D_MODEL ** -0.5),
        'b_ada': nrm((DEPTH, N_MODS * D_MODEL), 0.01),
        'norm_ffn1_w': 1.0 + nrm((DEPTH, D_MODEL), 0.1),
        'ffn1_w_gate': nrm((DEPTH, D_MODEL, D_FF), D_MODEL ** -0.5),
        'ffn1_w_up': nrm((DEPTH, D_MODEL, D_FF), D_MODEL ** -0.5),
        'ffn1_w_down': nrm((DEPTH, D_FF, D_MODEL), D_FF ** -0.5),
        'norm_mix_w': 1.0 + nrm((DEPTH, D_MODEL), 0.1),
        'w_in': nrm((DEPTH, D_MODEL, IN_WIDTH), D_MODEL ** -0.5),
        'q_norm_w': 1.0 + nrm((DEPTH, ATTN_HEAD_DIM), 0.1),
        'k_norm_w': 1.0 + nrm((DEPTH, ATTN_HEAD_DIM), 0.1),
        'w_out': nrm((DEPTH, MIX_WIDTH, D_MODEL), MIX_WIDTH ** -0.5),
        'norm_ffn2_w': 1.0 + nrm((DEPTH, D_MODEL), 0.1),
        'ffn2_w_gate': nrm((DEPTH, D_MODEL, D_FF), D_MODEL ** -0.5),
        'ffn2_w_up': nrm((DEPTH, D_MODEL, D_FF), D_MODEL ** -0.5),
        'ffn2_w_down': nrm((DEPTH, D_FF, D_MODEL), D_FF ** -0.5),
    }


def reference(x_prompt, x_sample, cache_k, cache_v, state_ret, page_table, c_prompt, c_sample, w_ada, b_ada,
              norm_ffn1_w, ffn1_w_gate, ffn1_w_up, ffn1_w_down, norm_mix_w, w_in, q_norm_w, k_norm_w, w_out,
              norm_ffn2_w, ffn2_w_gate, ffn2_w_up, ffn2_w_down):
    past_len = page_table.shape[1] * cache_k.shape[2]
    pos_prompt = jnp.arange(x_prompt.shape[1])
    pos_sample = past_len + jnp.arange(x_sample.shape[1])
    hp, hs = x_prompt, x_sample
    kp_l, vp_l, ks_l, vs_l, sp_l, ss_l = [], [], [], [], [], []
    for l in range(DEPTH):
        lw = (w_ada[l], b_ada[l], norm_ffn1_w[l], ffn1_w_gate[l], ffn1_w_up[l], ffn1_w_down[l], norm_mix_w[l],
              w_in[l], q_norm_w[l], k_norm_w[l], w_out[l], norm_ffn2_w[l], ffn2_w_gate[l], ffn2_w_up[l], ffn2_w_down[l])
        hp, kp, vp, sp = decoder_layer(hp, c_prompt, pos_prompt, moba_prompt, retention_prompt, *lw)
        attend_s = functools.partial(moba_sample, cache_k=cache_k[l], cache_v=cache_v[l], page_table=page_table)
        retain_s = functools.partial(retention_step, state=state_ret[l])
        hs, ks_, vs_, ss = decoder_layer(hs, c_sample, pos_sample, attend_s, retain_s, *lw)
        kp_l.append(kp)
        vp_l.append(vp)
        ks_l.append(ks_)
        vs_l.append(vs_)
        sp_l.append(sp)
        ss_l.append(ss)
    return (hp, hs, jnp.stack(kp_l), jnp.stack(vp_l), jnp.stack(ks_l), jnp.stack(vs_l), jnp.stack(sp_l), jnp.stack(ss_l))
```

```python
import functools

import jax
import jax.numpy as jnp
from jax import lax
from jax.experimental import pallas as pl
from jax.experimental.pallas import tpu as pltpu

F32 = jnp.float32
BF16 = jnp.bfloat16

D_MODEL = 1024
ATTN_HEADS = 8
ATTN_HEAD_DIM = 64
ATTN_WIDTH = ATTN_HEADS * ATTN_HEAD_DIM
RET_HEADS = 4
RET_HEAD_DIM = 128
RET_WIDTH = RET_HEADS * RET_HEAD_DIM
IN_WIDTH = 3 * ATTN_WIDTH + 4 * RET_WIDTH
MOBA_BLOCK = 256
MOBA_TOP_K = 3
ROPE_THETA = 500000.0
ROPE_DIMS = ATTN_HEAD_DIM // 4
RET_CHUNK = 128
RET_ROT_BASE = 10000.0
N_MODS = 9
EPS = 1e-6
NEG_INF = -1e30

LANES = 128
SUBLANES = 8
VMEM_LIMIT = 56 * 1024 * 1024
FF_CHUNK = 256


def _dot(a, b):
    return jnp.dot(a, b, preferred_element_type=F32)


def _dot_nt(a, b):
    return lax.dot_general(a, b, (((1,), (1,)), ((), ())), preferred_element_type=F32)


def _split(a):
    hi = a.astype(BF16)
    lo = (a - hi.astype(F32)).astype(BF16)
    return hi, lo


def _dot3(a, b):
    ah, al = _split(a)
    bh, bl = _split(b)
    return _dot(ah, bh) + (_dot(ah, bl) + _dot(al, bh))


def _dot3_nt(a, b):
    ah, al = _split(a)
    bh, bl = _split(b)
    return _dot_nt(ah, bh) + (_dot_nt(ah, bl) + _dot_nt(al, bh))


def _sigmoid(x):
    return 1.0 / (1.0 + jnp.exp(-x))


def _silu(x):
    return x * _sigmoid(x)


def _rms(x):
    return x * lax.rsqrt(jnp.mean(x * x, axis=-1, keepdims=True) + EPS)


def _top_k_lanes(g, lane_f, k):
    sel = jnp.zeros(g.shape, jnp.bool_)
    for _ in range(k):
        m = jnp.max(g, axis=-1, keepdims=True)
        idx = jnp.min(jnp.where(g == m, lane_f, 1e9), axis=-1, keepdims=True)
        pick = lane_f == idx
        sel = jnp.logical_or(sel, pick)
        g = jnp.where(pick, -jnp.inf, g)
    return sel


def _mods_kernel(c_ref, w_ref, b_ref, o_ref):
    s = _silu(c_ref[...]).astype(BF16)
    o_ref[...] = _dot(s, w_ref[...].astype(BF16)) + b_ref[...]


def _mods(c, w_ada, b_ada):
    n = c.shape[0]
    n_pad = -(-n // SUBLANES) * SUBLANES
    c = jnp.pad(c, ((0, n_pad - n), (0, 0)))
    width = w_ada.shape[1]
    tn = 9 * LANES
    out = pl.pallas_call(
        _mods_kernel,
        grid=(width // tn,),
        in_specs=[pl.BlockSpec((n_pad, D_MODEL), lambda j: (0, 0)),
                  pl.BlockSpec((D_MODEL, tn), lambda j: (0, j)),
                  pl.BlockSpec((1, tn), lambda j: (0, j))],
        out_specs=pl.BlockSpec((n_pad, tn), lambda j: (0, j)),
        out_shape=jax.ShapeDtypeStruct((n_pad, width), F32),
        name="mods",
    )(c, w_ada, b_ada.reshape(1, width))
    return out[:n]


def _swiglu_acc(xb, wg_ref, wu_ref, wd_ref):
    d_ff = wg_ref.shape[1]
    acc = None
    for c in range(d_ff // FF_CHUNK):
        sl = slice(c * FF_CHUNK, (c + 1) * FF_CHUNK)
        g = _dot(xb, wg_ref[:, sl])
        u = _dot(xb, wu_ref[:, sl])
        part = _dot((_silu(g) * u).astype(BF16), wd_ref[sl, :])
        acc = part if acc is None else acc + part
    return acc


def _const_spec(shape):
    nd = len(shape)
    return pl.BlockSpec(shape, lambda *_: (0,) * nd, pipeline_mode=pl.Buffered(1))


def _ffn_proj_kernel(x_ref, mod_ref, n1_ref, wg_ref, wu_ref, wd_ref, nm_ref, win_ref, qn_ref, kn_ref, bd_ref,
                     ca_ref, sa1_ref, sa2_ref, cr_ref, sr1_ref, sr2_ref,
                     h_ref, qa_ref, ka_ref, va_ref, qr_ref, kr_ref, vr_ref, gr_ref):
    x = x_ref[...]
    xn = (_rms(x) * n1_ref[...]) * (1.0 + mod_ref[0, 1]) + mod_ref[0, 0]
    acc = _swiglu_acc(xn.astype(BF16), wg_ref, wu_ref, wd_ref)
    h = x + 0.5 * mod_ref[0, 2] * acc
    h_ref[...] = h
    hn = (_rms(h) * nm_ref[...]) * (1.0 + mod_ref[0, 4]) + mod_ref[0, 3]
    hb = hn.astype(BF16)

    def seg(i):
        return _dot(hb, win_ref[:, i * ATTN_WIDTH:(i + 1) * ATTN_WIDTH])

    ca, sa1, sa2 = ca_ref[0], sa1_ref[0], sa2_ref[0]
    cr, sr1, sr2 = cr_ref[0], sr1_ref[0], sr2_ref[0]
    bd = bd_ref[...]

    def attn_head_norm_rope(p, w, o_ref):
        hi, lo = _split(p * p)
        ms = _dot(hi, bd) + _dot(lo, bd)
        pn = (p * lax.rsqrt(ms + EPS)) * w
        for g in range(ATTN_WIDTH // LANES):
            xg = pn[:, g * LANES:(g + 1) * LANES]
            o_ref[:, g * LANES:(g + 1) * LANES] = (
                xg * ca + pltpu.roll(xg, LANES - ROPE_DIMS // 2, 1) * sa1 + pltpu.roll(xg, ROPE_DIMS // 2, 1) * sa2)

    def ret_rotate(p, o_ref, scale):
        for g in range(RET_HEADS):
            xg = p[:, g * LANES:(g + 1) * LANES]
            r = xg * cr + pltpu.roll(xg, LANES - 1, 1) * sr1 + pltpu.roll(xg, 1, 1) * sr2
            o_ref[:, g * LANES:(g + 1) * LANES] = r if scale is None else r * scale

    attn_head_norm_rope(seg(0), qn_ref[...], qa_ref)
    attn_head_norm_rope(seg(1), kn_ref[...], ka_ref)
    va_ref[...] = seg(2)
    ret_rotate(seg(3), qr_ref, None)
    ret_rotate(seg(4), kr_ref, RET_HEAD_DIM ** -0.5)
    vr_ref[...] = seg(5)
    gr_ref[...] = seg(6)


def _ffn_proj(x2d, mods, tabs, lw, *, tm, mod_idx, tab_idx):
    n = x2d.shape[0]
    rm = mods.shape[2]
    rt = tabs[0].shape[1]
    d_ff = lw["wg1"].shape[1]
    row = lambda i: (i, 0)
    tab_spec = pl.BlockSpec((1, rt, LANES), lambda i: (tab_idx(i), 0, 0))
    in_specs = [
        pl.BlockSpec((tm, D_MODEL), row),
        pl.BlockSpec((1, N_MODS, rm, D_MODEL), lambda i: (mod_idx(i), 0, 0, 0)),
        _const_spec((1, D_MODEL)),
        _const_spec((D_MODEL, d_ff)), _const_spec((D_MODEL, d_ff)), _const_spec((d_ff, D_MODEL)),
        _const_spec((1, D_MODEL)),
        _const_spec((D_MODEL, IN_WIDTH)),
        _const_spec((1, ATTN_WIDTH)), _const_spec((1, ATTN_WIDTH)),
        _const_spec((ATTN_WIDTH, ATTN_WIDTH)),
    ] + [tab_spec] * 6
    half = jax.ShapeDtypeStruct((n, ATTN_WIDTH), F32)
    out_shape = [jax.ShapeDtypeStruct((n, D_MODEL), F32)] + [half] * 7
    out_specs = [pl.BlockSpec((tm, D_MODEL), row)] + [pl.BlockSpec((tm, ATTN_WIDTH), row)] * 7
    return pl.pallas_call(
        _ffn_proj_kernel,
        grid=(n // tm,),
        in_specs=in_specs,
        out_specs=out_specs,
        out_shape=out_shape,
        compiler_params=pltpu.CompilerParams(dimension_semantics=("arbitrary",), vmem_limit_bytes=VMEM_LIMIT),
        name="ffn_proj",
    )(x2d, mods, lw["n1"], lw["wg1"], lw["wu1"], lw["wd1"], lw["nm"], lw["win"], lw["qn"], lw["kn"], lw["bd"], *tabs)


def _out_ffn_kernel(h_ref, oa_ref, or_ref, mod_ref, wo_ref, n2_ref, wg_ref, wu_ref, wd_ref, y_ref):
    mix = _dot(oa_ref[...].astype(BF16), wo_ref[:ATTN_WIDTH, :]) + _dot(or_ref[...].astype(BF16), wo_ref[ATTN_WIDTH:, :])
    h = h_ref[...] + mod_ref[0, 5] * mix
    hn = (_rms(h) * n2_ref[...]) * (1.0 + mod_ref[0, 7]) + mod_ref[0, 6]
    acc = _swiglu_acc(hn.astype(BF16), wg_ref, wu_ref, wd_ref)
    y_ref[...] = h + 0.5 * mod_ref[0, 8] * acc


def _out_ffn(h2d, oa, o_ret, mods, lw, *, tm, mod_idx):
    n = h2d.shape[0]
    rm = mods.shape[2]
    d_ff = lw["wg2"].shape[1]
    row = lambda i: (i, 0)
    return pl.pallas_call(
        _out_ffn_kernel,
        grid=(n // tm,),
        in_specs=[
            pl.BlockSpec((tm, D_MODEL), row),
            pl.BlockSpec((tm, ATTN_WIDTH), row),
            pl.BlockSpec((tm, RET_WIDTH), row),
            pl.BlockSpec((1, N_MODS, rm, D_MODEL), lambda i: (mod_idx(i), 0, 0, 0)),
            _const_spec((D_MODEL, D_MODEL)),
            _const_spec((1, D_MODEL)),
            _const_spec((D_MODEL, d_ff)), _const_spec((D_MODEL, d_ff)), _const_spec((d_ff, D_MODEL)),
        ],
        out_specs=pl.BlockSpec((tm, D_MODEL), row),
        out_shape=jax.ShapeDtypeStruct((n, D_MODEL), F32),
        compiler_params=pltpu.CompilerParams(dimension_semantics=("arbitrary",), vmem_limit_bytes=VMEM_LIMIT),
        name="out_ffn",
    )(h2d, oa, o_ret, mods, lw["wo"], lw["n2"], lw["wg2"], lw["wu2"], lw["wd2"])


def _moba_prompt_kernel(q_ref, k_ref, v_ref, o_ref, kaug_ref, vt_ref, km_ref, acc_ref, *, nb):
    qi = pl.program_id(2)
    tq = MOBA_BLOCK
    hd = ATTN_HEAD_DIM
    lane = lax.broadcasted_iota(jnp.int32, (tq, LANES), 1)
    in_head = (lane < hd, lane >= hd)
    bias_off = (hd, 0)

    @pl.when(qi == 0)
    def _prepare():
        km_ref[...] = jnp.zeros(km_ref.shape, F32)
        for j in range(nb):
            kb = k_ref[j * tq:(j + 1) * tq, :]
            km_ref[hd + j:hd + j + 1, :] = jnp.mean(kb, axis=0, keepdims=True)
            for h in range(2):
                onehot = jnp.where(lane == bias_off[h] + j, 1.0, 0.0)
                kaug_ref[h, j] = jnp.where(in_head[h], kb, onehot).astype(BF16)
            vt_ref[j] = v_ref[j * tq:(j + 1) * tq, :].T.astype(BF16)

    q = q_ref[...]
    lane_f = lane.astype(F32)
    scale = hd ** -0.5
    qaug = []
    for h in range(2):
        qh = jnp.where(in_head[h], q, 0.0)
        off = bias_off[h]
        gate = _dot3_nt(qh, km_ref[hd - off:hd - off + LANES, :])
        valid = jnp.logical_and(lane >= off, lane < off + qi)
        sel = _top_k_lanes(jnp.where(valid, gate, NEG_INF), lane_f, MOBA_TOP_K)
        chosen = jnp.logical_or(jnp.logical_and(sel, valid), lane == off + qi)
        qaug.append(jnp.where(in_head[h], qh * scale, jnp.where(chosen, 0.0, NEG_INF)).astype(BF16))

    key_i = lax.broadcasted_iota(jnp.int32, (tq, tq), 0)
    qry_i = lax.broadcasted_iota(jnp.int32, (tq, tq), 1)
    causal = key_i <= qry_i
    stats = []
    for h in range(2):
        s = jnp.where(causal, _dot_nt(kaug_ref[h, qi], qaug[h]), NEG_INF)
        m = jnp.max(s, axis=0, keepdims=True)
        p = jnp.exp(s - m)
        acc_ref[h] = _dot(vt_ref[qi], p.astype(BF16))
        stats += [m, jnp.sum(p, axis=0, keepdims=True)]

    def body(j, carry):
        out = []
        for h in range(2):
            m, l = carry[2 * h], carry[2 * h + 1]
            s = _dot_nt(kaug_ref[h, j], qaug[h])
            m_new = jnp.maximum(m, jnp.max(s, axis=0, keepdims=True))
            alpha = jnp.exp(m - m_new)
            p = jnp.exp(s - m_new)
            acc_ref[h] = alpha * acc_ref[h] + _dot(vt_ref[j], p.astype(BF16))
            out += [m_new, alpha * l + jnp.sum(p, axis=0, keepdims=True)]
        return tuple(out)

    stats = lax.fori_loop(0, qi, body, tuple(stats))
    o_t = jnp.concatenate([acc_ref[0, :hd, :] * (1.0 / stats[1]), acc_ref[1, hd:, :] * (1.0 / stats[3])], axis=0)
    o_ref[...] = o_t.T


def _moba_prompt(qa, ka, va, batch, seq):
    tq = MOBA_BLOCK
    assert seq % tq == 0
    nb = seq // tq
    assert nb <= ATTN_HEAD_DIM, "block bias field holds at most 64 key blocks"
    n_pairs = ATTN_WIDTH // LANES
    kv_spec = pl.BlockSpec((seq, LANES), lambda b, hp, qi: (b, hp))
    q_spec = pl.BlockSpec((tq, LANES), lambda b, hp, qi: (b * nb + qi, hp))
    return pl.pallas_call(
        functools.partial(_moba_prompt_kernel, nb=nb),
        grid=(batch, n_pairs, nb),
        in_specs=[q_spec, kv_spec, kv_spec],
        out_specs=q_spec,
        out_shape=jax.ShapeDtypeStruct(qa.shape, F32),
        scratch_shapes=[
            pltpu.VMEM((2, nb, tq, LANES), BF16),
            pltpu.VMEM((nb, LANES, tq), BF16),
            pltpu.VMEM((ATTN_HEAD_DIM + LANES, LANES), F32),
            pltpu.VMEM((2, LANES, tq), F32),
        ],
        compiler_params=pltpu.CompilerParams(dimension_semantics=("arbitrary", "arbitrary", "arbitrary"),
                                             vmem_limit_bytes=VMEM_LIMIT),
        name="moba_prompt",
    )(qa, ka, va)


def _head_norm_gate(out, g):
    on = out * lax.rsqrt(jnp.mean(out * out, axis=-1, keepdims=True) + EPS)
    return on * _silu(g)


def _ret_prompt_kernel(q_ref, k_ref, v_ref, g_ref, dec_ref, qd_ref, kd_ref, cd_ref, o_ref, st_ref, s_ref):
    i = pl.program_id(1)
    c_len = RET_CHUNK

    @pl.when(i == 0)
    def _init():
        s_ref[...] = jnp.zeros(s_ref.shape, F32)

    for c in range(q_ref.shape[0] // c_len):
        rows = slice(c * c_len, (c + 1) * c_len)
        for h in range(RET_HEADS):
            cols = slice(h * LANES, (h + 1) * LANES)
            q, k, v = q_ref[rows, cols], k_ref[rows, cols], v_ref[rows, cols]
            vb = v.astype(BF16)
            sc = _dot_nt(q.astype(BF16), k.astype(BF16)) * dec_ref[h]
            st = s_ref[h]
            out = _dot(sc.astype(BF16), vb) + _dot((q * qd_ref[:, cols]).astype(BF16), st.astype(BF16))
            s_ref[h] = cd_ref[h] * st + _dot((k * kd_ref[:, cols]).T.astype(BF16), vb)
            o_ref[rows, cols] = _head_norm_gate(out, g_ref[rows, cols])

    @pl.when(i == pl.num_programs(1) - 1)
    def _flush():
        st_ref[0] = s_ref[...]


def _ret_tables(chunk_len):
    lg = jnp.log(1.0 - 2.0 ** (-5.0 - jnp.arange(RET_HEADS, dtype=F32)))
    i = jnp.arange(RET_CHUNK, dtype=F32)
    diff = i[:, None] - i[None, :]
    decay = jnp.where(diff >= 0, jnp.exp(jnp.maximum(diff, 0.0)[None] * lg[:, None, None]), 0.0)
    qd = jnp.exp((i + 1.0)[None] * lg[:, None])
    kd = jnp.exp((chunk_len - 1.0 - i)[None] * lg[:, None])
    widen = lambda t: jnp.repeat(t.T, RET_HEAD_DIM, axis=1)
    cd = jnp.broadcast_to(jnp.exp(chunk_len * lg)[:, None, None], (RET_HEADS, 1, LANES))
    return decay, widen(qd), widen(kd), cd


def _ret_prompt(qr, kr, vr, gr, batch, seq, *, tr):
    nt = seq // tr
    row = pl.BlockSpec((tr, RET_WIDTH), lambda b, i: (b * nt + i, 0))
    decay, qd, kd, cd = _ret_tables(RET_CHUNK)
    return pl.pallas_call(
        _ret_prompt_kernel,
        grid=(batch, nt),
        in_specs=[row, row, row, row,
                  _const_spec((RET_HEADS, RET_CHUNK, RET_CHUNK)),
                  _const_spec((RET_CHUNK, RET_WIDTH)), _const_spec((RET_CHUNK, RET_WIDTH)),
                  _const_spec((RET_HEADS, 1, LANES))],
        out_specs=[row, pl.BlockSpec((1, RET_HEADS, RET_HEAD_DIM, RET_HEAD_DIM), lambda b, i: (b, 0, 0, 0))],
        out_shape=[jax.ShapeDtypeStruct(qr.shape, F32),
                   jax.ShapeDtypeStruct((batch, RET_HEADS, RET_HEAD_DIM, RET_HEAD_DIM), F32)],
        scratch_shapes=[pltpu.VMEM((RET_HEADS, RET_HEAD_DIM, RET_HEAD_DIM), F32)],
        compiler_params=pltpu.CompilerParams(dimension_semantics=("arbitrary", "arbitrary")),
        name="ret_prompt",
    )(qr, kr, vr, gr, decay, qd, kd, cd)


def _ret_sample_kernel(q_ref, k_ref, v_ref, g_ref, st_ref, dec_ref, qd_ref, kd_ref, cd_ref, o_ref, ns_ref):
    tp = q_ref.shape[1]
    zeros = jnp.zeros((RET_CHUNK - tp, LANES), F32)
    for s in range(q_ref.shape[0]):
        for h in range(RET_HEADS):
            cols = slice(h * LANES, (h + 1) * LANES)
            q = q_ref[s, :, cols]
            k = jnp.concatenate([k_ref[s, :, cols], zeros], axis=0)
            v = jnp.concatenate([v_ref[s, :, cols], zeros], axis=0)
            vb = v.astype(BF16)
            sc = _dot_nt(q.astype(BF16), k.astype(BF16)) * dec_ref[h, :tp, :]
            st = st_ref[s, h]
            out = _dot(sc.astype(BF16), vb) + _dot((q * qd_ref[:tp, cols]).astype(BF16), st.astype(BF16))
            ns_ref[s, h] = cd_ref[h] * st + _dot((k * kd_ref[:, cols]).T.astype(BF16), vb)
            o_ref[s, :, cols] = _head_norm_gate(out, g_ref[s, :, cols])


def _ret_sample(qr, kr, vr, gr, state, step_len, *, ns):
    db, tp, _ = qr.shape
    tok = pl.BlockSpec((ns, tp, RET_WIDTH), lambda i: (i, 0, 0))
    st = pl.BlockSpec((ns, RET_HEADS, RET_HEAD_DIM, RET_HEAD_DIM), lambda i: (i, 0, 0, 0))
    decay, qd, kd, cd = _ret_tables(step_len)
    return pl.pallas_call(
        _ret_sample_kernel,
        grid=(db // ns,),
        in_specs=[tok, tok, tok, tok, st,
                  _const_spec((RET_HEADS, RET_CHUNK, RET_CHUNK)),
                  _const_spec((RET_CHUNK, RET_WIDTH)), _const_spec((RET_CHUNK, RET_WIDTH)),
                  _const_spec((RET_HEADS, 1, LANES))],
        out_specs=[tok, st],
        out_shape=[jax.ShapeDtypeStruct(qr.shape, F32), jax.ShapeDtypeStruct(state.shape, F32)],
        compiler_params=pltpu.CompilerParams(dimension_semantics=("arbitrary",)),
        name="ret_sample",
    )(qr, kr, vr, gr, state, decay, qd, kd, cd)


def _moba_sample_kernel(pt_ref, q_ref, kn_ref, vn_ref, *refs, n_pages, page, step_len):
    del pt_ref
    k_refs, v_refs, o_ref = refs[:n_pages], refs[n_pages:2 * n_pages], refs[2 * n_pages]
    hd = ATTN_HEAD_DIM
    ppb = MOBA_BLOCK // page
    n_full = n_pages // ppb
    rows = step_len * ATTN_HEADS
    q = q_ref[0]
    kn = kn_ref[0]
    vn = vn_ref[0]
    qrep = jnp.concatenate([jnp.broadcast_to(q[t:t + 1, :], (ATTN_HEADS, ATTN_WIDTH)) for t in range(step_len)], axis=0)
    r_i = lax.broadcasted_iota(jnp.int32, (rows, ATTN_WIDTH), 0)
    c_i = lax.broadcasted_iota(jnp.int32, (rows, ATTN_WIDTH), 1)
    own_head = (c_i // hd) == (r_i % ATTN_HEADS)
    qbd = jnp.where(own_head, qrep, 0.0)

    lane = lax.broadcasted_iota(jnp.int32, (rows, LANES), 1)
    lane_f = lane.astype(F32)
    kt = lambda p: k_refs[p][0].reshape(ATTN_WIDTH, page)
    vt = lambda p: v_refs[p][0].reshape(ATTN_WIDTH, page)

    gate = jnp.full((rows, LANES), NEG_INF, F32)
    for b in range(n_full):
        ksum = kt(b * ppb)
        for pp in range(1, ppb):
            ksum = ksum + kt(b * ppb + pp)
        g = jnp.sum(_dot3(qbd, ksum), axis=-1, keepdims=True) * (1.0 / MOBA_BLOCK)
        gate = jnp.where(lane == b, g, gate)
    sel = jnp.logical_and(_top_k_lanes(gate, lane_f, min(MOBA_TOP_K, n_full)), lane < n_full)
    bias = jnp.where(sel, 0.0, NEG_INF)

    qs = qbd * (hd ** -0.5)
    qsb = qs.astype(BF16)
    s_pages = []
    for p in range(n_pages):
        b_col = jnp.max(jnp.where(lane == p // ppb, bias, NEG_INF), axis=-1, keepdims=True)
        s_pages.append(_dot(qsb, kt(p).astype(BF16)) + b_col)
    tok_of_row = lax.broadcasted_iota(jnp.int32, (rows, 1), 0) // ATTN_HEADS
    s_own = []
    for t in range(step_len):
        s_t = jnp.sum(qs * kn[t:t + 1, :], axis=-1, keepdims=True)
        s_own.append(jnp.where(tok_of_row >= t, s_t, NEG_INF))

    m = functools.reduce(jnp.maximum, s_own)
    for s in s_pages:
        m = jnp.maximum(m, jnp.max(s, axis=-1, keepdims=True))
    l = jnp.zeros((rows, 1), F32)
    o = jnp.zeros((rows, ATTN_WIDTH), F32)
    for p in range(n_pages):
        e = jnp.exp(s_pages[p] - m)
        l = l + jnp.sum(e, axis=-1, keepdims=True)
        o = o + _dot_nt(e.astype(BF16), vt(p).astype(BF16))
    for t in range(step_len):
        e = jnp.exp(s_own[t] - m)
        l = l + e
        o = o + e * vn[t:t + 1, :]
    o = jnp.where(own_head, o, 0.0) * (1.0 / l)
    o_ref[0] = jnp.zeros(o_ref.shape[1:], F32)
    for t in range(step_len):
        o_ref[0, t:t + 1, :] = jnp.sum(o[t * ATTN_HEADS:(t + 1) * ATTN_HEADS, :], axis=0, keepdims=True)


def _moba_sample(qa, ka, va, cache_kt, cache_vt, page_table, step_len):
    db, tp, _ = qa.shape
    n_pages = page_table.shape[1]
    page = cache_kt.shape[-1]
    assert (n_pages * page) % MOBA_BLOCK == 0, "past length must be whole key blocks"
    assert MOBA_BLOCK % page == 0
    tok = pl.BlockSpec((1, tp, ATTN_WIDTH), lambda b, pt: (b, 0, 0))
    page_specs = [pl.BlockSpec((1, ATTN_HEADS, ATTN_HEAD_DIM, page), functools.partial(lambda b, pt, p: (pt[b, p], 0, 0, 0), p=p))
                  for p in range(n_pages)]
    grid_spec = pltpu.PrefetchScalarGridSpec(
        num_scalar_prefetch=1,
        grid=(db,),
        in_specs=[tok, tok, tok] + page_specs + page_specs,
        out_specs=tok,
    )
    return pl.pallas_call(
        functools.partial(_moba_sample_kernel, n_pages=n_pages, page=page, step_len=step_len),
        grid_spec=grid_spec,
        out_shape=jax.ShapeDtypeStruct(qa.shape, F32),
        compiler_params=pltpu.CompilerParams(dimension_semantics=("arbitrary",), vmem_limit_bytes=VMEM_LIMIT),
        name="moba_sample",
    )(page_table, qa, ka, va, *([cache_kt] * n_pages), *([cache_vt] * n_pages))


def _rope_tables(pos):
    half = ROPE_DIMS // 2
    inv = ROPE_THETA ** (-jnp.arange(half, dtype=F32) * 2.0 / ROPE_DIMS)
    ang = pos.astype(F32)[:, None] * inv[None, :]
    cos, sin = jnp.cos(ang), jnp.sin(ang)
    j = jnp.arange(LANES) % ATTN_HEAD_DIM
    first, second = j < half, jnp.logical_and(j >= half, j < ROPE_DIMS)
    cj, sj = cos[:, j % half], sin[:, j % half]
    return (jnp.where(jnp.logical_or(first, second), cj, 1.0),
            jnp.where(first, -sj, 0.0),
            jnp.where(second, sj, 0.0))


def _ret_rot_tables(pos):
    inv = 1.0 / (RET_ROT_BASE ** jnp.linspace(0.0, 1.0, RET_HEAD_DIM // 2, dtype=F32))
    ang = pos.astype(F32)[:, None] * inv[None, :]
    cos, sin = jnp.cos(ang), jnp.sin(ang)
    lane = jnp.arange(LANES)
    cl, sl = cos[:, lane // 2], sin[:, lane // 2]
    even = lane % 2 == 0
    return cl, jnp.where(even, -sl, 0.0), jnp.where(even, 0.0, sl)


def _layer_weights(l, norm_ffn1_w, ffn1_w_gate, ffn1_w_up, ffn1_w_down, norm_mix_w, w_in, q_norm_w, k_norm_w, w_out,
                   norm_ffn2_w, ffn2_w_gate, ffn2_w_up, ffn2_w_down):
    head = jnp.arange(ATTN_WIDTH) // ATTN_HEAD_DIM
    return {
        "n1": norm_ffn1_w[l][None], "nm": norm_mix_w[l][None], "n2": norm_ffn2_w[l][None],
        "wg1": ffn1_w_gate[l].astype(BF16), "wu1": ffn1_w_up[l].astype(BF16), "wd1": ffn1_w_down[l].astype(BF16),
        "wg2": ffn2_w_gate[l].astype(BF16), "wu2": ffn2_w_up[l].astype(BF16), "wd2": ffn2_w_down[l].astype(BF16),
        "win": w_in[l].astype(BF16), "wo": w_out[l].astype(BF16),
        "qn": jnp.tile(q_norm_w[l], ATTN_HEADS)[None], "kn": jnp.tile(k_norm_w[l], ATTN_HEADS)[None],
        "bd": jnp.where(head[:, None] == head[None, :], 1.0 / ATTN_HEAD_DIM, 0.0).astype(BF16),
    }


def kernel(x_prompt, x_sample, cache_k, cache_v, state_ret, page_table, c_prompt, c_sample, w_ada, b_ada, norm_ffn1_w, ffn1_w_gate, ffn1_w_up, ffn1_w_down, norm_mix_w, w_in, q_norm_w, k_norm_w, w_out, norm_ffn2_w, ffn2_w_gate, ffn2_w_up, ffn2_w_down):
    batch, seq, _ = x_prompt.shape
    db, step_len, _ = x_sample.shape
    depth = w_ada.shape[0]
    n_pages, page = page_table.shape[1], cache_k.shape[2]
    past_len = n_pages * page
    tm_p = 256
    tp = SUBLANES
    assert seq % tm_p == 0 and step_len <= tp and db % SUBLANES == 0

    tabs_p = [t.reshape(seq // tm_p, tm_p, LANES)
              for t in _rope_tables(jnp.arange(seq)) + _ret_rot_tables(jnp.arange(seq))]
    pos_s = past_len + jnp.arange(step_len)
    tabs_s = [t.reshape(step_len, 1, LANES) for t in _rope_tables(pos_s) + _ret_rot_tables(pos_s)]

    hp = x_prompt.reshape(batch * seq, D_MODEL)
    hs = x_sample.transpose(1, 0, 2).reshape(step_len * db, D_MODEL)
    outs = [[] for _ in range(6)]
    for l in range(depth):
        lw = _layer_weights(l, norm_ffn1_w, ffn1_w_gate, ffn1_w_up, ffn1_w_down, norm_mix_w, w_in, q_norm_w, k_norm_w,
                            w_out, norm_ffn2_w, ffn2_w_gate, ffn2_w_up, ffn2_w_down)
        mods = _mods(jnp.concatenate([c_prompt, c_sample], axis=0), w_ada[l], b_ada[l])
        mods_p = mods[:batch].reshape(batch, N_MODS, 1, D_MODEL)
        mods_s = mods[batch:].reshape(db, N_MODS, D_MODEL).transpose(1, 0, 2)[None]

        tiles_per_seq = seq // tm_p
        h1, qa, ka, va, qr, kr, vr, gr = _ffn_proj(hp, mods_p, tabs_p, lw, tm=tm_p,
                                                   mod_idx=lambda i: i // tiles_per_seq, tab_idx=lambda i: i % tiles_per_seq)
        oa = _moba_prompt(qa, ka, va, batch, seq)
        o_ret, st_p = _ret_prompt(qr, kr, vr, gr, batch, seq, tr=512)
        hp = _out_ffn(h1, oa, o_ret, mods_p, lw, tm=tm_p, mod_idx=lambda i: i // tiles_per_seq)

        h1s, qas, kas, vas, qrs, krs, vrs, grs = _ffn_proj(hs, mods_s, tabs_s, lw, tm=db,
                                                           mod_idx=lambda i: 0, tab_idx=lambda i: i)

        def seq_major(t):
            t = t.reshape(step_len, db, ATTN_WIDTH).transpose(1, 0, 2)
            return jnp.pad(t, ((0, 0), (0, tp - step_len), (0, 0)))

        def token_major(t):
            return t[:, :step_len].transpose(1, 0, 2).reshape(step_len * db, ATTN_WIDTH)

        cache_kt = cache_k[l].transpose(0, 2, 3, 1)
        cache_vt = cache_v[l].transpose(0, 2, 3, 1)
        oas = _moba_sample(seq_major(qas), seq_major(kas), seq_major(vas), cache_kt, cache_vt, page_table, step_len)
        o_rets, st_s = _ret_sample(seq_major(qrs), seq_major(krs), seq_major(vrs), seq_major(grs), state_ret[l],
                                   step_len, ns=SUBLANES)
        hs = _out_ffn(h1s, token_major(oas), token_major(o_rets), mods_s, lw, tm=db, mod_idx=lambda i: 0)

        outs[0].append(ka.reshape(batch, seq, ATTN_HEADS, ATTN_HEAD_DIM))
        outs[1].append(va.reshape(batch, seq, ATTN_HEADS, ATTN_HEAD_DIM))
        outs[2].append(kas.reshape(step_len, db, ATTN_HEADS, ATTN_HEAD_DIM).transpose(1, 0, 2, 3))
        outs[3].append(vas.reshape(step_len, db, ATTN_HEADS, ATTN_HEAD_DIM).transpose(1, 0, 2, 3))
        outs[4].append(st_p)
        outs[5].append(st_s)

    y_prompt = hp.reshape(batch, seq, D_MODEL)
    y_sample = hs.reshape(step_len, db, D_MODEL).transpose(1, 0, 2)
    return (y_prompt, y_sample) + tuple(jnp.stack(o) for o in outs)
```

```python
import functools

import jax
import jax.numpy as jnp
from jax import lax
from jax.experimental import pallas as pl
from jax.experimental.pallas import tpu as pltpu

F32 = jnp.float32
BF16 = jnp.bfloat16

D_MODEL = 1024
ATTN_HEADS = 8
ATTN_HEAD_DIM = 64
ATTN_WIDTH = ATTN_HEADS * ATTN_HEAD_DIM
RET_HEADS = 4
RET_HEAD_DIM = 128
RET_WIDTH = RET_HEADS * RET_HEAD_DIM
IN_WIDTH = 3 * ATTN_WIDTH + 4 * RET_WIDTH
MOBA_BLOCK = 256
MOBA_TOP_K = 3
ROPE_THETA = 500000.0
ROPE_DIMS = ATTN_HEAD_DIM // 4
RET_CHUNK = 128
RET_ROT_BASE = 10000.0
N_MODS = 9
EPS = 1e-6
NEG_INF = -1e30

LANES = 128
SUBLANES = 8
VMEM_LIMIT = 56 * 1024 * 1024
FF_CHUNK = 256
MOBA_GROUP = 4
LOG2_E = 1.4426950408889634


def _dot(a, b):
    return jnp.dot(a, b, preferred_element_type=F32)


def _dot_nt(a, b):
    return lax.dot_general(a, b, (((1,), (1,)), ((), ())), preferred_element_type=F32)


def _split(a):
    hi = a.astype(BF16)
    lo = (a - hi.astype(F32)).astype(BF16)
    return hi, lo


def _dot3(a, b):
    ah, al = _split(a)
    bh, bl = _split(b)
    return _dot(ah, bh) + (_dot(ah, bl) + _dot(al, bh))


def _dot3_nt(a, b):
    ah, al = _split(a)
    bh, bl = _split(b)
    return _dot_nt(ah, bh) + (_dot_nt(ah, bl) + _dot_nt(al, bh))


def _sigmoid(x):
    return 1.0 / (1.0 + jnp.exp(-x))


def _silu(x):
    return x * _sigmoid(x)


def _rms(x):
    return x * lax.rsqrt(jnp.mean(x * x, axis=-1, keepdims=True) + EPS)


def _top_k_lanes(g, lane_f, k):
    sel = jnp.zeros(g.shape, jnp.bool_)
    for _ in range(k):
        m = jnp.max(g, axis=-1, keepdims=True)
        idx = jnp.min(jnp.where(g == m, lane_f, 1e9), axis=-1, keepdims=True)
        pick = lane_f == idx
        sel = jnp.logical_or(sel, pick)
        g = jnp.where(pick, -jnp.inf, g)
    return sel


def _mods_kernel(c_ref, w_ref, b_ref, o_ref):
    s = _silu(c_ref[...]).astype(BF16)
    o_ref[...] = _dot(s, w_ref[...].astype(BF16)) + b_ref[...]


def _mods(c, w_ada, b_ada):
    n = c.shape[0]
    n_pad = -(-n // SUBLANES) * SUBLANES
    c = jnp.pad(c, ((0, n_pad - n), (0, 0)))
    width = w_ada.shape[1]
    tn = 9 * LANES
    out = pl.pallas_call(
        _mods_kernel,
        grid=(width // tn,),
        in_specs=[pl.BlockSpec((n_pad, D_MODEL), lambda j: (0, 0)),
                  pl.BlockSpec((D_MODEL, tn), lambda j: (0, j)),
                  pl.BlockSpec((1, tn), lambda j: (0, j))],
        out_specs=pl.BlockSpec((n_pad, tn), lambda j: (0, j)),
        out_shape=jax.ShapeDtypeStruct((n_pad, width), F32),
        name="mods",
    )(c, w_ada, b_ada.reshape(1, width))
    return out[:n]


def _swiglu_acc(xb, wg_ref, wu_ref, wd_ref):
    d_ff = wg_ref.shape[1]
    acc = None
    for c in range(d_ff // FF_CHUNK):
        sl = slice(c * FF_CHUNK, (c + 1) * FF_CHUNK)
        g = _dot(xb, wg_ref[:, sl])
        u = _dot(xb, wu_ref[:, sl])
        part = _dot((_silu(g) * u).astype(BF16), wd_ref[sl, :])
        acc = part if acc is None else acc + part
    return acc


def _const_spec(shape):
    nd = len(shape)
    return pl.BlockSpec(shape, lambda *_: (0,) * nd, pipeline_mode=pl.Buffered(1))


def _ffn_proj_kernel(x_ref, mod_ref, n1_ref, wg_ref, wu_ref, wd_ref, nm_ref, win_ref, qn_ref, kn_ref, bd_ref,
                     ca_ref, sa1_ref, sa2_ref, cr_ref, sr1_ref, sr2_ref,
                     h_ref, qa_ref, ka_ref, va_ref, qr_ref, kr_ref, vr_ref, gr_ref):
    x = x_ref[...]
    xn = (_rms(x) * n1_ref[...]) * (1.0 + mod_ref[0, 1]) + mod_ref[0, 0]
    acc = _swiglu_acc(xn.astype(BF16), wg_ref, wu_ref, wd_ref)
    h = x + 0.5 * mod_ref[0, 2] * acc
    h_ref[...] = h
    hn = (_rms(h) * nm_ref[...]) * (1.0 + mod_ref[0, 4]) + mod_ref[0, 3]
    hb = hn.astype(BF16)

    def seg(i):
        return _dot(hb, win_ref[:, i * ATTN_WIDTH:(i + 1) * ATTN_WIDTH])

    ca, sa1, sa2 = ca_ref[0], sa1_ref[0], sa2_ref[0]
    cr, sr1, sr2 = cr_ref[0], sr1_ref[0], sr2_ref[0]
    bd = bd_ref[...]

    def attn_head_norm_rope(p, w, o_ref):
        hi, lo = _split(p * p)
        ms = _dot(hi, bd) + _dot(lo, bd)
        pn = (p * lax.rsqrt(ms + EPS)) * w
        for g in range(ATTN_WIDTH // LANES):
            xg = pn[:, g * LANES:(g + 1) * LANES]
            o_ref[:, g * LANES:(g + 1) * LANES] = (
                xg * ca + pltpu.roll(xg, LANES - ROPE_DIMS // 2, 1) * sa1 + pltpu.roll(xg, ROPE_DIMS // 2, 1) * sa2)

    def ret_rotate(p, o_ref, scale):
        for g in range(RET_HEADS):
            xg = p[:, g * LANES:(g + 1) * LANES]
            r = xg * cr + pltpu.roll(xg, LANES - 1, 1) * sr1 + pltpu.roll(xg, 1, 1) * sr2
            o_ref[:, g * LANES:(g + 1) * LANES] = r if scale is None else r * scale

    attn_head_norm_rope(seg(0), qn_ref[...], qa_ref)
    attn_head_norm_rope(seg(1), kn_ref[...], ka_ref)
    va_ref[...] = seg(2)
    ret_rotate(seg(3), qr_ref, None)
    ret_rotate(seg(4), kr_ref, RET_HEAD_DIM ** -0.5)
    vr_ref[...] = seg(5)
    gr_ref[...] = seg(6)


def _ffn_proj(x2d, mods, tabs, lw, *, tm, mod_idx, tab_idx):
    n = x2d.shape[0]
    rm = mods.shape[2]
    rt = tabs[0].shape[1]
    d_ff = lw["wg1"].shape[1]
    row = lambda i: (i, 0)
    tab_spec = pl.BlockSpec((1, rt, LANES), lambda i: (tab_idx(i), 0, 0))
    in_specs = [
        pl.BlockSpec((tm, D_MODEL), row),
        pl.BlockSpec((1, N_MODS, rm, D_MODEL), lambda i: (mod_idx(i), 0, 0, 0)),
        _const_spec((1, D_MODEL)),
        _const_spec((D_MODEL, d_ff)), _const_spec((D_MODEL, d_ff)), _const_spec((d_ff, D_MODEL)),
        _const_spec((1, D_MODEL)),
        _const_spec((D_MODEL, IN_WIDTH)),
        _const_spec((1, ATTN_WIDTH)), _const_spec((1, ATTN_WIDTH)),
        _const_spec((ATTN_WIDTH, ATTN_WIDTH)),
    ] + [tab_spec] * 6
    half = jax.ShapeDtypeStruct((n, ATTN_WIDTH), F32)
    out_shape = [jax.ShapeDtypeStruct((n, D_MODEL), F32)] + [half] * 7
    out_specs = [pl.BlockSpec((tm, D_MODEL), row)] + [pl.BlockSpec((tm, ATTN_WIDTH), row)] * 7
    return pl.pallas_call(
        _ffn_proj_kernel,
        grid=(n // tm,),
        in_specs=in_specs,
        out_specs=out_specs,
        out_shape=out_shape,
        compiler_params=pltpu.CompilerParams(dimension_semantics=("arbitrary",), vmem_limit_bytes=VMEM_LIMIT),
        name="ffn_proj",
    )(x2d, mods, lw["n1"], lw["wg1"], lw["wu1"], lw["wd1"], lw["nm"], lw["win"], lw["qn"], lw["kn"], lw["bd"], *tabs)


def _out_ffn_kernel(h_ref, oa_ref, or_ref, mod_ref, wo_ref, n2_ref, wg_ref, wu_ref, wd_ref, y_ref):
    mix = _dot(oa_ref[...].astype(BF16), wo_ref[:ATTN_WIDTH, :]) + _dot(or_ref[...].astype(BF16), wo_ref[ATTN_WIDTH:, :])
    h = h_ref[...] + mod_ref[0, 5] * mix
    hn = (_rms(h) * n2_ref[...]) * (1.0 + mod_ref[0, 7]) + mod_ref[0, 6]
    acc = _swiglu_acc(hn.astype(BF16), wg_ref, wu_ref, wd_ref)
    y_ref[...] = h + 0.5 * mod_ref[0, 8] * acc


def _out_ffn(h2d, oa, o_ret, mods, lw, *, tm, mod_idx):
    n = h2d.shape[0]
    rm = mods.shape[2]
    d_ff = lw["wg2"].shape[1]
    row = lambda i: (i, 0)
    return pl.pallas_call(
        _out_ffn_kernel,
        grid=(n // tm,),
        in_specs=[
            pl.BlockSpec((tm, D_MODEL), row),
            pl.BlockSpec((tm, ATTN_WIDTH), row),
            pl.BlockSpec((tm, RET_WIDTH), row),
            pl.BlockSpec((1, N_MODS, rm, D_MODEL), lambda i: (mod_idx(i), 0, 0, 0)),
            _const_spec((D_MODEL, D_MODEL)),
            _const_spec((1, D_MODEL)),
            _const_spec((D_MODEL, d_ff)), _const_spec((D_MODEL, d_ff)), _const_spec((d_ff, D_MODEL)),
        ],
        out_specs=pl.BlockSpec((tm, D_MODEL), row),
        out_shape=jax.ShapeDtypeStruct((n, D_MODEL), F32),
        compiler_params=pltpu.CompilerParams(dimension_semantics=("arbitrary",), vmem_limit_bytes=VMEM_LIMIT),
        name="out_ffn",
    )(h2d, oa, o_ret, mods, lw["wo"], lw["n2"], lw["wg2"], lw["wu2"], lw["wd2"])


def _fold_rows(op, s):
    while s.shape[0] > SUBLANES:
        half = s.shape[0] // 2
        s = op(s[:half], s[half:])
    return s


def _col_max(s):
    return jnp.max(_fold_rows(jnp.maximum, s), axis=0, keepdims=True)


def _col_sum(s):
    return jnp.sum(_fold_rows(jnp.add, s), axis=0, keepdims=True)


def _moba_prompt_kernel(q_ref, k_ref, v_ref, o_ref, kaug_ref, vt_ref, km_ref, acc_ref, *, nb):
    qi = pl.program_id(2)
    tq = MOBA_BLOCK
    hd = ATTN_HEAD_DIM
    grp = MOBA_GROUP
    lane = lax.broadcasted_iota(jnp.int32, (tq, LANES), 1)
    in_head = (lane < hd, lane >= hd)
    bias_off = (hd, 0)

    @pl.when(qi == 0)
    def _prepare():
        km_ref[...] = jnp.zeros(km_ref.shape, F32)
        for j in range(nb):
            kb = k_ref[j * tq:(j + 1) * tq, :]
            km_ref[hd + j:hd + j + 1, :] = jnp.mean(kb, axis=0, keepdims=True)
            for h in range(2):
                onehot = jnp.where(lane == bias_off[h] + j, 1.0, 0.0)
                kaug_ref[h, j * tq:(j + 1) * tq, :] = jnp.where(in_head[h], kb, onehot).astype(BF16)
            vt_ref[j // grp, :, (j % grp) * tq:(j % grp + 1) * tq] = v_ref[j * tq:(j + 1) * tq, :].T.astype(BF16)

    q = q_ref[...]
    lane_f = lane.astype(F32)
    scale = hd ** -0.5 * LOG2_E
    q_own, q_past = [], []
    for h in range(2):
        qh = jnp.where(in_head[h], q, 0.0)
        off = bias_off[h]
        gate = _dot3_nt(qh, km_ref[hd - off:hd - off + LANES, :])
        valid = jnp.logical_and(lane >= off, lane < off + qi)
        sel = _top_k_lanes(jnp.where(valid, gate, NEG_INF), lane_f, MOBA_TOP_K)
        chosen = jnp.logical_and(sel, valid)
        q_own.append((qh * scale).astype(BF16))
        q_past.append(jnp.where(in_head[h], qh * scale, jnp.where(chosen, 0.0, NEG_INF)).astype(BF16))

    key_i = lax.broadcasted_iota(jnp.int32, (tq, tq), 0)
    qry_i = lax.broadcasted_iota(jnp.int32, (tq, tq), 1)
    causal = key_i <= qry_i
    own_rows = pl.ds(pl.multiple_of(qi * tq, tq), tq)
    v_own = v_ref[own_rows, :].T.astype(BF16)
    stats = []
    for h in range(2):
        s = jnp.where(causal, _dot_nt(kaug_ref[h, own_rows, :], q_own[h]), NEG_INF)
        m = _col_max(s)
        p = jnp.exp2(s - m)
        acc_ref[h] = _dot(v_own, p.astype(BF16))
        stats += [m, _col_sum(p)]

    def body(g, carry):
        out = []
        rows = pl.ds(pl.multiple_of(g * (grp * tq), grp * tq), grp * tq)
        scores = [_dot_nt(kaug_ref[h, rows, :], q_past[h]) for h in range(2)]
        probs, alphas = [], []
        for h in range(2):
            m, l = carry[2 * h], carry[2 * h + 1]
            m_new = jnp.maximum(m, _col_max(scores[h]))
            alpha = jnp.exp2(m - m_new)
            p = jnp.exp2(scores[h] - m_new)
            out += [m_new, alpha * l + _col_sum(p)]
            probs.append(p.astype(BF16))
            alphas.append(alpha)
        for h in range(2):
            acc_ref[h] = alphas[h] * acc_ref[h] + _dot(vt_ref[g], probs[h])
        return tuple(out)

    stats = lax.fori_loop(0, (qi + grp - 1) // grp, body, tuple(stats))
    o_t = jnp.concatenate([acc_ref[0, :hd, :] * (1.0 / stats[1]), acc_ref[1, hd:, :] * (1.0 / stats[3])], axis=0)
    o_ref[...] = o_t.T


def _moba_prompt(qa, ka, va, batch, seq):
    tq = MOBA_BLOCK
    assert seq % (tq * MOBA_GROUP) == 0
    nb = seq // tq
    assert nb <= ATTN_HEAD_DIM, "block bias field holds at most 64 key blocks"
    n_pairs = ATTN_WIDTH // LANES
    kv_spec = pl.BlockSpec((seq, LANES), lambda b, hp, qi: (b, hp))
    q_spec = pl.BlockSpec((tq, LANES), lambda b, hp, qi: (b * nb + qi, hp))
    return pl.pallas_call(
        functools.partial(_moba_prompt_kernel, nb=nb),
        grid=(batch, n_pairs, nb),
        in_specs=[q_spec, kv_spec, kv_spec],
        out_specs=q_spec,
        out_shape=jax.ShapeDtypeStruct(qa.shape, F32),
        scratch_shapes=[
            pltpu.VMEM((2, seq, LANES), BF16),
            pltpu.VMEM((nb // MOBA_GROUP, LANES, MOBA_GROUP * tq), BF16),
            pltpu.VMEM((ATTN_HEAD_DIM + LANES, LANES), F32),
            pltpu.VMEM((2, LANES, tq), F32),
        ],
        compiler_params=pltpu.CompilerParams(dimension_semantics=("arbitrary", "arbitrary", "arbitrary"),
                                             vmem_limit_bytes=VMEM_LIMIT),
        name="moba_prompt",
    )(qa, ka, va)


def _head_norm_gate(out, g):
    on = out * lax.rsqrt(jnp.mean(out * out, axis=-1, keepdims=True) + EPS)
    return on * _silu(g)


def _ret_prompt_kernel(q_ref, k_ref, v_ref, g_ref, dec_ref, qd_ref, kd_ref, cd_ref, o_ref, st_ref, s_ref):
    i = pl.program_id(1)
    c_len = RET_CHUNK

    @pl.when(i == 0)
    def _init():
        s_ref[...] = jnp.zeros(s_ref.shape, F32)

    for c in range(q_ref.shape[0] // c_len):
        rows = slice(c * c_len, (c + 1) * c_len)
        for h in range(RET_HEADS):
            cols = slice(h * LANES, (h + 1) * LANES)
            q, k, v = q_ref[rows, cols], k_ref[rows, cols], v_ref[rows, cols]
            vb = v.astype(BF16)
            sc = _dot_nt(q.astype(BF16), k.astype(BF16)) * dec_ref[h]
            st = s_ref[h]
            out = _dot(sc.astype(BF16), vb) + _dot((q * qd_ref[:, cols]).astype(BF16), st.astype(BF16))
            s_ref[h] = cd_ref[h] * st + _dot((k * kd_ref[:, cols]).T.astype(BF16), vb)
            o_ref[rows, cols] = _head_norm_gate(out, g_ref[rows, cols])

    @pl.when(i == pl.num_programs(1) - 1)
    def _flush():
        st_ref[0] = s_ref[...]


def _ret_tables(chunk_len):
    lg = jnp.log(1.0 - 2.0 ** (-5.0 - jnp.arange(RET_HEADS, dtype=F32)))
    i = jnp.arange(RET_CHUNK, dtype=F32)
    diff = i[:, None] - i[None, :]
    decay = jnp.where(diff >= 0, jnp.exp(jnp.maximum(diff, 0.0)[None] * lg[:, None, None]), 0.0)
    qd = jnp.exp((i + 1.0)[None] * lg[:, None])
    kd = jnp.exp((chunk_len - 1.0 - i)[None] * lg[:, None])
    widen = lambda t: jnp.repeat(t.T, RET_HEAD_DIM, axis=1)
    cd = jnp.broadcast_to(jnp.exp(chunk_len * lg)[:, None, None], (RET_HEADS, 1, LANES))
    return decay, widen(qd), widen(kd), cd


def _ret_prompt(qr, kr, vr, gr, batch, seq, *, tr):
    nt = seq // tr
    row = pl.BlockSpec((tr, RET_WIDTH), lambda b, i: (b * nt + i, 0))
    decay, qd, kd, cd = _ret_tables(RET_CHUNK)
    return pl.pallas_call(
        _ret_prompt_kernel,
        grid=(batch, nt),
        in_specs=[row, row, row, row,
                  _const_spec((RET_HEADS, RET_CHUNK, RET_CHUNK)),
                  _const_spec((RET_CHUNK, RET_WIDTH)), _const_spec((RET_CHUNK, RET_WIDTH)),
                  _const_spec((RET_HEADS, 1, LANES))],
        out_specs=[row, pl.BlockSpec((1, RET_HEADS, RET_HEAD_DIM, RET_HEAD_DIM), lambda b, i: (b, 0, 0, 0))],
        out_shape=[jax.ShapeDtypeStruct(qr.shape, F32),
                   jax.ShapeDtypeStruct((batch, RET_HEADS, RET_HEAD_DIM, RET_HEAD_DIM), F32)],
        scratch_shapes=[pltpu.VMEM((RET_HEADS, RET_HEAD_DIM, RET_HEAD_DIM), F32)],
        compiler_params=pltpu.CompilerParams(dimension_semantics=("arbitrary", "arbitrary")),
        name="ret_prompt",
    )(qr, kr, vr, gr, decay, qd, kd, cd)


def _ret_sample_kernel(q_ref, k_ref, v_ref, g_ref, st_ref, dec_ref, qd_ref, kd_ref, cd_ref, o_ref, ns_ref):
    tp = q_ref.shape[1]
    zeros = jnp.zeros((RET_CHUNK - tp, LANES), F32)
    for s in range(q_ref.shape[0]):
        for h in range(RET_HEADS):
            cols = slice(h * LANES, (h + 1) * LANES)
            q = q_ref[s, :, cols]
            k = jnp.concatenate([k_ref[s, :, cols], zeros], axis=0)
            v = jnp.concatenate([v_ref[s, :, cols], zeros], axis=0)
            vb = v.astype(BF16)
            sc = _dot_nt(q.astype(BF16), k.astype(BF16)) * dec_ref[h, :tp, :]
            st = st_ref[s, h]
            out = _dot(sc.astype(BF16), vb) + _dot((q * qd_ref[:tp, cols]).astype(BF16), st.astype(BF16))
            ns_ref[s, h] = cd_ref[h] * st + _dot((k * kd_ref[:, cols]).T.astype(BF16), vb)
            o_ref[s, :, cols] = _head_norm_gate(out, g_ref[s, :, cols])


def _ret_sample(qr, kr, vr, gr, state, step_len, *, ns):
    db, tp, _ = qr.shape
    tok = pl.BlockSpec((ns, tp, RET_WIDTH), lambda i: (i, 0, 0))
    st = pl.BlockSpec((ns, RET_HEADS, RET_HEAD_DIM, RET_HEAD_DIM), lambda i: (i, 0, 0, 0))
    decay, qd, kd, cd = _ret_tables(step_len)
    return pl.pallas_call(
        _ret_sample_kernel,
        grid=(db // ns,),
        in_specs=[tok, tok, tok, tok, st,
                  _const_spec((RET_HEADS, RET_CHUNK, RET_CHUNK)),
                  _const_spec((RET_CHUNK, RET_WIDTH)), _const_spec((RET_CHUNK, RET_WIDTH)),
                  _const_spec((RET_HEADS, 1, LANES))],
        out_specs=[tok, st],
        out_shape=[jax.ShapeDtypeStruct(qr.shape, F32), jax.ShapeDtypeStruct(state.shape, F32)],
        compiler_params=pltpu.CompilerParams(dimension_semantics=("arbitrary",)),
        name="ret_sample",
    )(qr, kr, vr, gr, state, decay, qd, kd, cd)


def _moba_sample_kernel(pt_ref, q_ref, kn_ref, vn_ref, *refs, n_pages, page, step_len):
    del pt_ref
    k_refs, v_refs, o_ref = refs[:n_pages], refs[n_pages:2 * n_pages], refs[2 * n_pages]
    hd = ATTN_HEAD_DIM
    ppb = MOBA_BLOCK // page
    n_full = n_pages // ppb
    rows = step_len * ATTN_HEADS
    q = q_ref[0]
    kn = kn_ref[0]
    vn = vn_ref[0]
    qrep = jnp.concatenate([jnp.broadcast_to(q[t:t + 1, :], (ATTN_HEADS, ATTN_WIDTH)) for t in range(step_len)], axis=0)
    r_i = lax.broadcasted_iota(jnp.int32, (rows, ATTN_WIDTH), 0)
    c_i = lax.broadcasted_iota(jnp.int32, (rows, ATTN_WIDTH), 1)
    own_head = (c_i // hd) == (r_i % ATTN_HEADS)
    qbd = jnp.where(own_head, qrep, 0.0)

    lane = lax.broadcasted_iota(jnp.int32, (rows, LANES), 1)
    lane_f = lane.astype(F32)
    kt = lambda p: k_refs[p][0].reshape(ATTN_WIDTH, page)
    vt = lambda p: v_refs[p][0].reshape(ATTN_WIDTH, page)

    gate = jnp.full((rows, LANES), NEG_INF, F32)
    for b in range(n_full):
        ksum = kt(b * ppb)
        for pp in range(1, ppb):
            ksum = ksum + kt(b * ppb + pp)
        g = jnp.sum(_dot3(qbd, ksum), axis=-1, keepdims=True) * (1.0 / MOBA_BLOCK)
        gate = jnp.where(lane == b, g, gate)
    sel = jnp.logical_and(_top_k_lanes(gate, lane_f, min(MOBA_TOP_K, n_full)), lane < n_full)
    bias = jnp.where(sel, 0.0, NEG_INF)

    qs = qbd * (hd ** -0.5)
    qsb = qs.astype(BF16)
    s_pages = []
    for p in range(n_pages):
        b_col = jnp.max(jnp.where(lane == p // ppb, bias, NEG_INF), axis=-1, keepdims=True)
        s_pages.append(_dot(qsb, kt(p).astype(BF16)) + b_col)
    tok_of_row = lax.broadcasted_iota(jnp.int32, (rows, 1), 0) // ATTN_HEADS
    s_own = []
    for t in range(step_len):
        s_t = jnp.sum(qs * kn[t:t + 1, :], axis=-1, keepdims=True)
        s_own.append(jnp.where(tok_of_row >= t, s_t, NEG_INF))

    m = functools.reduce(jnp.maximum, s_own)
    for s in s_pages:
        m = jnp.maximum(m, jnp.max(s, axis=-1, keepdims=True))
    l = jnp.zeros((rows, 1), F32)
    o = jnp.zeros((rows, ATTN_WIDTH), F32)
    for p in range(n_pages):
        e = jnp.exp(s_pages[p] - m)
        l = l + jnp.sum(e, axis=-1, keepdims=True)
        o = o + _dot_nt(e.astype(BF16), vt(p).astype(BF16))
    for t in range(step_len):
        e = jnp.exp(s_own[t] - m)
        l = l + e
        o = o + e * vn[t:t + 1, :]
    o = jnp.where(own_head, o, 0.0) * (1.0 / l)
    o_ref[0] = jnp.zeros(o_ref.shape[1:], F32)
    for t in range(step_len):
        o_ref[0, t:t + 1, :] = jnp.sum(o[t * ATTN_HEADS:(t + 1) * ATTN_HEADS, :], axis=0, keepdims=True)


def _moba_sample(qa, ka, va, cache_kt, cache_vt, page_table, step_len):
    db, tp, _ = qa.shape
    n_pages = page_table.shape[1]
    page = cache_kt.shape[-1]
    assert (n_pages * page) % MOBA_BLOCK == 0, "past length must be whole key blocks"
    assert MOBA_BLOCK % page == 0
    tok = pl.BlockSpec((1, tp, ATTN_WIDTH), lambda b, pt: (b, 0, 0))
    page_specs = [pl.BlockSpec((1, ATTN_HEADS, ATTN_HEAD_DIM, page), functools.partial(lambda b, pt, p: (pt[b, p], 0, 0, 0), p=p))
                  for p in range(n_pages)]
    grid_spec = pltpu.PrefetchScalarGridSpec(
        num_scalar_prefetch=1,
        grid=(db,),
        in_specs=[tok, tok, tok] + page_specs + page_specs,
        out_specs=tok,
    )
    return pl.pallas_call(
        functools.partial(_moba_sample_kernel, n_pages=n_pages, page=page, step_len=step_len),
        grid_spec=grid_spec,
        out_shape=jax.ShapeDtypeStruct(qa.shape, F32),
        compiler_params=pltpu.CompilerParams(dimension_semantics=("arbitrary",), vmem_limit_bytes=VMEM_LIMIT),
        name="moba_sample",
    )(page_table, qa, ka, va, *([cache_kt] * n_pages), *([cache_vt] * n_pages))


def _rope_tables(pos):
    half = ROPE_DIMS // 2
    inv = ROPE_THETA ** (-jnp.arange(half, dtype=F32) * 2.0 / ROPE_DIMS)
    ang = pos.astype(F32)[:, None] * inv[None, :]
    cos, sin = jnp.cos(ang), jnp.sin(ang)
    j = jnp.arange(LANES) % ATTN_HEAD_DIM
    first, second = j < half, jnp.logical_and(j >= half, j < ROPE_DIMS)
    cj, sj = cos[:, j % half], sin[:, j % half]
    return (jnp.where(jnp.logical_or(first, second), cj, 1.0),
            jnp.where(first, -sj, 0.0),
            jnp.where(second, sj, 0.0))


def _ret_rot_tables(pos):
    inv = 1.0 / (RET_ROT_BASE ** jnp.linspace(0.0, 1.0, RET_HEAD_DIM // 2, dtype=F32))
    ang = pos.astype(F32)[:, None] * inv[None, :]
    cos, sin = jnp.cos(ang), jnp.sin(ang)
    lane = jnp.arange(LANES)
    cl, sl = cos[:, lane // 2], sin[:, lane // 2]
    even = lane % 2 == 0
    return cl, jnp.where(even, -sl, 0.0), jnp.where(even, 0.0, sl)


def _layer_weights(l, norm_ffn1_w, ffn1_w_gate, ffn1_w_up, ffn1_w_down, norm_mix_w, w_in, q_norm_w, k_norm_w, w_out,
                   norm_ffn2_w, ffn2_w_gate, ffn2_w_up, ffn2_w_down):
    head = jnp.arange(ATTN_WIDTH) // ATTN_HEAD_DIM
    return {
        "n1": norm_ffn1_w[l][None], "nm": norm_mix_w[l][None], "n2": norm_ffn2_w[l][None],
        "wg1": ffn1_w_gate[l].astype(BF16), "wu1": ffn1_w_up[l].astype(BF16), "wd1": ffn1_w_down[l].astype(BF16),
        "wg2": ffn2_w_gate[l].astype(BF16), "wu2": ffn2_w_up[l].astype(BF16), "wd2": ffn2_w_down[l].astype(BF16),
        "win": w_in[l].astype(BF16), "wo": w_out[l].astype(BF16),
        "qn": jnp.tile(q_norm_w[l], ATTN_HEADS)[None], "kn": jnp.tile(k_norm_w[l], ATTN_HEADS)[None],
        "bd": jnp.where(head[:, None] == head[None, :], 1.0 / ATTN_HEAD_DIM, 0.0).astype(BF16),
    }


def kernel(x_prompt, x_sample, cache_k, cache_v, state_ret, page_table, c_prompt, c_sample, w_ada, b_ada, norm_ffn1_w, ffn1_w_gate, ffn1_w_up, ffn1_w_down, norm_mix_w, w_in, q_norm_w, k_norm_w, w_out, norm_ffn2_w, ffn2_w_gate, ffn2_w_up, ffn2_w_down):
    batch, seq, _ = x_prompt.shape
    db, step_len, _ = x_sample.shape
    depth = w_ada.shape[0]
    n_pages, page = page_table.shape[1], cache_k.shape[2]
    past_len = n_pages * page
    tm_p = 256
    tp = SUBLANES
    assert seq % tm_p == 0 and step_len <= tp and db % SUBLANES == 0

    tabs_p = [t.reshape(seq // tm_p, tm_p, LANES)
              for t in _rope_tables(jnp.arange(seq)) + _ret_rot_tables(jnp.arange(seq))]
    pos_s = past_len + jnp.arange(step_len)
    tabs_s = [t.reshape(step_len, 1, LANES) for t in _rope_tables(pos_s) + _ret_rot_tables(pos_s)]

    hp = x_prompt.reshape(batch * seq, D_MODEL)
    hs = x_sample.transpose(1, 0, 2).reshape(step_len * db, D_MODEL)
    outs = [[] for _ in range(6)]
    for l in range(depth):
        lw = _layer_weights(l, norm_ffn1_w, ffn1_w_gate, ffn1_w_up, ffn1_w_down, norm_mix_w, w_in, q_norm_w, k_norm_w,
                            w_out, norm_ffn2_w, ffn2_w_gate, ffn2_w_up, ffn2_w_down)
        mods = _mods(jnp.concatenate([c_prompt, c_sample], axis=0), w_ada[l], b_ada[l])
        mods_p = mods[:batch].reshape(batch, N_MODS, 1, D_MODEL)
        mods_s = mods[batch:].reshape(db, N_MODS, D_MODEL).transpose(1, 0, 2)[None]

        tiles_per_seq = seq // tm_p
        h1, qa, ka, va, qr, kr, vr, gr = _ffn_proj(hp, mods_p, tabs_p, lw, tm=tm_p,
                                                   mod_idx=lambda i: i // tiles_per_seq, tab_idx=lambda i: i % tiles_per_seq)
        oa = _moba_prompt(qa, ka, va, batch, seq)
        o_ret, st_p = _ret_prompt(qr, kr, vr, gr, batch, seq, tr=512)
        hp = _out_ffn(h1, oa, o_ret, mods_p, lw, tm=tm_p, mod_idx=lambda i: i // tiles_per_seq)

        h1s, qas, kas, vas, qrs, krs, vrs, grs = _ffn_proj(hs, mods_s, tabs_s, lw, tm=db,
                                                           mod_idx=lambda i: 0, tab_idx=lambda i: i)

        def seq_major(t):
            t = t.reshape(step_len, db, ATTN_WIDTH).transpose(1, 0, 2)
            return jnp.pad(t, ((0, 0), (0, tp - step_len), (0, 0)))

        def token_major(t):
            return t[:, :step_len].transpose(1, 0, 2).reshape(step_len * db, ATTN_WIDTH)

        cache_kt = cache_k[l].transpose(0, 2, 3, 1)
        cache_vt = cache_v[l].transpose(0, 2, 3, 1)
        oas = _moba_sample(seq_major(qas), seq_major(kas), seq_major(vas), cache_kt, cache_vt, page_table, step_len)
        o_rets, st_s = _ret_sample(seq_major(qrs), seq_major(krs), seq_major(vrs), seq_major(grs), state_ret[l],
                                   step_len, ns=SUBLANES)
        hs = _out_ffn(h1s, token_major(oas), token_major(o_rets), mods_s, lw, tm=db, mod_idx=lambda i: 0)

        outs[0].append(ka.reshape(batch, seq, ATTN_HEADS, ATTN_HEAD_DIM))
        outs[1].append(va.reshape(batch, seq, ATTN_HEADS, ATTN_HEAD_DIM))
        outs[2].append(kas.reshape(step_len, db, ATTN_HEADS, ATTN_HEAD_DIM).transpose(1, 0, 2, 3))
        outs[3].append(vas.reshape(step_len, db, ATTN_HEADS, ATTN_HEAD_DIM).transpose(1, 0, 2, 3))
        outs[4].append(st_p)
        outs[5].append(st_s)

    y_prompt = hp.reshape(batch, seq, D_MODEL)
    y_sample = hs.reshape(step_len, db, D_MODEL).transpose(1, 0, 2)
    return (y_prompt, y_sample) + tuple(jnp.stack(o) for o in outs)
```

```python
import functools

import jax
import jax.numpy as jnp
from jax import lax
from jax.experimental import pallas as pl
from jax.experimental.pallas import tpu as pltpu

F32 = jnp.float32
BF16 = jnp.bfloat16

D_MODEL = 1024
ATTN_HEADS = 8
ATTN_HEAD_DIM = 64
ATTN_WIDTH = ATTN_HEADS * ATTN_HEAD_DIM
RET_HEADS = 4
RET_HEAD_DIM = 128
RET_WIDTH = RET_HEADS * RET_HEAD_DIM
IN_WIDTH = 3 * ATTN_WIDTH + 4 * RET_WIDTH
MOBA_BLOCK = 256
MOBA_TOP_K = 3
ROPE_THETA = 500000.0
ROPE_DIMS = ATTN_HEAD_DIM // 4
RET_CHUNK = 128
RET_ROT_BASE = 10000.0
N_MODS = 9
EPS = 1e-6
NEG_INF = -1e30

LANES = 128
SUBLANES = 8
VMEM_LIMIT = 56 * 1024 * 1024
FF_CHUNK = 256
MOBA_GROUP = 4
LOG2_E = 1.4426950408889634
SHIFT_LANE = 63
SHIFT_LIMIT = 60.0


def _dot(a, b):
    return jnp.dot(a, b, preferred_element_type=F32)


def _dot_nt(a, b):
    return lax.dot_general(a, b, (((1,), (1,)), ((), ())), preferred_element_type=F32)


def _split(a):
    hi = a.astype(BF16)
    lo = (a - hi.astype(F32)).astype(BF16)
    return hi, lo


def _dot3(a, b):
    ah, al = _split(a)
    bh, bl = _split(b)
    return _dot(ah, bh) + (_dot(ah, bl) + _dot(al, bh))


def _dot3_nt(a, b):
    ah, al = _split(a)
    bh, bl = _split(b)
    return _dot_nt(ah, bh) + (_dot_nt(ah, bl) + _dot_nt(al, bh))


def _sigmoid(x):
    return 1.0 / (1.0 + jnp.exp(-x))


def _silu(x):
    return x * _sigmoid(x)


def _rms(x):
    return x * lax.rsqrt(jnp.mean(x * x, axis=-1, keepdims=True) + EPS)


def _top_k_lanes(g, lane_f, k):
    sel = jnp.zeros(g.shape, jnp.bool_)
    for _ in range(k):
        m = jnp.max(g, axis=-1, keepdims=True)
        idx = jnp.min(jnp.where(g == m, lane_f, 1e9), axis=-1, keepdims=True)
        pick = lane_f == idx
        sel = jnp.logical_or(sel, pick)
        g = jnp.where(pick, -jnp.inf, g)
    return sel


def _fold_rows(op, s):
    while s.shape[0] > SUBLANES:
        half = s.shape[0] // 2
        s = op(s[:half], s[half:])
    return s


def _col_max(s):
    return jnp.max(_fold_rows(jnp.maximum, s), axis=0, keepdims=True)


def _col_min(s):
    return jnp.min(_fold_rows(jnp.minimum, s), axis=0, keepdims=True)


def _top_k_rows(g, row_f, k):
    sel = jnp.zeros(g.shape, jnp.bool_)
    for _ in range(k):
        m = _col_max(g)
        idx = _col_min(jnp.where(g == m, row_f, 1e9))
        pick = row_f == idx
        sel = jnp.logical_or(sel, pick)
        g = jnp.where(pick, -jnp.inf, g)
    return sel


def _mods_kernel(c_ref, w_ref, b_ref, o_ref):
    s = _silu(c_ref[...]).astype(BF16)
    o_ref[...] = _dot(s, w_ref[...].astype(BF16)) + b_ref[...]


def _mods(c, w_ada, b_ada):
    n = c.shape[0]
    n_pad = -(-n // SUBLANES) * SUBLANES
    c = jnp.pad(c, ((0, n_pad - n), (0, 0)))
    width = w_ada.shape[1]
    tn = 9 * LANES
    out = pl.pallas_call(
        _mods_kernel,
        grid=(width // tn,),
        in_specs=[pl.BlockSpec((n_pad, D_MODEL), lambda j: (0, 0)),
                  pl.BlockSpec((D_MODEL, tn), lambda j: (0, j)),
                  pl.BlockSpec((1, tn), lambda j: (0, j))],
        out_specs=pl.BlockSpec((n_pad, tn), lambda j: (0, j)),
        out_shape=jax.ShapeDtypeStruct((n_pad, width), F32),
        name="mods",
    )(c, w_ada, b_ada.reshape(1, width))
    return out[:n]


def _swiglu_acc(xb, wg_ref, wu_ref, wd_ref):
    d_ff = wg_ref.shape[1]
    acc = None
    for c in range(d_ff // FF_CHUNK):
        sl = slice(c * FF_CHUNK, (c + 1) * FF_CHUNK)
        g = _dot(xb, wg_ref[:, sl])
        u = _dot(xb, wu_ref[:, sl])
        part = _dot((_silu(g) * u).astype(BF16), wd_ref[sl, :])
        acc = part if acc is None else acc + part
    return acc


def _const_spec(shape):
    nd = len(shape)
    return pl.BlockSpec(shape, lambda *_: (0,) * nd, pipeline_mode=pl.Buffered(1))


def _ffn_proj_kernel(x_ref, mod_ref, n1_ref, wg_ref, wu_ref, wd_ref, nm_ref, win_ref, qn_ref, kn_ref, bd_ref,
                     ca_ref, sa1_ref, sa2_ref, cr_ref, sr1_ref, sr2_ref,
                     h_ref, qa_ref, ka_ref, va_ref, qr_ref, kr_ref, vr_ref, gr_ref):
    x = x_ref[...]
    xn = (_rms(x) * n1_ref[...]) * (1.0 + mod_ref[0, 1]) + mod_ref[0, 0]
    acc = _swiglu_acc(xn.astype(BF16), wg_ref, wu_ref, wd_ref)
    h = x + 0.5 * mod_ref[0, 2] * acc
    h_ref[...] = h
    hn = (_rms(h) * nm_ref[...]) * (1.0 + mod_ref[0, 4]) + mod_ref[0, 3]
    hb = hn.astype(BF16)

    def seg(i):
        return _dot(hb, win_ref[:, i * ATTN_WIDTH:(i + 1) * ATTN_WIDTH])

    ca, sa1, sa2 = ca_ref[0], sa1_ref[0], sa2_ref[0]
    cr, sr1, sr2 = cr_ref[0], sr1_ref[0], sr2_ref[0]
    bd = bd_ref[...]

    def attn_head_norm_rope(p, w, o_ref, transposed):
        hi, lo = _split(p * p)
        ms = _dot(hi, bd) + _dot(lo, bd)
        pn = (p * lax.rsqrt(ms + EPS)) * w
        for g in range(ATTN_WIDTH // LANES):
            xg = pn[:, g * LANES:(g + 1) * LANES]
            r = xg * ca + pltpu.roll(xg, LANES - ROPE_DIMS // 2, 1) * sa1 + pltpu.roll(xg, ROPE_DIMS // 2, 1) * sa2
            if transposed:
                o_ref[0, g * LANES:(g + 1) * LANES, :] = r.T
            else:
                o_ref[:, g * LANES:(g + 1) * LANES] = r

    def ret_rotate(p, o_ref, scale):
        for g in range(RET_HEADS):
            xg = p[:, g * LANES:(g + 1) * LANES]
            r = xg * cr + pltpu.roll(xg, LANES - 1, 1) * sr1 + pltpu.roll(xg, 1, 1) * sr2
            o_ref[:, g * LANES:(g + 1) * LANES] = r if scale is None else r * scale

    attn_head_norm_rope(seg(0), qn_ref[...], qa_ref, False)
    attn_head_norm_rope(seg(1), kn_ref[...], ka_ref, True)
    va = seg(2)
    for g in range(ATTN_WIDTH // LANES):
        va_ref[0, g * LANES:(g + 1) * LANES, :] = va[:, g * LANES:(g + 1) * LANES].T
    ret_rotate(seg(3), qr_ref, None)
    ret_rotate(seg(4), kr_ref, RET_HEAD_DIM ** -0.5)
    vr_ref[...] = seg(5)
    gr_ref[...] = seg(6)


def _ffn_proj(x2d, mods, tabs, lw, *, tm, mod_idx, tab_idx, kv_groups, kv_idx):
    n = x2d.shape[0]
    rm = mods.shape[2]
    rt = tabs[0].shape[1]
    d_ff = lw["wg1"].shape[1]
    row = lambda i: (i, 0)
    tab_spec = pl.BlockSpec((1, rt, LANES), lambda i: (tab_idx(i), 0, 0))
    in_specs = [
        pl.BlockSpec((tm, D_MODEL), row),
        pl.BlockSpec((1, N_MODS, rm, D_MODEL), lambda i: (mod_idx(i), 0, 0, 0)),
        _const_spec((1, D_MODEL)),
        _const_spec((D_MODEL, d_ff)), _const_spec((D_MODEL, d_ff)), _const_spec((d_ff, D_MODEL)),
        _const_spec((1, D_MODEL)),
        _const_spec((D_MODEL, IN_WIDTH)),
        _const_spec((1, ATTN_WIDTH)), _const_spec((1, ATTN_WIDTH)),
        _const_spec((ATTN_WIDTH, ATTN_WIDTH)),
    ] + [tab_spec] * 6
    half = jax.ShapeDtypeStruct((n, ATTN_WIDTH), F32)
    half_t = jax.ShapeDtypeStruct((kv_groups, ATTN_WIDTH, n // kv_groups), F32)
    half_spec = pl.BlockSpec((tm, ATTN_WIDTH), row)
    half_t_spec = pl.BlockSpec((1, ATTN_WIDTH, tm), lambda i: kv_idx(i))
    out_shape = [jax.ShapeDtypeStruct((n, D_MODEL), F32), half, half_t, half_t] + [half] * 4
    out_specs = [pl.BlockSpec((tm, D_MODEL), row), half_spec, half_t_spec, half_t_spec] + [half_spec] * 4
    return pl.pallas_call(
        _ffn_proj_kernel,
        grid=(n // tm,),
        in_specs=in_specs,
        out_specs=out_specs,
        out_shape=out_shape,
        compiler_params=pltpu.CompilerParams(dimension_semantics=("arbitrary",), vmem_limit_bytes=VMEM_LIMIT),
        name="ffn_proj",
    )(x2d, mods, lw["n1"], lw["wg1"], lw["wu1"], lw["wd1"], lw["nm"], lw["win"], lw["qn"], lw["kn"], lw["bd"], *tabs)


def _out_ffn_kernel(h_ref, oa_ref, or_ref, mod_ref, wo_ref, n2_ref, wg_ref, wu_ref, wd_ref, y_ref):
    mix = _dot(oa_ref[...].astype(BF16), wo_ref[:ATTN_WIDTH, :]) + _dot(or_ref[...].astype(BF16), wo_ref[ATTN_WIDTH:, :])
    h = h_ref[...] + mod_ref[0, 5] * mix
    hn = (_rms(h) * n2_ref[...]) * (1.0 + mod_ref[0, 7]) + mod_ref[0, 6]
    acc = _swiglu_acc(hn.astype(BF16), wg_ref, wu_ref, wd_ref)
    y_ref[...] = h + 0.5 * mod_ref[0, 8] * acc


def _out_ffn(h2d, oa, o_ret, mods, lw, *, tm, mod_idx):
    n = h2d.shape[0]
    rm = mods.shape[2]
    d_ff = lw["wg2"].shape[1]
    row = lambda i: (i, 0)
    return pl.pallas_call(
        _out_ffn_kernel,
        grid=(n // tm,),
        in_specs=[
            pl.BlockSpec((tm, D_MODEL), row),
            pl.BlockSpec((tm, ATTN_WIDTH), row),
            pl.BlockSpec((tm, RET_WIDTH), row),
            pl.BlockSpec((1, N_MODS, rm, D_MODEL), lambda i: (mod_idx(i), 0, 0, 0)),
            _const_spec((D_MODEL, D_MODEL)),
            _const_spec((1, D_MODEL)),
            _const_spec((D_MODEL, d_ff)), _const_spec((D_MODEL, d_ff)), _const_spec((d_ff, D_MODEL)),
        ],
        out_specs=pl.BlockSpec((tm, D_MODEL), row),
        out_shape=jax.ShapeDtypeStruct((n, D_MODEL), F32),
        compiler_params=pltpu.CompilerParams(dimension_semantics=("arbitrary",), vmem_limit_bytes=VMEM_LIMIT),
        name="out_ffn",
    )(h2d, oa, o_ret, mods, lw["wo"], lw["n2"], lw["wg2"], lw["wu2"], lw["wd2"])


def _moba_prompt_kernel(q_ref, kt_ref, vt_in_ref, qn2_ref, o_ref,
                        kaug_ref, vt_ref, km_ref, bound_ref, flag_ref, acc_ref, shift_ref, *, nb):
    qi = pl.program_id(2)
    tq = MOBA_BLOCK
    hd = ATTN_HEAD_DIM
    grp = MOBA_GROUP
    lane = lax.broadcasted_iota(jnp.int32, (tq, LANES), 1)
    in_head = (lane < hd, lane >= hd)
    field_off = (hd, 0)

    @pl.when(qi == 0)
    def _prepare():
        km_ref[...] = jnp.zeros(km_ref.shape, F32)
        feat = lax.broadcasted_iota(jnp.int32, (LANES, tq), 0)
        feat_in_head = (feat < hd, feat >= hd)
        kn2 = [jnp.zeros((1, 1), F32)] * 2
        for j in range(nb):
            kb = kt_ref[0, :, j * tq:(j + 1) * tq].T
            vtb = vt_in_ref[0, :, j * tq:(j + 1) * tq]
            km_ref[hd + j:hd + j + 1, :] = jnp.mean(kb, axis=0, keepdims=True)
            sq = kb * kb
            for h in range(2):
                norm2 = jnp.sum(jnp.where(in_head[h], sq, 0.0), axis=1, keepdims=True)
                kn2[h] = jnp.maximum(kn2[h], jnp.max(norm2, axis=0, keepdims=True))
                off = field_off[h]
                field = jnp.where(jnp.logical_or(lane == off + j, lane == off + SHIFT_LANE), 1.0, 0.0)
                kaug_ref[h, j * tq:(j + 1) * tq, :] = jnp.where(in_head[h], kb, field).astype(BF16)
                vt_ref[h, j] = jnp.where(feat_in_head[h], vtb, 1.0).astype(BF16)
        bound = [jnp.sqrt(qn2_ref[0:1, 0:1] * kn2[h]) for h in range(2)]
        for h in range(2):
            bound_ref[h] = jnp.broadcast_to(bound[h], bound_ref.shape[1:])
        flag_ref[0] = (jnp.max(jnp.maximum(bound[0], bound[1])) <= SHIFT_LIMIT).astype(jnp.int32)

    q = q_ref[...]
    scale = hd ** -0.5 * LOG2_E
    q_head = [jnp.where(in_head[h], q, 0.0) for h in range(2)]
    qs = [q_head[h] * scale for h in range(2)]

    field_row = lax.broadcasted_iota(jnp.int32, (LANES, tq), 0)
    field_row_f = field_row.astype(F32)

    def block_bias(h):
        off = field_off[h]
        gate_t = _dot3_nt(km_ref[hd - off:hd - off + LANES, :], q_head[h])
        valid = jnp.logical_and(field_row >= off, field_row < off + qi)
        sel = _top_k_rows(jnp.where(valid, gate_t, NEG_INF), field_row_f, MOBA_TOP_K)
        return jnp.where(jnp.logical_and(sel, valid), 0.0, NEG_INF).T

    key_i = lax.broadcasted_iota(jnp.int32, (tq, tq), 0)
    qry_i = lax.broadcasted_iota(jnp.int32, (tq, tq), 1)
    causal = key_i <= qry_i
    own_rows = pl.ds(pl.multiple_of(qi * tq, tq), tq)
    n_groups = (qi + grp - 1) // grp

    def group_rows(g):
        return pl.ds(pl.multiple_of(g * (grp * tq), grp * tq), grp * tq)

    bounded = flag_ref[0] == 1

    @pl.when(bounded)
    def _bound_shift():
        for h in range(2):
            shift_ref[h] = jnp.broadcast_to(bound_ref[h, 0:1, :], (tq, LANES))

    @pl.when(jnp.logical_not(bounded))
    def _exact_shift():
        for h in range(2):
            q_sel = jnp.where(in_head[h], qs[h], block_bias(h)).astype(BF16)
            m = _col_max(jnp.where(causal, _dot_nt(kaug_ref[h, own_rows, :], qs[h].astype(BF16)), NEG_INF))
            m = lax.fori_loop(
                0, n_groups, lambda g, m: jnp.maximum(m, _col_max(_dot_nt(kaug_ref[h, group_rows(g), :], q_sel))), m)
            shift_ref[h] = jnp.broadcast_to(m, (LANES, tq)).T

    q_own, own_scores = [], []
    for h in range(2):
        at_shift = lane == field_off[h] + SHIFT_LANE
        q_own.append(jnp.where(in_head[h], qs[h], jnp.where(at_shift, -shift_ref[h], 0.0)))
        own_scores.append(_dot_nt(kaug_ref[h, own_rows, :], q_own[h].astype(BF16)))
    q_past = [jnp.where(jnp.logical_or(in_head[h], lane == field_off[h] + SHIFT_LANE), q_own[h], block_bias(h)).astype(BF16)
              for h in range(2)]
    for h in range(2):
        s = jnp.where(causal, own_scores[h], NEG_INF)
        acc_ref[h] = _dot(vt_ref[h, qi], jnp.exp2(s).astype(BF16))

    def add_groups(groups):
        scores = [[_dot_nt(kaug_ref[h, group_rows(g), :], q_past[h]) for h in range(2)] for g in groups]
        for g, sc in zip(groups, scores):
            for h in range(2):
                p = jnp.exp2(sc[h]).astype(BF16)
                pv = None
                for c in range(grp):
                    part = _dot(vt_ref[h, g * grp + c], p[c * tq:(c + 1) * tq, :])
                    pv = part if pv is None else pv + part
                acc_ref[h] += pv

    def pair_body(t, carry):
        add_groups([2 * t, 2 * t + 1])
        return carry

    lax.fori_loop(0, n_groups // 2, pair_body, 0)

    @pl.when(n_groups % 2 == 1)
    def _tail():
        add_groups([n_groups - 1])

    a0, a1 = acc_ref[0], acc_ref[1]
    o_t = jnp.concatenate([a0[:hd] * (1.0 / a0[hd:hd + 1]), a1[hd:] * (1.0 / a1[0:1])], axis=0)
    o_ref[...] = o_t.T


def _moba_prompt(qa, kt, vt, q_norm2_max, batch, seq):
    tq = MOBA_BLOCK
    assert seq % (tq * MOBA_GROUP) == 0
    nb = seq // tq
    assert nb <= SHIFT_LANE, "side field holds one lane per key block below the shift lane"
    n_pairs = ATTN_WIDTH // LANES
    kv_spec = pl.BlockSpec((1, LANES, seq), lambda b, hp, qi: (b, hp, 0))
    q_spec = pl.BlockSpec((tq, LANES), lambda b, hp, qi: (b * nb + qi, hp))
    return pl.pallas_call(
        functools.partial(_moba_prompt_kernel, nb=nb),
        grid=(batch, n_pairs, nb),
        in_specs=[q_spec, kv_spec, kv_spec, _const_spec((1, LANES))],
        out_specs=q_spec,
        out_shape=jax.ShapeDtypeStruct(qa.shape, F32),
        scratch_shapes=[
            pltpu.VMEM((2, seq, LANES), BF16),
            pltpu.VMEM((2, nb, LANES, tq), BF16),
            pltpu.VMEM((ATTN_HEAD_DIM + LANES, LANES), F32),
            pltpu.VMEM((2, SUBLANES, LANES), F32),
            pltpu.SMEM((1,), jnp.int32),
            pltpu.VMEM((2, LANES, tq), F32),
            pltpu.VMEM((2, tq, LANES), F32),
        ],
        compiler_params=pltpu.CompilerParams(dimension_semantics=("arbitrary", "arbitrary", "arbitrary"),
                                             vmem_limit_bytes=VMEM_LIMIT),
        name="moba_prompt",
    )(qa, kt, vt, jnp.broadcast_to(q_norm2_max, (1, LANES)).astype(F32))


def _head_norm_gate(out, g):
    on = out * lax.rsqrt(jnp.mean(out * out, axis=-1, keepdims=True) + EPS)
    return on * _silu(g)


def _ret_prompt_kernel(q_ref, k_ref, v_ref, g_ref, dec_ref, qd_ref, kd_ref, cd_ref, o_ref, st_ref, s_ref):
    i = pl.program_id(1)
    c_len = RET_CHUNK

    @pl.when(i == 0)
    def _init():
        s_ref[...] = jnp.zeros(s_ref.shape, F32)

    for c in range(q_ref.shape[0] // c_len):
        rows = slice(c * c_len, (c + 1) * c_len)
        for h in range(RET_HEADS):
            cols = slice(h * LANES, (h + 1) * LANES)
            q, k, v = q_ref[rows, cols], k_ref[rows, cols], v_ref[rows, cols]
            vb = v.astype(BF16)
            sc = _dot_nt(q.astype(BF16), k.astype(BF16)) * dec_ref[h]
            st = s_ref[h]
            out = _dot(sc.astype(BF16), vb) + _dot((q * qd_ref[:, cols]).astype(BF16), st.astype(BF16))
            s_ref[h] = cd_ref[h] * st + _dot((k * kd_ref[:, cols]).T.astype(BF16), vb)
            o_ref[rows, cols] = _head_norm_gate(out, g_ref[rows, cols])

    @pl.when(i == pl.num_programs(1) - 1)
    def _flush():
        st_ref[0] = s_ref[...]


def _ret_tables(chunk_len):
    lg = jnp.log(1.0 - 2.0 ** (-5.0 - jnp.arange(RET_HEADS, dtype=F32)))
    i = jnp.arange(RET_CHUNK, dtype=F32)
    diff = i[:, None] - i[None, :]
    decay = jnp.where(diff >= 0, jnp.exp(jnp.maximum(diff, 0.0)[None] * lg[:, None, None]), 0.0)
    qd = jnp.exp((i + 1.0)[None] * lg[:, None])
    kd = jnp.exp((chunk_len - 1.0 - i)[None] * lg[:, None])
    widen = lambda t: jnp.repeat(t.T, RET_HEAD_DIM, axis=1)
    cd = jnp.broadcast_to(jnp.exp(chunk_len * lg)[:, None, None], (RET_HEADS, 1, LANES))
    return decay, widen(qd), widen(kd), cd


def _ret_prompt(qr, kr, vr, gr, batch, seq, *, tr):
    nt = seq // tr
    row = pl.BlockSpec((tr, RET_WIDTH), lambda b, i: (b * nt + i, 0))
    decay, qd, kd, cd = _ret_tables(RET_CHUNK)
    return pl.pallas_call(
        _ret_prompt_kernel,
        grid=(batch, nt),
        in_specs=[row, row, row, row,
                  _const_spec((RET_HEADS, RET_CHUNK, RET_CHUNK)),
                  _const_spec((RET_CHUNK, RET_WIDTH)), _const_spec((RET_CHUNK, RET_WIDTH)),
                  _const_spec((RET_HEADS, 1, LANES))],
        out_specs=[row, pl.BlockSpec((1, RET_HEADS, RET_HEAD_DIM, RET_HEAD_DIM), lambda b, i: (b, 0, 0, 0))],
        out_shape=[jax.ShapeDtypeStruct(qr.shape, F32),
                   jax.ShapeDtypeStruct((batch, RET_HEADS, RET_HEAD_DIM, RET_HEAD_DIM), F32)],
        scratch_shapes=[pltpu.VMEM((RET_HEADS, RET_HEAD_DIM, RET_HEAD_DIM), F32)],
        compiler_params=pltpu.CompilerParams(dimension_semantics=("arbitrary", "arbitrary")),
        name="ret_prompt",
    )(qr, kr, vr, gr, decay, qd, kd, cd)


def _ret_sample_kernel(q_ref, k_ref, v_ref, g_ref, st_ref, dec_ref, qd_ref, kd_ref, cd_ref, o_ref, ns_ref):
    tp = q_ref.shape[1]
    zeros = jnp.zeros((RET_CHUNK - tp, LANES), F32)
    for s in range(q_ref.shape[0]):
        for h in range(RET_HEADS):
            cols = slice(h * LANES, (h + 1) * LANES)
            q = q_ref[s, :, cols]
            k = jnp.concatenate([k_ref[s, :, cols], zeros], axis=0)
            v = jnp.concatenate([v_ref[s, :, cols], zeros], axis=0)
            vb = v.astype(BF16)
            sc = _dot_nt(q.astype(BF16), k.astype(BF16)) * dec_ref[h, :tp, :]
            st = st_ref[s, h]
            out = _dot(sc.astype(BF16), vb) + _dot((q * qd_ref[:tp, cols]).astype(BF16), st.astype(BF16))
            ns_ref[s, h] = cd_ref[h] * st + _dot((k * kd_ref[:, cols]).T.astype(BF16), vb)
            o_ref[s, :, cols] = _head_norm_gate(out, g_ref[s, :, cols])


def _ret_sample(qr, kr, vr, gr, state, step_len, *, ns):
    db, tp, _ = qr.shape
    tok = pl.BlockSpec((ns, tp, RET_WIDTH), lambda i: (i, 0, 0))
    st = pl.BlockSpec((ns, RET_HEADS, RET_HEAD_DIM, RET_HEAD_DIM), lambda i: (i, 0, 0, 0))
    decay, qd, kd, cd = _ret_tables(step_len)
    return pl.pallas_call(
        _ret_sample_kernel,
        grid=(db // ns,),
        in_specs=[tok, tok, tok, tok, st,
                  _const_spec((RET_HEADS, RET_CHUNK, RET_CHUNK)),
                  _const_spec((RET_CHUNK, RET_WIDTH)), _const_spec((RET_CHUNK, RET_WIDTH)),
                  _const_spec((RET_HEADS, 1, LANES))],
        out_specs=[tok, st],
        out_shape=[jax.ShapeDtypeStruct(qr.shape, F32), jax.ShapeDtypeStruct(state.shape, F32)],
        compiler_params=pltpu.CompilerParams(dimension_semantics=("arbitrary",)),
        name="ret_sample",
    )(qr, kr, vr, gr, state, decay, qd, kd, cd)


def _moba_sample_kernel(pt_ref, q_ref, kn_ref, vn_ref, *refs, n_pages, page, step_len):
    del pt_ref
    k_refs, v_refs, o_ref = refs[:n_pages], refs[n_pages:2 * n_pages], refs[2 * n_pages]
    hd = ATTN_HEAD_DIM
    ppb = MOBA_BLOCK // page
    n_full = n_pages // ppb
    rows = step_len * ATTN_HEADS
    q = q_ref[0]
    kn = kn_ref[0]
    vn = vn_ref[0]
    qrep = jnp.concatenate([jnp.broadcast_to(q[t:t + 1, :], (ATTN_HEADS, ATTN_WIDTH)) for t in range(step_len)], axis=0)
    r_i = lax.broadcasted_iota(jnp.int32, (rows, ATTN_WIDTH), 0)
    c_i = lax.broadcasted_iota(jnp.int32, (rows, ATTN_WIDTH), 1)
    own_head = (c_i // hd) == (r_i % ATTN_HEADS)
    qbd = jnp.where(own_head, qrep, 0.0)

    lane = lax.broadcasted_iota(jnp.int32, (rows, LANES), 1)
    lane_f = lane.astype(F32)
    kt = lambda p: k_refs[p][0].reshape(ATTN_WIDTH, page)
    vt = lambda p: v_refs[p][0].reshape(ATTN_WIDTH, page)

    gate = jnp.full((rows, LANES), NEG_INF, F32)
    for b in range(n_full):
        ksum = kt(b * ppb)
        for pp in range(1, ppb):
            ksum = ksum + kt(b * ppb + pp)
        g = jnp.sum(_dot3(qbd, ksum), axis=-1, keepdims=True) * (1.0 / MOBA_BLOCK)
        gate = jnp.where(lane == b, g, gate)
    sel = jnp.logical_and(_top_k_lanes(gate, lane_f, min(MOBA_TOP_K, n_full)), lane < n_full)
    bias = jnp.where(sel, 0.0, NEG_INF)

    qs = qbd * (hd ** -0.5)
    qsb = qs.astype(BF16)
    s_pages = []
    for p in range(n_pages):
        b_col = jnp.max(jnp.where(lane == p // ppb, bias, NEG_INF), axis=-1, keepdims=True)
        s_pages.append(_dot(qsb, kt(p).astype(BF16)) + b_col)
    tok_of_row = lax.broadcasted_iota(jnp.int32, (rows, 1), 0) // ATTN_HEADS
    s_own = []
    for t in range(step_len):
        s_t = jnp.sum(qs * kn[t:t + 1, :], axis=-1, keepdims=True)
        s_own.append(jnp.where(tok_of_row >= t, s_t, NEG_INF))

    m = functools.reduce(jnp.maximum, s_own)
    for s in s_pages:
        m = jnp.maximum(m, jnp.max(s, axis=-1, keepdims=True))
    l = jnp.zeros((rows, 1), F32)
    o = jnp.zeros((rows, ATTN_WIDTH), F32)
    for p in range(n_pages):
        e = jnp.exp(s_pages[p] - m)
        l = l + jnp.sum(e, axis=-1, keepdims=True)
        o = o + _dot_nt(e.astype(BF16), vt(p).astype(BF16))
    for t in range(step_len):
        e = jnp.exp(s_own[t] - m)
        l = l + e
        o = o + e * vn[t:t + 1, :]
    o = jnp.where(own_head, o, 0.0) * (1.0 / l)
    o_ref[0] = jnp.zeros(o_ref.shape[1:], F32)
    for t in range(step_len):
        o_ref[0, t:t + 1, :] = jnp.sum(o[t * ATTN_HEADS:(t + 1) * ATTN_HEADS, :], axis=0, keepdims=True)


def _moba_sample(qa, ka, va, cache_kt, cache_vt, page_table, step_len):
    db, tp, _ = qa.shape
    n_pages = page_table.shape[1]
    page = cache_kt.shape[-1]
    assert (n_pages * page) % MOBA_BLOCK == 0, "past length must be whole key blocks"
    assert MOBA_BLOCK % page == 0
    tok = pl.BlockSpec((1, tp, ATTN_WIDTH), lambda b, pt: (b, 0, 0))
    page_specs = [pl.BlockSpec((1, ATTN_HEADS, ATTN_HEAD_DIM, page), functools.partial(lambda b, pt, p: (pt[b, p], 0, 0, 0), p=p))
                  for p in range(n_pages)]
    grid_spec = pltpu.PrefetchScalarGridSpec(
        num_scalar_prefetch=1,
        grid=(db,),
        in_specs=[tok, tok, tok] + page_specs + page_specs,
        out_specs=tok,
    )
    return pl.pallas_call(
        functools.partial(_moba_sample_kernel, n_pages=n_pages, page=page, step_len=step_len),
        grid_spec=grid_spec,
        out_shape=jax.ShapeDtypeStruct(qa.shape, F32),
        compiler_params=pltpu.CompilerParams(dimension_semantics=("arbitrary",), vmem_limit_bytes=VMEM_LIMIT),
        name="moba_sample",
    )(page_table, qa, ka, va, *([cache_kt] * n_pages), *([cache_vt] * n_pages))


def _rope_tables(pos):
    half = ROPE_DIMS // 2
    inv = ROPE_THETA ** (-jnp.arange(half, dtype=F32) * 2.0 / ROPE_DIMS)
    ang = pos.astype(F32)[:, None] * inv[None, :]
    cos, sin = jnp.cos(ang), jnp.sin(ang)
    j = jnp.arange(LANES) % ATTN_HEAD_DIM
    first, second = j < half, jnp.logical_and(j >= half, j < ROPE_DIMS)
    cj, sj = cos[:, j % half], sin[:, j % half]
    return (jnp.where(jnp.logical_or(first, second), cj, 1.0),
            jnp.where(first, -sj, 0.0),
            jnp.where(second, sj, 0.0))


def _ret_rot_tables(pos):
    inv = 1.0 / (RET_ROT_BASE ** jnp.linspace(0.0, 1.0, RET_HEAD_DIM // 2, dtype=F32))
    ang = pos.astype(F32)[:, None] * inv[None, :]
    cos, sin = jnp.cos(ang), jnp.sin(ang)
    lane = jnp.arange(LANES)
    cl, sl = cos[:, lane // 2], sin[:, lane // 2]
    even = lane % 2 == 0
    return cl, jnp.where(even, -sl, 0.0), jnp.where(even, 0.0, sl)


def _layer_weights(l, norm_ffn1_w, ffn1_w_gate, ffn1_w_up, ffn1_w_down, norm_mix_w, w_in, q_norm_w, k_norm_w, w_out,
                   norm_ffn2_w, ffn2_w_gate, ffn2_w_up, ffn2_w_down):
    head = jnp.arange(ATTN_WIDTH) // ATTN_HEAD_DIM
    return {
        "n1": norm_ffn1_w[l][None], "nm": norm_mix_w[l][None], "n2": norm_ffn2_w[l][None],
        "wg1": ffn1_w_gate[l].astype(BF16), "wu1": ffn1_w_up[l].astype(BF16), "wd1": ffn1_w_down[l].astype(BF16),
        "wg2": ffn2_w_gate[l].astype(BF16), "wu2": ffn2_w_up[l].astype(BF16), "wd2": ffn2_w_down[l].astype(BF16),
        "win": w_in[l].astype(BF16), "wo": w_out[l].astype(BF16),
        "qn": jnp.tile(q_norm_w[l], ATTN_HEADS)[None], "kn": jnp.tile(k_norm_w[l], ATTN_HEADS)[None],
        "bd": jnp.where(head[:, None] == head[None, :], 1.0 / ATTN_HEAD_DIM, 0.0).astype(BF16),
    }


def kernel(x_prompt, x_sample, cache_k, cache_v, state_ret, page_table, c_prompt, c_sample, w_ada, b_ada, norm_ffn1_w, ffn1_w_gate, ffn1_w_up, ffn1_w_down, norm_mix_w, w_in, q_norm_w, k_norm_w, w_out, norm_ffn2_w, ffn2_w_gate, ffn2_w_up, ffn2_w_down):
    batch, seq, _ = x_prompt.shape
    db, step_len, _ = x_sample.shape
    depth = w_ada.shape[0]
    n_pages, page = page_table.shape[1], cache_k.shape[2]
    past_len = n_pages * page
    tm_p = 256
    tp = SUBLANES
    assert seq % tm_p == 0 and step_len <= tp and db % SUBLANES == 0

    tabs_p = [t.reshape(seq // tm_p, tm_p, LANES)
              for t in _rope_tables(jnp.arange(seq)) + _ret_rot_tables(jnp.arange(seq))]
    pos_s = past_len + jnp.arange(step_len)
    tabs_s = [t.reshape(step_len, 1, LANES) for t in _rope_tables(pos_s) + _ret_rot_tables(pos_s)]

    hp = x_prompt.reshape(batch * seq, D_MODEL)
    hs = x_sample.transpose(1, 0, 2).reshape(step_len * db, D_MODEL)
    outs = [[] for _ in range(6)]
    for l in range(depth):
        lw = _layer_weights(l, norm_ffn1_w, ffn1_w_gate, ffn1_w_up, ffn1_w_down, norm_mix_w, w_in, q_norm_w, k_norm_w,
                            w_out, norm_ffn2_w, ffn2_w_gate, ffn2_w_up, ffn2_w_down)
        mods = _mods(jnp.concatenate([c_prompt, c_sample], axis=0), w_ada[l], b_ada[l])
        mods_p = mods[:batch].reshape(batch, N_MODS, 1, D_MODEL)
        mods_s = mods[batch:].reshape(db, N_MODS, D_MODEL).transpose(1, 0, 2)[None]

        tiles_per_seq = seq // tm_p
        h1, qa, kt, vt, qr, kr, vr, gr = _ffn_proj(
            hp, mods_p, tabs_p, lw, tm=tm_p, mod_idx=lambda i: i // tiles_per_seq, tab_idx=lambda i: i % tiles_per_seq,
            kv_groups=batch, kv_idx=lambda i: (i // tiles_per_seq, 0, i % tiles_per_seq))
        q_norm2_max = jnp.max(jnp.square(q_norm_w[l])) * (LOG2_E * LOG2_E)
        oa = _moba_prompt(qa, kt, vt, q_norm2_max, batch, seq)
        o_ret, st_p = _ret_prompt(qr, kr, vr, gr, batch, seq, tr=512)
        hp = _out_ffn(h1, oa, o_ret, mods_p, lw, tm=tm_p, mod_idx=lambda i: i // tiles_per_seq)

        h1s, qas, kts, vts, qrs, krs, vrs, grs = _ffn_proj(hs, mods_s, tabs_s, lw, tm=db, mod_idx=lambda i: 0,
                                                           tab_idx=lambda i: i, kv_groups=step_len, kv_idx=lambda i: (i, 0, 0))

        def seq_major(t):
            t = t.reshape(step_len, db, ATTN_WIDTH).transpose(1, 0, 2)
            return jnp.pad(t, ((0, 0), (0, tp - step_len), (0, 0)))

        def seq_major_t(t):
            return jnp.pad(t.transpose(2, 0, 1), ((0, 0), (0, tp - step_len), (0, 0)))

        def token_major(t):
            return t[:, :step_len].transpose(1, 0, 2).reshape(step_len * db, ATTN_WIDTH)

        cache_kt = cache_k[l].transpose(0, 2, 3, 1)
        cache_vt = cache_v[l].transpose(0, 2, 3, 1)
        oas = _moba_sample(seq_major(qas), seq_major_t(kts), seq_major_t(vts), cache_kt, cache_vt, page_table, step_len)
        o_rets, st_s = _ret_sample(seq_major(qrs), seq_major(krs), seq_major(vrs), seq_major(grs), state_ret[l],
                                   step_len, ns=SUBLANES)
        hs = _out_ffn(h1s, token_major(oas), token_major(o_rets), mods_s, lw, tm=db, mod_idx=lambda i: 0)

        heads = (ATTN_HEADS, ATTN_HEAD_DIM)
        outs[0].append(kt.reshape(batch, *heads, seq).transpose(0, 3, 1, 2))
        outs[1].append(vt.reshape(batch, *heads, seq).transpose(0, 3, 1, 2))
        outs[2].append(kts.reshape(step_len, *heads, db).transpose(3, 0, 1, 2))
        outs[3].append(vts.reshape(step_len, *heads, db).transpose(3, 0, 1, 2))
        outs[4].append(st_p)
        outs[5].append(st_s)

    y_prompt = hp.reshape(batch, seq, D_MODEL)
    y_sample = hs.reshape(step_len, db, D_MODEL).transpose(1, 0, 2)
    return (y_prompt, y_sample) + tuple(jnp.stack(o) for o in outs)
```

```python
import functools

import jax
import jax.numpy as jnp
from jax import lax
from jax.experimental import pallas as pl
from jax.experimental.pallas import tpu as pltpu

F32 = jnp.float32
BF16 = jnp.bfloat16

D_MODEL = 1024
ATTN_HEADS = 8
ATTN_HEAD_DIM = 64
ATTN_WIDTH = ATTN_HEADS * ATTN_HEAD_DIM
RET_HEADS = 4
RET_HEAD_DIM = 128
RET_WIDTH = RET_HEADS * RET_HEAD_DIM
IN_WIDTH = 3 * ATTN_WIDTH + 4 * RET_WIDTH
MOBA_BLOCK = 256
MOBA_TOP_K = 3
ROPE_THETA = 500000.0
ROPE_DIMS = ATTN_HEAD_DIM // 4
RET_CHUNK = 128
RET_ROT_BASE = 10000.0
N_MODS = 9
EPS = 1e-6
NEG_INF = -1e30

LANES = 128
SUBLANES = 8
VMEM_LIMIT = 56 * 1024 * 1024
MXU_TILE = 256
FF_CHUNK = MXU_TILE
MOBA_GROUP = 4
LOG2_E = 1.4426950408889634
SHIFT_LANE = 63
SHIFT_LIMIT = 60.0


def _dot(a, b):
    return jnp.dot(a, b, preferred_element_type=F32)


def _dot_nt(a, b):
    return lax.dot_general(a, b, (((1,), (1,)), ((), ())), preferred_element_type=F32)


def _split(a):
    hi = a.astype(BF16)
    lo = (a - hi.astype(F32)).astype(BF16)
    return hi, lo


def _dot3(a, b):
    ah, al = _split(a)
    bh, bl = _split(b)
    return _dot(ah, bh) + (_dot(ah, bl) + _dot(al, bh))


def _dot3_nt(a, b):
    ah, al = _split(a)
    bh, bl = _split(b)
    return _dot_nt(ah, bh) + (_dot_nt(ah, bl) + _dot_nt(al, bh))


def _sigmoid(x):
    return 1.0 / (1.0 + jnp.exp(-x))


def _silu(x):
    return x * _sigmoid(x)


def _rms(x):
    return x * lax.rsqrt(jnp.mean(x * x, axis=-1, keepdims=True) + EPS)


def _top_k_lanes(g, lane_f, k):
    sel = jnp.zeros(g.shape, jnp.bool_)
    for _ in range(k):
        m = jnp.max(g, axis=-1, keepdims=True)
        idx = jnp.min(jnp.where(g == m, lane_f, 1e9), axis=-1, keepdims=True)
        pick = lane_f == idx
        sel = jnp.logical_or(sel, pick)
        g = jnp.where(pick, -jnp.inf, g)
    return sel


def _fold_rows(op, s):
    while s.shape[0] > SUBLANES:
        half = s.shape[0] // 2
        s = op(s[:half], s[half:])
    return s


def _col_max(s):
    return jnp.max(_fold_rows(jnp.maximum, s), axis=0, keepdims=True)


def _col_min(s):
    return jnp.min(_fold_rows(jnp.minimum, s), axis=0, keepdims=True)


def _top_k_rows(g, row_f, k):
    sel = jnp.zeros(g.shape, jnp.bool_)
    for _ in range(k):
        m = _col_max(g)
        idx = _col_min(jnp.where(g == m, row_f, 1e9))
        pick = row_f == idx
        sel = jnp.logical_or(sel, pick)
        g = jnp.where(pick, -jnp.inf, g)
    return sel


def _mods_kernel(c_ref, w_ref, b_ref, o_ref):
    s = _silu(c_ref[...]).astype(BF16)
    o_ref[...] = _dot(s, w_ref[...].astype(BF16)) + b_ref[...]


def _mods(c, w_ada, b_ada):
    n = c.shape[0]
    n_pad = -(-n // SUBLANES) * SUBLANES
    c = jnp.pad(c, ((0, n_pad - n), (0, 0)))
    width = w_ada.shape[1]
    tn = 9 * LANES
    out = pl.pallas_call(
        _mods_kernel,
        grid=(width // tn,),
        in_specs=[pl.BlockSpec((n_pad, D_MODEL), lambda j: (0, 0)),
                  pl.BlockSpec((D_MODEL, tn), lambda j: (0, j)),
                  pl.BlockSpec((1, tn), lambda j: (0, j))],
        out_specs=pl.BlockSpec((n_pad, tn), lambda j: (0, j)),
        out_shape=jax.ShapeDtypeStruct((n_pad, width), F32),
        name="mods",
    )(c, w_ada, b_ada.reshape(1, width))
    return out[:n]


def _swiglu_acc(xb, wg_ref, wu_ref, wd_ref):
    d_ff = wg_ref.shape[1]
    acc = None
    for c in range(d_ff // FF_CHUNK):
        sl = slice(c * FF_CHUNK, (c + 1) * FF_CHUNK)
        g = _dot(xb, wg_ref[:, sl])
        u = _dot(xb, wu_ref[:, sl])
        part = _dot((_silu(g) * u).astype(BF16), wd_ref[sl, :])
        acc = part if acc is None else acc + part
    return acc


def _const_spec(shape):
    nd = len(shape)
    return pl.BlockSpec(shape, lambda *_: (0,) * nd, pipeline_mode=pl.Buffered(1))


def _ffn_proj_kernel(x_ref, mod_ref, n1_ref, wg_ref, wu_ref, wd_ref, nm_ref, win_ref, qn_ref, kn_ref, bd_ref,
                     ca_ref, sa1_ref, sa2_ref, cr_ref, sr1_ref, sr2_ref,
                     h_ref, qa_ref, ka_ref, va_ref, qr_ref, kr_ref, vr_ref, gr_ref):
    x = x_ref[...]
    xn = (_rms(x) * n1_ref[...]) * (1.0 + mod_ref[0, 1]) + mod_ref[0, 0]
    acc = _swiglu_acc(xn.astype(BF16), wg_ref, wu_ref, wd_ref)
    h = x + 0.5 * mod_ref[0, 2] * acc
    h_ref[...] = h
    hn = (_rms(h) * nm_ref[...]) * (1.0 + mod_ref[0, 4]) + mod_ref[0, 3]
    hb = hn.astype(BF16)

    def seg(i):
        return _dot(hb, win_ref[:, i * ATTN_WIDTH:(i + 1) * ATTN_WIDTH])

    ca, sa1, sa2 = ca_ref[0], sa1_ref[0], sa2_ref[0]
    cr, sr1, sr2 = cr_ref[0], sr1_ref[0], sr2_ref[0]
    bd = bd_ref[...]

    def attn_head_norm_rope(p, w, o_ref, transposed):
        hi, lo = _split(p * p)
        ms = jnp.concatenate([_dot(hi[:, c:c + MXU_TILE], bd) + _dot(lo[:, c:c + MXU_TILE], bd)
                              for c in range(0, ATTN_WIDTH, MXU_TILE)], axis=1)
        pn = (p * lax.rsqrt(ms + EPS)) * w
        for g in range(ATTN_WIDTH // LANES):
            xg = pn[:, g * LANES:(g + 1) * LANES]
            r = xg * ca + pltpu.roll(xg, LANES - ROPE_DIMS // 2, 1) * sa1 + pltpu.roll(xg, ROPE_DIMS // 2, 1) * sa2
            if transposed:
                o_ref[0, g * LANES:(g + 1) * LANES, :] = r.T
            else:
                o_ref[:, g * LANES:(g + 1) * LANES] = r

    def ret_rotate(p, o_ref, scale):
        for g in range(RET_HEADS):
            xg = p[:, g * LANES:(g + 1) * LANES]
            r = xg * cr + pltpu.roll(xg, LANES - 1, 1) * sr1 + pltpu.roll(xg, 1, 1) * sr2
            o_ref[:, g * LANES:(g + 1) * LANES] = r if scale is None else r * scale

    attn_head_norm_rope(seg(0), qn_ref[...], qa_ref, False)
    attn_head_norm_rope(seg(1), kn_ref[...], ka_ref, True)
    va = seg(2)
    for g in range(ATTN_WIDTH // LANES):
        va_ref[0, g * LANES:(g + 1) * LANES, :] = va[:, g * LANES:(g + 1) * LANES].T
    ret_rotate(seg(3), qr_ref, None)
    ret_rotate(seg(4), kr_ref, RET_HEAD_DIM ** -0.5)
    vr_ref[...] = seg(5)
    gr_ref[...] = seg(6)


def _ffn_proj(x2d, mods, tabs, lw, *, tm, mod_idx, tab_idx, kv_groups, kv_idx):
    n = x2d.shape[0]
    rm = mods.shape[2]
    rt = tabs[0].shape[1]
    d_ff = lw["wg1"].shape[1]
    row = lambda i: (i, 0)
    tab_spec = pl.BlockSpec((1, rt, LANES), lambda i: (tab_idx(i), 0, 0))
    in_specs = [
        pl.BlockSpec((tm, D_MODEL), row),
        pl.BlockSpec((1, N_MODS, rm, D_MODEL), lambda i: (mod_idx(i), 0, 0, 0)),
        _const_spec((1, D_MODEL)),
        _const_spec((D_MODEL, d_ff)), _const_spec((D_MODEL, d_ff)), _const_spec((d_ff, D_MODEL)),
        _const_spec((1, D_MODEL)),
        _const_spec((D_MODEL, IN_WIDTH)),
        _const_spec((1, ATTN_WIDTH)), _const_spec((1, ATTN_WIDTH)),
        _const_spec((MXU_TILE, MXU_TILE)),
    ] + [tab_spec] * 6
    half = jax.ShapeDtypeStruct((n, ATTN_WIDTH), F32)
    half_t = jax.ShapeDtypeStruct((kv_groups, ATTN_WIDTH, n // kv_groups), F32)
    half_spec = pl.BlockSpec((tm, ATTN_WIDTH), row)
    half_t_spec = pl.BlockSpec((1, ATTN_WIDTH, tm), lambda i: kv_idx(i))
    out_shape = [jax.ShapeDtypeStruct((n, D_MODEL), F32), half, half_t, half_t] + [half] * 4
    out_specs = [pl.BlockSpec((tm, D_MODEL), row), half_spec, half_t_spec, half_t_spec] + [half_spec] * 4
    return pl.pallas_call(
        _ffn_proj_kernel,
        grid=(n // tm,),
        in_specs=in_specs,
        out_specs=out_specs,
        out_shape=out_shape,
        compiler_params=pltpu.CompilerParams(dimension_semantics=("arbitrary",), vmem_limit_bytes=VMEM_LIMIT),
        name="ffn_proj",
    )(x2d, mods, lw["n1"], lw["wg1"], lw["wu1"], lw["wd1"], lw["nm"], lw["win"], lw["qn"], lw["kn"], lw["bd"], *tabs)


def _out_ffn_kernel(h_ref, oa_ref, or_ref, mod_ref, wo_ref, n2_ref, wg_ref, wu_ref, wd_ref, y_ref):
    mix = _dot(oa_ref[...].astype(BF16), wo_ref[:ATTN_WIDTH, :]) + _dot(or_ref[...].astype(BF16), wo_ref[ATTN_WIDTH:, :])
    h = h_ref[...] + mod_ref[0, 5] * mix
    hn = (_rms(h) * n2_ref[...]) * (1.0 + mod_ref[0, 7]) + mod_ref[0, 6]
    acc = _swiglu_acc(hn.astype(BF16), wg_ref, wu_ref, wd_ref)
    y_ref[...] = h + 0.5 * mod_ref[0, 8] * acc


def _out_ffn(h2d, oa, o_ret, mods, lw, *, tm, mod_idx):
    n = h2d.shape[0]
    rm = mods.shape[2]
    d_ff = lw["wg2"].shape[1]
    row = lambda i: (i, 0)
    return pl.pallas_call(
        _out_ffn_kernel,
        grid=(n // tm,),
        in_specs=[
            pl.BlockSpec((tm, D_MODEL), row),
            pl.BlockSpec((tm, ATTN_WIDTH), row),
            pl.BlockSpec((tm, RET_WIDTH), row),
            pl.BlockSpec((1, N_MODS, rm, D_MODEL), lambda i: (mod_idx(i), 0, 0, 0)),
            _const_spec((D_MODEL, D_MODEL)),
            _const_spec((1, D_MODEL)),
            _const_spec((D_MODEL, d_ff)), _const_spec((D_MODEL, d_ff)), _const_spec((d_ff, D_MODEL)),
        ],
        out_specs=pl.BlockSpec((tm, D_MODEL), row),
        out_shape=jax.ShapeDtypeStruct((n, D_MODEL), F32),
        compiler_params=pltpu.CompilerParams(dimension_semantics=("arbitrary",), vmem_limit_bytes=VMEM_LIMIT),
        name="out_ffn",
    )(h2d, oa, o_ret, mods, lw["wo"], lw["n2"], lw["wg2"], lw["wu2"], lw["wd2"])


def _moba_prompt_kernel(q_ref, kt_ref, vt_in_ref, qn2_ref, o_ref,
                        kaug_ref, vt_ref, km_ref, bound_ref, flag_ref, acc_ref, shift_ref, *, nb):
    qi = pl.program_id(2)
    tq = MOBA_BLOCK
    hd = ATTN_HEAD_DIM
    grp = MOBA_GROUP
    lane = lax.broadcasted_iota(jnp.int32, (tq, LANES), 1)
    in_head = (lane < hd, lane >= hd)
    field_off = (hd, 0)

    @pl.when(qi == 0)
    def _prepare():
        km_ref[...] = jnp.zeros(km_ref.shape, F32)
        feat = lax.broadcasted_iota(jnp.int32, (LANES, tq), 0)
        feat_in_head = (feat < hd, feat >= hd)
        kn2 = [jnp.zeros((1, 1), F32)] * 2
        for j in range(nb):
            kb = kt_ref[0, :, j * tq:(j + 1) * tq].T
            vtb = vt_in_ref[0, :, j * tq:(j + 1) * tq]
            km_ref[hd + j:hd + j + 1, :] = jnp.mean(kb, axis=0, keepdims=True)
            sq = kb * kb
            for h in range(2):
                norm2 = jnp.sum(jnp.where(in_head[h], sq, 0.0), axis=1, keepdims=True)
                kn2[h] = jnp.maximum(kn2[h], jnp.max(norm2, axis=0, keepdims=True))
                off = field_off[h]
                field = jnp.where(jnp.logical_or(lane == off + j, lane == off + SHIFT_LANE), 1.0, 0.0)
                kaug_ref[h, j * tq:(j + 1) * tq, :] = jnp.where(in_head[h], kb, field).astype(BF16)
                vt_ref[h, j] = jnp.where(feat_in_head[h], vtb, 1.0).astype(BF16)
        bound = [jnp.sqrt(qn2_ref[0:1, 0:1] * kn2[h]) for h in range(2)]
        for h in range(2):
            bound_ref[h] = jnp.broadcast_to(bound[h], bound_ref.shape[1:])
        flag_ref[0] = (jnp.max(jnp.maximum(bound[0], bound[1])) <= SHIFT_LIMIT).astype(jnp.int32)

    q = q_ref[...]
    scale = hd ** -0.5 * LOG2_E
    q_head = [jnp.where(in_head[h], q, 0.0) for h in range(2)]
    qs = [q_head[h] * scale for h in range(2)]

    field_row = lax.broadcasted_iota(jnp.int32, (LANES, tq), 0)
    field_row_f = field_row.astype(F32)

    def block_bias(h):
        off = field_off[h]
        gate_t = _dot3_nt(km_ref[hd - off:hd - off + LANES, :], q_head[h])
        valid = jnp.logical_and(field_row >= off, field_row < off + qi)
        sel = _top_k_rows(jnp.where(valid, gate_t, NEG_INF), field_row_f, MOBA_TOP_K)
        return jnp.where(jnp.logical_and(sel, valid), 0.0, NEG_INF).T

    key_i = lax.broadcasted_iota(jnp.int32, (tq, tq), 0)
    qry_i = lax.broadcasted_iota(jnp.int32, (tq, tq), 1)
    causal = key_i <= qry_i
    own_rows = pl.ds(pl.multiple_of(qi * tq, tq), tq)
    n_groups = (qi + grp - 1) // grp

    def group_rows(g):
        return pl.ds(pl.multiple_of(g * (grp * tq), grp * tq), grp * tq)

    bounded = flag_ref[0] == 1

    @pl.when(bounded)
    def _bound_shift():
        for h in range(2):
            shift_ref[h] = jnp.broadcast_to(bound_ref[h, 0:1, :], (tq, LANES))

    @pl.when(jnp.logical_not(bounded))
    def _exact_shift():
        for h in range(2):
            q_sel = jnp.where(in_head[h], qs[h], block_bias(h)).astype(BF16)
            m = _col_max(jnp.where(causal, _dot_nt(kaug_ref[h, own_rows, :], qs[h].astype(BF16)), NEG_INF))
            m = lax.fori_loop(
                0, n_groups, lambda g, m: jnp.maximum(m, _col_max(_dot_nt(kaug_ref[h, group_rows(g), :], q_sel))), m)
            shift_ref[h] = jnp.broadcast_to(m, (LANES, tq)).T

    q_own, own_scores = [], []
    for h in range(2):
        at_shift = lane == field_off[h] + SHIFT_LANE
        q_own.append(jnp.where(in_head[h], qs[h], jnp.where(at_shift, -shift_ref[h], 0.0)))
        own_scores.append(_dot_nt(kaug_ref[h, own_rows, :], q_own[h].astype(BF16)))
    q_past = [jnp.where(jnp.logical_or(in_head[h], lane == field_off[h] + SHIFT_LANE), q_own[h], block_bias(h)).astype(BF16)
              for h in range(2)]
    for h in range(2):
        s = jnp.where(causal, own_scores[h], NEG_INF)
        acc_ref[h] = _dot(vt_ref[h, qi], jnp.exp2(s).astype(BF16))

    def add_groups(groups):
        scores = [[_dot_nt(kaug_ref[h, group_rows(g), :], q_past[h]) for h in range(2)] for g in groups]
        for g, sc in zip(groups, scores):
            for h in range(2):
                p = jnp.exp2(sc[h]).astype(BF16)
                pv = None
                for c in range(grp):
                    part = _dot(vt_ref[h, g * grp + c], p[c * tq:(c + 1) * tq, :])
                    pv = part if pv is None else pv + part
                acc_ref[h] += pv

    def pair_body(t, carry):
        add_groups([2 * t, 2 * t + 1])
        return carry

    lax.fori_loop(0, n_groups // 2, pair_body, 0)

    @pl.when(n_groups % 2 == 1)
    def _tail():
        add_groups([n_groups - 1])

    a0, a1 = acc_ref[0], acc_ref[1]
    o_t = jnp.concatenate([a0[:hd] * (1.0 / a0[hd:hd + 1]), a1[hd:] * (1.0 / a1[0:1])], axis=0)
    o_ref[...] = o_t.T


def _moba_prompt(qa, kt, vt, q_norm2_max, batch, seq):
    tq = MOBA_BLOCK
    assert seq % (tq * MOBA_GROUP) == 0
    nb = seq // tq
    assert nb <= SHIFT_LANE, "side field holds one lane per key block below the shift lane"
    n_pairs = ATTN_WIDTH // LANES
    kv_spec = pl.BlockSpec((1, LANES, seq), lambda b, hp, qi: (b, hp, 0))
    q_spec = pl.BlockSpec((tq, LANES), lambda b, hp, qi: (b * nb + qi, hp))
    return pl.pallas_call(
        functools.partial(_moba_prompt_kernel, nb=nb),
        grid=(batch, n_pairs, nb),
        in_specs=[q_spec, kv_spec, kv_spec, _const_spec((1, LANES))],
        out_specs=q_spec,
        out_shape=jax.ShapeDtypeStruct(qa.shape, F32),
        scratch_shapes=[
            pltpu.VMEM((2, seq, LANES), BF16),
            pltpu.VMEM((2, nb, LANES, tq), BF16),
            pltpu.VMEM((ATTN_HEAD_DIM + LANES, LANES), F32),
            pltpu.VMEM((2, SUBLANES, LANES), F32),
            pltpu.SMEM((1,), jnp.int32),
            pltpu.VMEM((2, LANES, tq), F32),
            pltpu.VMEM((2, tq, LANES), F32),
        ],
        compiler_params=pltpu.CompilerParams(dimension_semantics=("arbitrary", "arbitrary", "arbitrary"),
                                             vmem_limit_bytes=VMEM_LIMIT),
        name="moba_prompt",
    )(qa, kt, vt, jnp.broadcast_to(q_norm2_max, (1, LANES)).astype(F32))


def _head_norm_gate(out, g):
    on = out * lax.rsqrt(jnp.mean(out * out, axis=-1, keepdims=True) + EPS)
    return on * _silu(g)


def _ret_prompt_kernel(q_ref, k_ref, v_ref, g_ref, dec_ref, qd_ref, kd_ref, cd_ref, o_ref, st_ref, s_ref):
    i = pl.program_id(1)
    c_len = RET_CHUNK

    @pl.when(i == 0)
    def _init():
        s_ref[...] = jnp.zeros(s_ref.shape, F32)

    states = [s_ref[h] for h in range(RET_HEADS)]
    for c in range(q_ref.shape[0] // c_len):
        rows = slice(c * c_len, (c + 1) * c_len)
        for h in range(RET_HEADS):
            cols = slice(h * LANES, (h + 1) * LANES)
            q, k, v = q_ref[rows, cols], k_ref[rows, cols], v_ref[rows, cols]
            vb = v.astype(BF16)
            sc = _dot_nt(q.astype(BF16), k.astype(BF16)) * dec_ref[h]
            out = _dot(sc.astype(BF16), vb) + _dot((q * qd_ref[:, cols]).astype(BF16), states[h].astype(BF16))
            states[h] = cd_ref[h] * states[h] + _dot((k * kd_ref[:, cols]).T.astype(BF16), vb)
            o_ref[rows, cols] = _head_norm_gate(out, g_ref[rows, cols])
    for h in range(RET_HEADS):
        s_ref[h] = states[h]

    @pl.when(i == pl.num_programs(1) - 1)
    def _flush():
        st_ref[0] = s_ref[...]


def _ret_tables(chunk_len):
    lg = jnp.log(1.0 - 2.0 ** (-5.0 - jnp.arange(RET_HEADS, dtype=F32)))
    i = jnp.arange(RET_CHUNK, dtype=F32)
    diff = i[:, None] - i[None, :]
    decay = jnp.where(diff >= 0, jnp.exp(jnp.maximum(diff, 0.0)[None] * lg[:, None, None]), 0.0)
    qd = jnp.exp((i + 1.0)[None] * lg[:, None])
    kd = jnp.exp((chunk_len - 1.0 - i)[None] * lg[:, None])
    widen = lambda t: jnp.repeat(t.T, RET_HEAD_DIM, axis=1)
    cd = jnp.broadcast_to(jnp.exp(chunk_len * lg)[:, None, None], (RET_HEADS, 1, LANES))
    return decay, widen(qd), widen(kd), cd


def _ret_prompt(qr, kr, vr, gr, batch, seq, *, tr):
    nt = seq // tr
    row = pl.BlockSpec((tr, RET_WIDTH), lambda b, i: (b * nt + i, 0))
    decay, qd, kd, cd = _ret_tables(RET_CHUNK)
    return pl.pallas_call(
        _ret_prompt_kernel,
        grid=(batch, nt),
        in_specs=[row, row, row, row,
                  _const_spec((RET_HEADS, RET_CHUNK, RET_CHUNK)),
                  _const_spec((RET_CHUNK, RET_WIDTH)), _const_spec((RET_CHUNK, RET_WIDTH)),
                  _const_spec((RET_HEADS, 1, LANES))],
        out_specs=[row, pl.BlockSpec((1, RET_HEADS, RET_HEAD_DIM, RET_HEAD_DIM), lambda b, i: (b, 0, 0, 0))],
        out_shape=[jax.ShapeDtypeStruct(qr.shape, F32),
                   jax.ShapeDtypeStruct((batch, RET_HEADS, RET_HEAD_DIM, RET_HEAD_DIM), F32)],
        scratch_shapes=[pltpu.VMEM((RET_HEADS, RET_HEAD_DIM, RET_HEAD_DIM), F32)],
        compiler_params=pltpu.CompilerParams(dimension_semantics=("arbitrary", "arbitrary")),
        name="ret_prompt",
    )(qr, kr, vr, gr, decay, qd, kd, cd)


def _ret_sample_kernel(q_ref, k_ref, v_ref, g_ref, st_ref, dec_ref, qd_ref, kd_ref, cd_ref, o_ref, ns_ref):
    tp = q_ref.shape[1]
    zeros = jnp.zeros((RET_CHUNK - tp, LANES), F32)
    for s in range(q_ref.shape[0]):
        for h in range(RET_HEADS):
            cols = slice(h * LANES, (h + 1) * LANES)
            q = q_ref[s, :, cols]
            k = jnp.concatenate([k_ref[s, :, cols], zeros], axis=0)
            v = jnp.concatenate([v_ref[s, :, cols], zeros], axis=0)
            vb = v.astype(BF16)
            sc = _dot_nt(q.astype(BF16), k.astype(BF16)) * dec_ref[h, :tp, :]
            st = st_ref[s, h]
            out = _dot(sc.astype(BF16), vb) + _dot((q * qd_ref[:tp, cols]).astype(BF16), st.astype(BF16))
            ns_ref[s, h] = cd_ref[h] * st + _dot((k * kd_ref[:, cols]).T.astype(BF16), vb)
            o_ref[s, :, cols] = _head_norm_gate(out, g_ref[s, :, cols])


def _ret_sample(qr, kr, vr, gr, state, step_len, *, ns):
    db, tp, _ = qr.shape
    tok = pl.BlockSpec((ns, tp, RET_WIDTH), lambda i: (i, 0, 0))
    st = pl.BlockSpec((ns, RET_HEADS, RET_HEAD_DIM, RET_HEAD_DIM), lambda i: (i, 0, 0, 0))
    decay, qd, kd, cd = _ret_tables(step_len)
    return pl.pallas_call(
        _ret_sample_kernel,
        grid=(db // ns,),
        in_specs=[tok, tok, tok, tok, st,
                  _const_spec((RET_HEADS, RET_CHUNK, RET_CHUNK)),
                  _const_spec((RET_CHUNK, RET_WIDTH)), _const_spec((RET_CHUNK, RET_WIDTH)),
                  _const_spec((RET_HEADS, 1, LANES))],
        out_specs=[tok, st],
        out_shape=[jax.ShapeDtypeStruct(qr.shape, F32), jax.ShapeDtypeStruct(state.shape, F32)],
        compiler_params=pltpu.CompilerParams(dimension_semantics=("arbitrary",)),
        name="ret_sample",
    )(qr, kr, vr, gr, state, decay, qd, kd, cd)


def _moba_sample_kernel(pt_ref, q_ref, kn_ref, vn_ref, *refs, n_pages, page, step_len):
    del pt_ref
    k_refs, v_refs, o_ref = refs[:n_pages], refs[n_pages:2 * n_pages], refs[2 * n_pages]
    hd = ATTN_HEAD_DIM
    ppb = MOBA_BLOCK // page
    n_full = n_pages // ppb
    rows = step_len * ATTN_HEADS
    q = q_ref[0]
    kn = kn_ref[0]
    vn = vn_ref[0]
    qrep = jnp.concatenate([jnp.broadcast_to(q[t:t + 1, :], (ATTN_HEADS, ATTN_WIDTH)) for t in range(step_len)], axis=0)
    r_i = lax.broadcasted_iota(jnp.int32, (rows, ATTN_WIDTH), 0)
    c_i = lax.broadcasted_iota(jnp.int32, (rows, ATTN_WIDTH), 1)
    own_head = (c_i // hd) == (r_i % ATTN_HEADS)
    qbd = jnp.where(own_head, qrep, 0.0)

    lane = lax.broadcasted_iota(jnp.int32, (rows, LANES), 1)
    lane_f = lane.astype(F32)
    kt = lambda p: k_refs[p][0].reshape(ATTN_WIDTH, page)
    vt = lambda p: v_refs[p][0].reshape(ATTN_WIDTH, page)

    gate = jnp.full((rows, LANES), NEG_INF, F32)
    for b in range(n_full):
        ksum = kt(b * ppb)
        for pp in range(1, ppb):
            ksum = ksum + kt(b * ppb + pp)
        g = jnp.sum(_dot3(qbd, ksum), axis=-1, keepdims=True) * (1.0 / MOBA_BLOCK)
        gate = jnp.where(lane == b, g, gate)
    sel = jnp.logical_and(_top_k_lanes(gate, lane_f, min(MOBA_TOP_K, n_full)), lane < n_full)
    bias = jnp.where(sel, 0.0, NEG_INF)

    qs = qbd * (hd ** -0.5)
    qsb = qs.astype(BF16)
    s_pages = []
    for p in range(n_pages):
        b_col = jnp.max(jnp.where(lane == p // ppb, bias, NEG_INF), axis=-1, keepdims=True)
        s_pages.append(_dot(qsb, kt(p).astype(BF16)) + b_col)
    tok_of_row = lax.broadcasted_iota(jnp.int32, (rows, 1), 0) // ATTN_HEADS
    s_own = []
    for t in range(step_len):
        s_t = jnp.sum(qs * kn[t:t + 1, :], axis=-1, keepdims=True)
        s_own.append(jnp.where(tok_of_row >= t, s_t, NEG_INF))

    m = functools.reduce(jnp.maximum, s_own)
    for s in s_pages:
        m = jnp.maximum(m, jnp.max(s, axis=-1, keepdims=True))
    l = jnp.zeros((rows, 1), F32)
    o = jnp.zeros((rows, ATTN_WIDTH), F32)
    for p in range(n_pages):
        e = jnp.exp(s_pages[p] - m)
        l = l + jnp.sum(e, axis=-1, keepdims=True)
        o = o + _dot_nt(e.astype(BF16), vt(p).astype(BF16))
    for t in range(step_len):
        e = jnp.exp(s_own[t] - m)
        l = l + e
        o = o + e * vn[t:t + 1, :]
    o = jnp.where(own_head, o, 0.0) * (1.0 / l)
    o_ref[0] = jnp.zeros(o_ref.shape[1:], F32)
    for t in range(step_len):
        o_ref[0, t:t + 1, :] = jnp.sum(o[t * ATTN_HEADS:(t + 1) * ATTN_HEADS, :], axis=0, keepdims=True)


def _moba_sample(qa, ka, va, cache_kt, cache_vt, page_table, step_len):
    db, tp, _ = qa.shape
    n_pages = page_table.shape[1]
    page = cache_kt.shape[-1]
    assert (n_pages * page) % MOBA_BLOCK == 0, "past length must be whole key blocks"
    assert MOBA_BLOCK % page == 0
    tok = pl.BlockSpec((1, tp, ATTN_WIDTH), lambda b, pt: (b, 0, 0))
    page_specs = [pl.BlockSpec((1, ATTN_HEADS, ATTN_HEAD_DIM, page), functools.partial(lambda b, pt, p: (pt[b, p], 0, 0, 0), p=p))
                  for p in range(n_pages)]
    grid_spec = pltpu.PrefetchScalarGridSpec(
        num_scalar_prefetch=1,
        grid=(db,),
        in_specs=[tok, tok, tok] + page_specs + page_specs,
        out_specs=tok,
    )
    return pl.pallas_call(
        functools.partial(_moba_sample_kernel, n_pages=n_pages, page=page, step_len=step_len),
        grid_spec=grid_spec,
        out_shape=jax.ShapeDtypeStruct(qa.shape, F32),
        compiler_params=pltpu.CompilerParams(dimension_semantics=("arbitrary",), vmem_limit_bytes=VMEM_LIMIT),
        name="moba_sample",
    )(page_table, qa, ka, va, *([cache_kt] * n_pages), *([cache_vt] * n_pages))


def _rope_tables(pos):
    half = ROPE_DIMS // 2
    inv = ROPE_THETA ** (-jnp.arange(half, dtype=F32) * 2.0 / ROPE_DIMS)
    ang = pos.astype(F32)[:, None] * inv[None, :]
    cos, sin = jnp.cos(ang), jnp.sin(ang)
    j = jnp.arange(LANES) % ATTN_HEAD_DIM
    first, second = j < half, jnp.logical_and(j >= half, j < ROPE_DIMS)
    cj, sj = cos[:, j % half], sin[:, j % half]
    return (jnp.where(jnp.logical_or(first, second), cj, 1.0),
            jnp.where(first, -sj, 0.0),
            jnp.where(second, sj, 0.0))


def _ret_rot_tables(pos):
    inv = 1.0 / (RET_ROT_BASE ** jnp.linspace(0.0, 1.0, RET_HEAD_DIM // 2, dtype=F32))
    ang = pos.astype(F32)[:, None] * inv[None, :]
    cos, sin = jnp.cos(ang), jnp.sin(ang)
    lane = jnp.arange(LANES)
    cl, sl = cos[:, lane // 2], sin[:, lane // 2]
    even = lane % 2 == 0
    return cl, jnp.where(even, -sl, 0.0), jnp.where(even, 0.0, sl)


def _layer_weights(l, norm_ffn1_w, ffn1_w_gate, ffn1_w_up, ffn1_w_down, norm_mix_w, w_in, q_norm_w, k_norm_w, w_out,
                   norm_ffn2_w, ffn2_w_gate, ffn2_w_up, ffn2_w_down):
    head = jnp.arange(MXU_TILE) // ATTN_HEAD_DIM
    return {
        "n1": norm_ffn1_w[l][None], "nm": norm_mix_w[l][None], "n2": norm_ffn2_w[l][None],
        "wg1": ffn1_w_gate[l].astype(BF16), "wu1": ffn1_w_up[l].astype(BF16), "wd1": ffn1_w_down[l].astype(BF16),
        "wg2": ffn2_w_gate[l].astype(BF16), "wu2": ffn2_w_up[l].astype(BF16), "wd2": ffn2_w_down[l].astype(BF16),
        "win": w_in[l].astype(BF16), "wo": w_out[l].astype(BF16),
        "qn": jnp.tile(q_norm_w[l], ATTN_HEADS)[None], "kn": jnp.tile(k_norm_w[l], ATTN_HEADS)[None],
        "bd": jnp.where(head[:, None] == head[None, :], 1.0 / ATTN_HEAD_DIM, 0.0).astype(BF16),
    }


def kernel(x_prompt, x_sample, cache_k, cache_v, state_ret, page_table, c_prompt, c_sample, w_ada, b_ada, norm_ffn1_w, ffn1_w_gate, ffn1_w_up, ffn1_w_down, norm_mix_w, w_in, q_norm_w, k_norm_w, w_out, norm_ffn2_w, ffn2_w_gate, ffn2_w_up, ffn2_w_down):
    batch, seq, _ = x_prompt.shape
    db, step_len, _ = x_sample.shape
    depth = w_ada.shape[0]
    n_pages, page = page_table.shape[1], cache_k.shape[2]
    past_len = n_pages * page
    tm_p = 512
    tp = SUBLANES
    assert seq % tm_p == 0 and step_len <= tp and db % SUBLANES == 0

    tabs_p = [t.reshape(seq // tm_p, tm_p, LANES)
              for t in _rope_tables(jnp.arange(seq)) + _ret_rot_tables(jnp.arange(seq))]
    pos_s = past_len + jnp.arange(step_len)
    tabs_s = [t.reshape(step_len, 1, LANES) for t in _rope_tables(pos_s) + _ret_rot_tables(pos_s)]

    hp = x_prompt.reshape(batch * seq, D_MODEL)
    hs = x_sample.transpose(1, 0, 2).reshape(step_len * db, D_MODEL)
    outs = [[] for _ in range(6)]
    for l in range(depth):
        lw = _layer_weights(l, norm_ffn1_w, ffn1_w_gate, ffn1_w_up, ffn1_w_down, norm_mix_w, w_in, q_norm_w, k_norm_w,
                            w_out, norm_ffn2_w, ffn2_w_gate, ffn2_w_up, ffn2_w_down)
        mods = _mods(jnp.concatenate([c_prompt, c_sample], axis=0), w_ada[l], b_ada[l])
        mods_p = mods[:batch].reshape(batch, N_MODS, 1, D_MODEL)
        mods_s = mods[batch:].reshape(db, N_MODS, D_MODEL).transpose(1, 0, 2)[None]

        tiles_per_seq = seq // tm_p
        h1, qa, kt, vt, qr, kr, vr, gr = _ffn_proj(
            hp, mods_p, tabs_p, lw, tm=tm_p, mod_idx=lambda i: i // tiles_per_seq, tab_idx=lambda i: i % tiles_per_seq,
            kv_groups=batch, kv_idx=lambda i: (i // tiles_per_seq, 0, i % tiles_per_seq))
        q_norm2_max = jnp.max(jnp.square(q_norm_w[l])) * (LOG2_E * LOG2_E)
        oa = _moba_prompt(qa, kt, vt, q_norm2_max, batch, seq)
        o_ret, st_p = _ret_prompt(qr, kr, vr, gr, batch, seq, tr=512)
        hp = _out_ffn(h1, oa, o_ret, mods_p, lw, tm=tm_p, mod_idx=lambda i: i // tiles_per_seq)

        h1s, qas, kts, vts, qrs, krs, vrs, grs = _ffn_proj(hs, mods_s, tabs_s, lw, tm=db, mod_idx=lambda i: 0,
                                                           tab_idx=lambda i: i, kv_groups=step_len, kv_idx=lambda i: (i, 0, 0))

        def seq_major(t):
            t = t.reshape(step_len, db, ATTN_WIDTH).transpose(1, 0, 2)
            return jnp.pad(t, ((0, 0), (0, tp - step_len), (0, 0)))

        def seq_major_t(t):
            return jnp.pad(t.transpose(2, 0, 1), ((0, 0), (0, tp - step_len), (0, 0)))

        def token_major(t):
            return t[:, :step_len].transpose(1, 0, 2).reshape(step_len * db, ATTN_WIDTH)

        cache_kt = cache_k[l].transpose(0, 2, 3, 1)
        cache_vt = cache_v[l].transpose(0, 2, 3, 1)
        oas = _moba_sample(seq_major(qas), seq_major_t(kts), seq_major_t(vts), cache_kt, cache_vt, page_table, step_len)
        o_rets, st_s = _ret_sample(seq_major(qrs), seq_major(krs), seq_major(vrs), seq_major(grs), state_ret[l],
                                   step_len, ns=SUBLANES)
        hs = _out_ffn(h1s, token_major(oas), token_major(o_rets), mods_s, lw, tm=db, mod_idx=lambda i: 0)

        heads = (ATTN_HEADS, ATTN_HEAD_DIM)
        outs[0].append(kt.reshape(batch, *heads, seq).transpose(0, 3, 1, 2))
        outs[1].append(vt.reshape(batch, *heads, seq).transpose(0, 3, 1, 2))
        outs[2].append(kts.reshape(step_len, *heads, db).transpose(3, 0, 1, 2))
        outs[3].append(vts.reshape(step_len, *heads, db).transpose(3, 0, 1, 2))
        outs[4].append(st_p)
        outs[5].append(st_s)

    y_prompt = hp.reshape(batch, seq, D_MODEL)
    y_sample = hs.reshape(step_len, db, D_MODEL).transpose(1, 0, 2)
    return (y_prompt, y_sample) + tuple(jnp.stack(o) for o in outs)
```

```python
import functools

import jax
import jax.numpy as jnp
from jax import lax
from jax.experimental import pallas as pl
from jax.experimental.pallas import tpu as pltpu

F32 = jnp.float32
BF16 = jnp.bfloat16

D_MODEL = 1024
ATTN_HEADS = 8
ATTN_HEAD_DIM = 64
ATTN_WIDTH = ATTN_HEADS * ATTN_HEAD_DIM
RET_HEADS = 4
RET_HEAD_DIM = 128
RET_WIDTH = RET_HEADS * RET_HEAD_DIM
IN_WIDTH = 3 * ATTN_WIDTH + 4 * RET_WIDTH
MOBA_BLOCK = 256
MOBA_TOP_K = 3
ROPE_THETA = 500000.0
ROPE_DIMS = ATTN_HEAD_DIM // 4
RET_CHUNK = 128
RET_ROT_BASE = 10000.0
N_MODS = 9
EPS = 1e-6
NEG_INF = -1e30

LANES = 128
SUBLANES = 8
VMEM_LIMIT = 56 * 1024 * 1024
MXU_TILE = 256
FF_CHUNK = MXU_TILE
MOBA_GROUP = 4
LOG2_E = 1.4426950408889634
SHIFT_LANE = 63
SHIFT_LIMIT = 60.0


def _dot(a, b):
    return jnp.dot(a, b, preferred_element_type=F32)


def _dot_nt(a, b):
    return lax.dot_general(a, b, (((1,), (1,)), ((), ())), preferred_element_type=F32)


def _split(a):
    hi = a.astype(BF16)
    lo = (a - hi.astype(F32)).astype(BF16)
    return hi, lo


def _dot3(a, b):
    ah, al = _split(a)
    bh, bl = _split(b)
    return _dot(ah, bh) + (_dot(ah, bl) + _dot(al, bh))


def _dot3_nt(a, b):
    ah, al = _split(a)
    bh, bl = _split(b)
    return _dot_nt(ah, bh) + (_dot_nt(ah, bl) + _dot_nt(al, bh))


def _sigmoid(x):
    return 1.0 / (1.0 + jnp.exp(-x))


def _silu(x):
    return x * _sigmoid(x)


def _rms(x):
    return x * lax.rsqrt(jnp.mean(x * x, axis=-1, keepdims=True) + EPS)


def _top_k_lanes(g, lane_f, k):
    sel = jnp.zeros(g.shape, jnp.bool_)
    for _ in range(k):
        m = jnp.max(g, axis=-1, keepdims=True)
        idx = jnp.min(jnp.where(g == m, lane_f, 1e9), axis=-1, keepdims=True)
        pick = lane_f == idx
        sel = jnp.logical_or(sel, pick)
        g = jnp.where(pick, -jnp.inf, g)
    return sel


def _fold_rows(op, s):
    while s.shape[0] > SUBLANES:
        half = s.shape[0] // 2
        s = op(s[:half], s[half:])
    return s


def _col_max(s):
    return jnp.max(_fold_rows(jnp.maximum, s), axis=0, keepdims=True)


def _col_min(s):
    return jnp.min(_fold_rows(jnp.minimum, s), axis=0, keepdims=True)


def _top_k_rows(g, row_f, k):
    sel = jnp.zeros(g.shape, jnp.bool_)
    for _ in range(k):
        m = _col_max(g)
        idx = _col_min(jnp.where(g == m, row_f, 1e9))
        pick = row_f == idx
        sel = jnp.logical_or(sel, pick)
        g = jnp.where(pick, -jnp.inf, g)
    return sel


def _mods_kernel(c_ref, w_ref, b_ref, o_ref):
    s = _silu(c_ref[...]).astype(BF16)
    o_ref[...] = _dot(s, w_ref[...].astype(BF16)) + b_ref[...]


def _mods(c, w_ada, b_ada):
    n = c.shape[0]
    n_pad = -(-n // SUBLANES) * SUBLANES
    c = jnp.pad(c, ((0, n_pad - n), (0, 0)))
    width = w_ada.shape[1]
    tn = 9 * LANES
    out = pl.pallas_call(
        _mods_kernel,
        grid=(width // tn,),
        in_specs=[pl.BlockSpec((n_pad, D_MODEL), lambda j: (0, 0)),
                  pl.BlockSpec((D_MODEL, tn), lambda j: (0, j)),
                  pl.BlockSpec((1, tn), lambda j: (0, j))],
        out_specs=pl.BlockSpec((n_pad, tn), lambda j: (0, j)),
        out_shape=jax.ShapeDtypeStruct((n_pad, width), F32),
        name="mods",
    )(c, w_ada, b_ada.reshape(1, width))
    return out[:n]


def _swiglu_acc(xb, wg_ref, wu_ref, wd_ref):
    d_ff = wg_ref.shape[1]
    acc = None
    for c in range(d_ff // FF_CHUNK):
        sl = slice(c * FF_CHUNK, (c + 1) * FF_CHUNK)
        g = _dot(xb, wg_ref[:, sl])
        u = _dot(xb, wu_ref[:, sl])
        part = _dot((_silu(g) * u).astype(BF16), wd_ref[sl, :])
        acc = part if acc is None else acc + part
    return acc


def _const_spec(shape):
    nd = len(shape)
    return pl.BlockSpec(shape, lambda *_: (0,) * nd, pipeline_mode=pl.Buffered(1))


def _ffn_proj_kernel(x_ref, mod_ref, n1_ref, wg_ref, wu_ref, wd_ref, nm_ref, win_ref, qn_ref, kn_ref, bd_ref,
                     ca_ref, sa1_ref, sa2_ref, cr_ref, sr1_ref, sr2_ref,
                     h_ref, qa_ref, ka_ref, va_ref, qr_ref, kr_ref, vr_ref, gr_ref):
    x = x_ref[...]
    xn = (_rms(x) * n1_ref[...]) * (1.0 + mod_ref[0, 1]) + mod_ref[0, 0]
    acc = _swiglu_acc(xn.astype(BF16), wg_ref, wu_ref, wd_ref)
    h = x + 0.5 * mod_ref[0, 2] * acc
    h_ref[...] = h
    hn = (_rms(h) * nm_ref[...]) * (1.0 + mod_ref[0, 4]) + mod_ref[0, 3]
    hb = hn.astype(BF16)

    def seg(i):
        return _dot(hb, win_ref[:, i * ATTN_WIDTH:(i + 1) * ATTN_WIDTH])

    ca, sa1, sa2 = ca_ref[0], sa1_ref[0], sa2_ref[0]
    cr, sr1, sr2 = cr_ref[0], sr1_ref[0], sr2_ref[0]
    bd = bd_ref[...]

    def attn_head_norm_rope(p, w, o_ref, transposed):
        hi, lo = _split(p * p)
        ms = jnp.concatenate([_dot(hi[:, c:c + MXU_TILE], bd) + _dot(lo[:, c:c + MXU_TILE], bd)
                              for c in range(0, ATTN_WIDTH, MXU_TILE)], axis=1)
        pn = (p * lax.rsqrt(ms + EPS)) * w
        for g in range(ATTN_WIDTH // LANES):
            xg = pn[:, g * LANES:(g + 1) * LANES]
            r = xg * ca + pltpu.roll(xg, LANES - ROPE_DIMS // 2, 1) * sa1 + pltpu.roll(xg, ROPE_DIMS // 2, 1) * sa2
            if transposed:
                o_ref[0, g * LANES:(g + 1) * LANES, :] = r.T
            else:
                o_ref[:, g * LANES:(g + 1) * LANES] = r

    def ret_rotate(p, o_ref, scale):
        for g in range(RET_HEADS):
            xg = p[:, g * LANES:(g + 1) * LANES]
            r = xg * cr + pltpu.roll(xg, LANES - 1, 1) * sr1 + pltpu.roll(xg, 1, 1) * sr2
            o_ref[:, g * LANES:(g + 1) * LANES] = r if scale is None else r * scale

    attn_head_norm_rope(seg(0), qn_ref[...], qa_ref, False)
    attn_head_norm_rope(seg(1), kn_ref[...], ka_ref, True)
    va = seg(2)
    for g in range(ATTN_WIDTH // LANES):
        va_ref[0, g * LANES:(g + 1) * LANES, :] = va[:, g * LANES:(g + 1) * LANES].T
    ret_rotate(seg(3), qr_ref, None)
    ret_rotate(seg(4), kr_ref, RET_HEAD_DIM ** -0.5)
    vr_ref[...] = seg(5)
    gr_ref[...] = seg(6)


def _ffn_proj(x2d, mods, tabs, lw, *, tm, mod_idx, tab_idx, kv_groups, kv_idx):
    n = x2d.shape[0]
    rm = mods.shape[2]
    rt = tabs[0].shape[1]
    d_ff = lw["wg1"].shape[1]
    row = lambda i: (i, 0)
    tab_spec = pl.BlockSpec((1, rt, LANES), lambda i: (tab_idx(i), 0, 0))
    in_specs = [
        pl.BlockSpec((tm, D_MODEL), row),
        pl.BlockSpec((1, N_MODS, rm, D_MODEL), lambda i: (mod_idx(i), 0, 0, 0)),
        _const_spec((1, D_MODEL)),
        _const_spec((D_MODEL, d_ff)), _const_spec((D_MODEL, d_ff)), _const_spec((d_ff, D_MODEL)),
        _const_spec((1, D_MODEL)),
        _const_spec((D_MODEL, IN_WIDTH)),
        _const_spec((1, ATTN_WIDTH)), _const_spec((1, ATTN_WIDTH)),
        _const_spec((MXU_TILE, MXU_TILE)),
    ] + [tab_spec] * 6
    half = jax.ShapeDtypeStruct((n, ATTN_WIDTH), F32)
    half_t = jax.ShapeDtypeStruct((kv_groups, ATTN_WIDTH, n // kv_groups), F32)
    half_spec = pl.BlockSpec((tm, ATTN_WIDTH), row)
    half_t_spec = pl.BlockSpec((1, ATTN_WIDTH, tm), lambda i: kv_idx(i))
    out_shape = [jax.ShapeDtypeStruct((n, D_MODEL), F32), half, half_t, half_t] + [half] * 4
    out_specs = [pl.BlockSpec((tm, D_MODEL), row), half_spec, half_t_spec, half_t_spec] + [half_spec] * 4
    return pl.pallas_call(
        _ffn_proj_kernel,
        grid=(n // tm,),
        in_specs=in_specs,
        out_specs=out_specs,
        out_shape=out_shape,
        compiler_params=pltpu.CompilerParams(dimension_semantics=("arbitrary",), vmem_limit_bytes=VMEM_LIMIT),
        name="ffn_proj",
    )(x2d, mods, lw["n1"], lw["wg1"], lw["wu1"], lw["wd1"], lw["nm"], lw["win"], lw["qn"], lw["kn"], lw["bd"], *tabs)


def _out_ffn_kernel(h_ref, oa_ref, or_ref, mod_ref, wo_ref, n2_ref, wg_ref, wu_ref, wd_ref, y_ref):
    mix = _dot(oa_ref[...].astype(BF16), wo_ref[:ATTN_WIDTH, :]) + _dot(or_ref[...].astype(BF16), wo_ref[ATTN_WIDTH:, :])
    h = h_ref[...] + mod_ref[0, 5] * mix
    hn = (_rms(h) * n2_ref[...]) * (1.0 + mod_ref[0, 7]) + mod_ref[0, 6]
    acc = _swiglu_acc(hn.astype(BF16), wg_ref, wu_ref, wd_ref)
    y_ref[...] = h + 0.5 * mod_ref[0, 8] * acc


def _out_ffn(h2d, oa, o_ret, mods, lw, *, tm, mod_idx):
    n = h2d.shape[0]
    rm = mods.shape[2]
    d_ff = lw["wg2"].shape[1]
    row = lambda i: (i, 0)
    return pl.pallas_call(
        _out_ffn_kernel,
        grid=(n // tm,),
        in_specs=[
            pl.BlockSpec((tm, D_MODEL), row),
            pl.BlockSpec((tm, ATTN_WIDTH), row),
            pl.BlockSpec((tm, RET_WIDTH), row),
            pl.BlockSpec((1, N_MODS, rm, D_MODEL), lambda i: (mod_idx(i), 0, 0, 0)),
            _const_spec((D_MODEL, D_MODEL)),
            _const_spec((1, D_MODEL)),
            _const_spec((D_MODEL, d_ff)), _const_spec((D_MODEL, d_ff)), _const_spec((d_ff, D_MODEL)),
        ],
        out_specs=pl.BlockSpec((tm, D_MODEL), row),
        out_shape=jax.ShapeDtypeStruct((n, D_MODEL), F32),
        compiler_params=pltpu.CompilerParams(dimension_semantics=("arbitrary",), vmem_limit_bytes=VMEM_LIMIT),
        name="out_ffn",
    )(h2d, oa, o_ret, mods, lw["wo"], lw["n2"], lw["wg2"], lw["wu2"], lw["wd2"])


def _moba_prompt_kernel(q_ref, qn_ref, kt_ref, vt_in_ref, qn2_ref, o_ref,
                        kaug_ref, vt_ref, km_ref, bound_ref, flag_ref, acc_ref, shift_ref, bias_ref, *, nb):
    qi = pl.program_id(2)
    tq = MOBA_BLOCK
    hd = ATTN_HEAD_DIM
    grp = MOBA_GROUP
    lane = lax.broadcasted_iota(jnp.int32, (tq, LANES), 1)
    in_head = (lane < hd, lane >= hd)
    field_off = (hd, 0)

    @pl.when(qi == 0)
    def _prepare():
        km_ref[...] = jnp.zeros(km_ref.shape, F32)
        feat = lax.broadcasted_iota(jnp.int32, (LANES, tq), 0)
        feat_in_head = (feat < hd, feat >= hd)
        kn2 = [jnp.zeros((1, 1), F32)] * 2
        for j in range(nb):
            kb = kt_ref[0, :, j * tq:(j + 1) * tq].T
            vtb = vt_in_ref[0, :, j * tq:(j + 1) * tq]
            km_ref[hd + j:hd + j + 1, :] = jnp.mean(kb, axis=0, keepdims=True)
            sq = kb * kb
            for h in range(2):
                norm2 = jnp.sum(jnp.where(in_head[h], sq, 0.0), axis=1, keepdims=True)
                kn2[h] = jnp.maximum(kn2[h], jnp.max(norm2, axis=0, keepdims=True))
                off = field_off[h]
                field = jnp.where(jnp.logical_or(lane == off + j, lane == off + SHIFT_LANE), 1.0, 0.0)
                kaug_ref[h, j * tq:(j + 1) * tq, :] = jnp.where(in_head[h], kb, field).astype(BF16)
                vt_ref[h, j] = jnp.where(feat_in_head[h], vtb, 1.0).astype(BF16)
        bound = [jnp.sqrt(qn2_ref[0:1, 0:1] * kn2[h]) for h in range(2)]
        for h in range(2):
            bound_ref[h] = jnp.broadcast_to(bound[h], bound_ref.shape[1:])
        flag_ref[0] = (jnp.max(jnp.maximum(bound[0], bound[1])) <= SHIFT_LIMIT).astype(jnp.int32)

        bias_ref[...] = jnp.full(bias_ref.shape, NEG_INF, F32)

    q = q_ref[...]
    scale = hd ** -0.5 * LOG2_E
    qs = [jnp.where(in_head[h], q, 0.0) * scale for h in range(2)]

    field_row = lax.broadcasted_iota(jnp.int32, (LANES, tq), 0)
    field_row_f = field_row.astype(F32)

    def block_bias(q_tile, h, n_past):
        off = field_off[h]
        q_head = jnp.where(in_head[h], q_tile, 0.0)
        gate_t = _dot3_nt(km_ref[hd - off:hd - off + LANES, :], q_head)
        valid = jnp.logical_and(field_row >= off, field_row < off + n_past)
        sel = _top_k_rows(jnp.where(valid, gate_t, NEG_INF), field_row_f, MOBA_TOP_K)
        return jnp.where(jnp.logical_and(sel, valid), 0.0, NEG_INF).T

    key_i = lax.broadcasted_iota(jnp.int32, (tq, tq), 0)
    qry_i = lax.broadcasted_iota(jnp.int32, (tq, tq), 1)
    causal = key_i <= qry_i
    own_rows = pl.ds(pl.multiple_of(qi * tq, tq), tq)
    n_groups = (qi + grp - 1) // grp

    def group_rows(g):
        return pl.ds(pl.multiple_of(g * (grp * tq), grp * tq), grp * tq)

    bounded = flag_ref[0] == 1

    @pl.when(bounded)
    def _bound_shift():
        for h in range(2):
            shift_ref[h] = jnp.broadcast_to(bound_ref[h, 0:1, :], (tq, LANES))

    @pl.when(jnp.logical_not(bounded))
    def _exact_shift():
        for h in range(2):
            q_sel = jnp.where(in_head[h], qs[h], bias_ref[h]).astype(BF16)
            m = _col_max(jnp.where(causal, _dot_nt(kaug_ref[h, own_rows, :], qs[h].astype(BF16)), NEG_INF))
            m = lax.fori_loop(
                0, n_groups, lambda g, m: jnp.maximum(m, _col_max(_dot_nt(kaug_ref[h, group_rows(g), :], q_sel))), m)
            shift_ref[h] = jnp.broadcast_to(m, (LANES, tq)).T

    q_own, q_past = [], []
    for h in range(2):
        at_shift = lane == field_off[h] + SHIFT_LANE
        q_own.append(jnp.where(in_head[h], qs[h], jnp.where(at_shift, -shift_ref[h], 0.0)))
        q_past.append(jnp.where(jnp.logical_or(in_head[h], at_shift), q_own[h], bias_ref[h]).astype(BF16))
    acc_ref[...] = jnp.zeros(acc_ref.shape, F32)

    def add_groups(groups, with_own=False):
        scores = [[_dot_nt(kaug_ref[h, group_rows(g), :], q_past[h]) for h in range(2)] for g in groups]
        if with_own:
            own_scores = [_dot_nt(kaug_ref[h, own_rows, :], q_own[h].astype(BF16)) for h in range(2)]
            q_next = qn_ref[...]
            for h in range(2):
                bias_ref[h] = block_bias(q_next, h, qi + 1)
            for h in range(2):
                s = jnp.where(causal, own_scores[h], NEG_INF)
                acc_ref[h] += _dot(vt_ref[h, qi], jnp.exp2(s).astype(BF16))
        for g, sc in zip(groups, scores):
            for h in range(2):
                p = jnp.exp2(sc[h]).astype(BF16)
                pv = None
                for c in range(grp):
                    part = _dot(vt_ref[h, g * grp + c], p[c * tq:(c + 1) * tq, :])
                    pv = part if pv is None else pv + part
                acc_ref[h] += pv

    def pair_body(t, carry):
        add_groups([2 * t, 2 * t + 1])
        return carry

    lax.fori_loop(0, n_groups // 2, pair_body, 0)

    @pl.when(n_groups % 2 == 1)
    def _odd_tail():
        add_groups([n_groups - 1], with_own=True)

    @pl.when(n_groups % 2 == 0)
    def _even_tail():
        add_groups([], with_own=True)

    a0, a1 = acc_ref[0], acc_ref[1]
    o_t = jnp.concatenate([a0[:hd] * (1.0 / a0[hd:hd + 1]), a1[hd:] * (1.0 / a1[0:1])], axis=0)
    o_ref[...] = o_t.T


def _moba_prompt(qa, kt, vt, q_norm2_max, batch, seq):
    tq = MOBA_BLOCK
    assert seq % (tq * MOBA_GROUP) == 0
    nb = seq // tq
    assert nb <= SHIFT_LANE, "side field holds one lane per key block below the shift lane"
    n_pairs = ATTN_WIDTH // LANES
    kv_spec = pl.BlockSpec((1, LANES, seq), lambda b, hp, qi: (b, hp, 0))
    q_spec = pl.BlockSpec((tq, LANES), lambda b, hp, qi: (b * nb + qi, hp))
    q_next_spec = pl.BlockSpec((tq, LANES), lambda b, hp, qi: (b * nb + jnp.minimum(qi + 1, nb - 1), hp))
    return pl.pallas_call(
        functools.partial(_moba_prompt_kernel, nb=nb),
        grid=(batch, n_pairs, nb),
        in_specs=[q_spec, q_next_spec, kv_spec, kv_spec, _const_spec((1, LANES))],
        out_specs=q_spec,
        out_shape=jax.ShapeDtypeStruct(qa.shape, F32),
        scratch_shapes=[
            pltpu.VMEM((2, seq, LANES), BF16),
            pltpu.VMEM((2, nb, LANES, tq), BF16),
            pltpu.VMEM((ATTN_HEAD_DIM + LANES, LANES), F32),
            pltpu.VMEM((2, SUBLANES, LANES), F32),
            pltpu.SMEM((1,), jnp.int32),
            pltpu.VMEM((2, LANES, tq), F32),
            pltpu.VMEM((2, tq, LANES), F32),
            pltpu.VMEM((2, tq, LANES), F32),
        ],
        compiler_params=pltpu.CompilerParams(dimension_semantics=("arbitrary", "arbitrary", "arbitrary"),
                                             vmem_limit_bytes=VMEM_LIMIT),
        name="moba_prompt",
    )(qa, qa, kt, vt, jnp.broadcast_to(q_norm2_max, (1, LANES)).astype(F32))


def _head_norm_gate(out, g):
    on = out * lax.rsqrt(jnp.mean(out * out, axis=-1, keepdims=True) + EPS)
    return on * _silu(g)


def _ret_prompt_kernel(q_ref, k_ref, v_ref, g_ref, dec_ref, qd_ref, kd_ref, cd_ref, o_ref, st_ref, s_ref):
    i = pl.program_id(1)
    c_len = RET_CHUNK

    @pl.when(i == 0)
    def _init():
        s_ref[...] = jnp.zeros(s_ref.shape, F32)

    pairs = [(c, h) for c in range(q_ref.shape[0] // c_len) for h in range(RET_HEADS)]
    rows = lambda c: slice(c * c_len, (c + 1) * c_len)
    cols = lambda h: slice(h * LANES, (h + 1) * LANES)
    scores, updates = {}, {}
    for c, h in pairs:
        q, k, vb = q_ref[rows(c), cols(h)], k_ref[rows(c), cols(h)], v_ref[rows(c), cols(h)].astype(BF16)
        scores[c, h] = (_dot_nt(q.astype(BF16), k.astype(BF16)) * dec_ref[h]).astype(BF16)
        updates[c, h] = _dot((k * kd_ref[:, cols(h)]).T.astype(BF16), vb)
    states = {}
    for h in range(RET_HEADS):
        st = s_ref[h]
        for c in range(q_ref.shape[0] // c_len):
            states[c, h] = st
            st = cd_ref[h] * st + updates[c, h]
        s_ref[h] = st
    for c, h in pairs:
        q, vb = q_ref[rows(c), cols(h)], v_ref[rows(c), cols(h)].astype(BF16)
        out = _dot(scores[c, h], vb) + _dot((q * qd_ref[:, cols(h)]).astype(BF16), states[c, h].astype(BF16))
        o_ref[rows(c), cols(h)] = _head_norm_gate(out, g_ref[rows(c), cols(h)])

    @pl.when(i == pl.num_programs(1) - 1)
    def _flush():
        st_ref[0] = s_ref[...]


def _ret_tables(chunk_len):
    lg = jnp.log(1.0 - 2.0 ** (-5.0 - jnp.arange(RET_HEADS, dtype=F32)))
    i = jnp.arange(RET_CHUNK, dtype=F32)
    diff = i[:, None] - i[None, :]
    decay = jnp.where(diff >= 0, jnp.exp(jnp.maximum(diff, 0.0)[None] * lg[:, None, None]), 0.0)
    qd = jnp.exp((i + 1.0)[None] * lg[:, None])
    kd = jnp.exp((chunk_len - 1.0 - i)[None] * lg[:, None])
    widen = lambda t: jnp.repeat(t.T, RET_HEAD_DIM, axis=1)
    cd = jnp.broadcast_to(jnp.exp(chunk_len * lg)[:, None, None], (RET_HEADS, 1, LANES))
    return decay, widen(qd), widen(kd), cd


def _ret_prompt(qr, kr, vr, gr, batch, seq, *, tr):
    nt = seq // tr
    row = pl.BlockSpec((tr, RET_WIDTH), lambda b, i: (b * nt + i, 0))
    decay, qd, kd, cd = _ret_tables(RET_CHUNK)
    return pl.pallas_call(
        _ret_prompt_kernel,
        grid=(batch, nt),
        in_specs=[row, row, row, row,
                  _const_spec((RET_HEADS, RET_CHUNK, RET_CHUNK)),
                  _const_spec((RET_CHUNK, RET_WIDTH)), _const_spec((RET_CHUNK, RET_WIDTH)),
                  _const_spec((RET_HEADS, 1, LANES))],
        out_specs=[row, pl.BlockSpec((1, RET_HEADS, RET_HEAD_DIM, RET_HEAD_DIM), lambda b, i: (b, 0, 0, 0))],
        out_shape=[jax.ShapeDtypeStruct(qr.shape, F32),
                   jax.ShapeDtypeStruct((batch, RET_HEADS, RET_HEAD_DIM, RET_HEAD_DIM), F32)],
        scratch_shapes=[pltpu.VMEM((RET_HEADS, RET_HEAD_DIM, RET_HEAD_DIM), F32)],
        compiler_params=pltpu.CompilerParams(dimension_semantics=("arbitrary", "arbitrary")),
        name="ret_prompt",
    )(qr, kr, vr, gr, decay, qd, kd, cd)


def _ret_sample_kernel(q_ref, k_ref, v_ref, g_ref, st_ref, dec_ref, qd_ref, kd_ref, cd_ref, o_ref, ns_ref):
    tp = q_ref.shape[1]
    zeros = jnp.zeros((RET_CHUNK - tp, LANES), F32)
    pairs = [(s, h) for s in range(q_ref.shape[0]) for h in range(RET_HEADS)]
    cols = lambda h: slice(h * LANES, (h + 1) * LANES)
    scores, values = {}, {}
    for s, h in pairs:
        k = jnp.concatenate([k_ref[s, :, cols(h)], zeros], axis=0)
        vb = jnp.concatenate([v_ref[s, :, cols(h)], zeros], axis=0).astype(BF16)
        scores[s, h] = (_dot_nt(q_ref[s, :, cols(h)].astype(BF16), k.astype(BF16)) * dec_ref[h, :tp, :]).astype(BF16)
        ns_ref[s, h] = cd_ref[h] * st_ref[s, h] + _dot((k * kd_ref[:, cols(h)]).T.astype(BF16), vb)
        values[s, h] = vb
    for s, h in pairs:
        q_dec = (q_ref[s, :, cols(h)] * qd_ref[:tp, cols(h)]).astype(BF16)
        out = _dot(scores[s, h], values[s, h]) + _dot(q_dec, st_ref[s, h].astype(BF16))
        o_ref[s, :, cols(h)] = _head_norm_gate(out, g_ref[s, :, cols(h)])


def _ret_sample(qr, kr, vr, gr, state, step_len, *, ns):
    db, tp, _ = qr.shape
    tok = pl.BlockSpec((ns, tp, RET_WIDTH), lambda i: (i, 0, 0))
    st = pl.BlockSpec((ns, RET_HEADS, RET_HEAD_DIM, RET_HEAD_DIM), lambda i: (i, 0, 0, 0))
    decay, qd, kd, cd = _ret_tables(step_len)
    return pl.pallas_call(
        _ret_sample_kernel,
        grid=(db // ns,),
        in_specs=[tok, tok, tok, tok, st,
                  _const_spec((RET_HEADS, RET_CHUNK, RET_CHUNK)),
                  _const_spec((RET_CHUNK, RET_WIDTH)), _const_spec((RET_CHUNK, RET_WIDTH)),
                  _const_spec((RET_HEADS, 1, LANES))],
        out_specs=[tok, st],
        out_shape=[jax.ShapeDtypeStruct(qr.shape, F32), jax.ShapeDtypeStruct(state.shape, F32)],
        compiler_params=pltpu.CompilerParams(dimension_semantics=("arbitrary",)),
        name="ret_sample",
    )(qr, kr, vr, gr, state, decay, qd, kd, cd)


def _moba_sample_seq(q, kn, vn, k_refs, v_refs, *, page, step_len):
    n_pages = len(k_refs)
    hd = ATTN_HEAD_DIM
    ppb = MOBA_BLOCK // page
    n_full = n_pages // ppb
    rows = step_len * ATTN_HEADS
    qrep = jnp.concatenate([jnp.broadcast_to(q[t:t + 1, :], (ATTN_HEADS, ATTN_WIDTH)) for t in range(step_len)], axis=0)
    r_i = lax.broadcasted_iota(jnp.int32, (rows, ATTN_WIDTH), 0)
    c_i = lax.broadcasted_iota(jnp.int32, (rows, ATTN_WIDTH), 1)
    own_head = (c_i // hd) == (r_i % ATTN_HEADS)
    qbd = jnp.where(own_head, qrep, 0.0)
    qs = qbd * (hd ** -0.5)

    vt = lambda p: v_refs[p][0].reshape(ATTN_WIDTH, page)
    lane = lax.broadcasted_iota(jnp.int32, (rows, LANES), 1)
    gate = jnp.full((rows, LANES), NEG_INF, F32)
    k_pages = []
    for b in range(n_full):
        ksum = None
        for p in range(b * ppb, (b + 1) * ppb):
            kt = k_refs[p][0].reshape(ATTN_WIDTH, page)
            k_pages.append(kt.astype(BF16))
            ksum = kt if ksum is None else ksum + kt
        g = jnp.sum(_dot3(qbd, ksum), axis=-1, keepdims=True) * (1.0 / MOBA_BLOCK)
        gate = jnp.where(lane == b, g, gate)
    s_past = _dot(qs.astype(BF16), jnp.concatenate(k_pages, axis=1))
    sel = jnp.logical_and(_top_k_lanes(gate, lane.astype(F32), min(MOBA_TOP_K, n_full)), lane < n_full)
    bias = jnp.where(sel, 0.0, NEG_INF)
    bias_cols = [jnp.max(jnp.where(lane == b, bias, NEG_INF), axis=-1, keepdims=True) for b in range(n_full)]
    s_past = s_past + jnp.concatenate([jnp.broadcast_to(c, (rows, MOBA_BLOCK)) for c in bias_cols], axis=1)

    tok_of_row = lax.broadcasted_iota(jnp.int32, (rows, 1), 0) // ATTN_HEADS
    s_own = []
    for t in range(step_len):
        s_t = jnp.sum(qs * kn[t:t + 1, :], axis=-1, keepdims=True)
        s_own.append(jnp.where(tok_of_row >= t, s_t, NEG_INF))

    m = jnp.maximum(functools.reduce(jnp.maximum, s_own), jnp.max(s_past, axis=-1, keepdims=True))
    e_past = jnp.exp(s_past - m)
    l = jnp.sum(e_past, axis=-1, keepdims=True)
    o = _dot_nt(e_past.astype(BF16), jnp.concatenate([vt(p).astype(BF16) for p in range(n_pages)], axis=1))
    for t in range(step_len):
        e = jnp.exp(s_own[t] - m)
        l = l + e
        o = o + e * vn[t:t + 1, :]
    o = jnp.where(own_head, o, 0.0) * (1.0 / l)
    return [jnp.sum(o[t * ATTN_HEADS:(t + 1) * ATTN_HEADS, :], axis=0, keepdims=True) for t in range(step_len)]


def _moba_sample_kernel(pt_ref, q_ref, kn_ref, vn_ref, *refs, n_pages, page, step_len):
    del pt_ref
    n_seq = q_ref.shape[0]
    o_ref = refs[2 * n_seq * n_pages]
    o_ref[...] = jnp.zeros(o_ref.shape, F32)
    for s in range(n_seq):
        k_refs = refs[s * n_pages:(s + 1) * n_pages]
        v_refs = refs[(n_seq + s) * n_pages:(n_seq + s + 1) * n_pages]
        out_rows = _moba_sample_seq(q_ref[s], kn_ref[s], vn_ref[s], k_refs, v_refs, page=page, step_len=step_len)
        for t, row in enumerate(out_rows):
            o_ref[s, t:t + 1, :] = row


def _moba_sample(qa, ka, va, cache_kt, cache_vt, page_table, step_len, *, n_seq):
    db, tp, _ = qa.shape
    n_pages = page_table.shape[1]
    page = cache_kt.shape[-1]
    assert (n_pages * page) % MOBA_BLOCK == 0, "past length must be whole key blocks"
    assert MOBA_BLOCK % page == 0 and db % n_seq == 0
    tok = pl.BlockSpec((n_seq, tp, ATTN_WIDTH), lambda b, pt: (b, 0, 0))
    page_specs = [
        pl.BlockSpec((1, ATTN_HEADS, ATTN_HEAD_DIM, page),
                     functools.partial(lambda b, pt, s, p: (pt[b * n_seq + s, p], 0, 0, 0), s=s, p=p))
        for s in range(n_seq) for p in range(n_pages)]
    grid_spec = pltpu.PrefetchScalarGridSpec(
        num_scalar_prefetch=1,
        grid=(db // n_seq,),
        in_specs=[tok, tok, tok] + page_specs + page_specs,
        out_specs=tok,
    )
    n_bufs = n_seq * n_pages
    return pl.pallas_call(
        functools.partial(_moba_sample_kernel, n_pages=n_pages, page=page, step_len=step_len),
        grid_spec=grid_spec,
        out_shape=jax.ShapeDtypeStruct(qa.shape, F32),
        compiler_params=pltpu.CompilerParams(dimension_semantics=("arbitrary",), vmem_limit_bytes=VMEM_LIMIT),
        name="moba_sample",
    )(page_table, qa, ka, va, *([cache_kt] * n_bufs), *([cache_vt] * n_bufs))


def _rope_tables(pos):
    half = ROPE_DIMS // 2
    inv = ROPE_THETA ** (-jnp.arange(half, dtype=F32) * 2.0 / ROPE_DIMS)
    ang = pos.astype(F32)[:, None] * inv[None, :]
    cos, sin = jnp.cos(ang), jnp.sin(ang)
    j = jnp.arange(LANES) % ATTN_HEAD_DIM
    first, second = j < half, jnp.logical_and(j >= half, j < ROPE_DIMS)
    cj, sj = cos[:, j % half], sin[:, j % half]
    return (jnp.where(jnp.logical_or(first, second), cj, 1.0),
            jnp.where(first, -sj, 0.0),
            jnp.where(second, sj, 0.0))


def _ret_rot_tables(pos):
    inv = 1.0 / (RET_ROT_BASE ** jnp.linspace(0.0, 1.0, RET_HEAD_DIM // 2, dtype=F32))
    ang = pos.astype(F32)[:, None] * inv[None, :]
    cos, sin = jnp.cos(ang), jnp.sin(ang)
    lane = jnp.arange(LANES)
    cl, sl = cos[:, lane // 2], sin[:, lane // 2]
    even = lane % 2 == 0
    return cl, jnp.where(even, -sl, 0.0), jnp.where(even, 0.0, sl)


def _layer_weights(l, norm_ffn1_w, ffn1_w_gate, ffn1_w_up, ffn1_w_down, norm_mix_w, w_in, q_norm_w, k_norm_w, w_out,
                   norm_ffn2_w, ffn2_w_gate, ffn2_w_up, ffn2_w_down):
    head = jnp.arange(MXU_TILE) // ATTN_HEAD_DIM
    return {
        "n1": norm_ffn1_w[l][None], "nm": norm_mix_w[l][None], "n2": norm_ffn2_w[l][None],
        "wg1": ffn1_w_gate[l].astype(BF16), "wu1": ffn1_w_up[l].astype(BF16), "wd1": ffn1_w_down[l].astype(BF16),
        "wg2": ffn2_w_gate[l].astype(BF16), "wu2": ffn2_w_up[l].astype(BF16), "wd2": ffn2_w_down[l].astype(BF16),
        "win": w_in[l].astype(BF16), "wo": w_out[l].astype(BF16),
        "qn": jnp.tile(q_norm_w[l], ATTN_HEADS)[None], "kn": jnp.tile(k_norm_w[l], ATTN_HEADS)[None],
        "bd": jnp.where(head[:, None] == head[None, :], 1.0 / ATTN_HEAD_DIM, 0.0).astype(BF16),
    }


def kernel(x_prompt, x_sample, cache_k, cache_v, state_ret, page_table, c_prompt, c_sample, w_ada, b_ada, norm_ffn1_w, ffn1_w_gate, ffn1_w_up, ffn1_w_down, norm_mix_w, w_in, q_norm_w, k_norm_w, w_out, norm_ffn2_w, ffn2_w_gate, ffn2_w_up, ffn2_w_down):
    batch, seq, _ = x_prompt.shape
    db, step_len, _ = x_sample.shape
    depth = w_ada.shape[0]
    n_pages, page = page_table.shape[1], cache_k.shape[2]
    past_len = n_pages * page
    tm_p = 512
    tp = SUBLANES
    assert seq % tm_p == 0 and step_len <= tp and db % SUBLANES == 0

    tabs_p = [t.reshape(seq // tm_p, tm_p, LANES)
              for t in _rope_tables(jnp.arange(seq)) + _ret_rot_tables(jnp.arange(seq))]
    pos_s = past_len + jnp.arange(step_len)
    tabs_s = [t.reshape(step_len, 1, LANES) for t in _rope_tables(pos_s) + _ret_rot_tables(pos_s)]

    hp = x_prompt.reshape(batch * seq, D_MODEL)
    hs = x_sample.transpose(1, 0, 2).reshape(step_len * db, D_MODEL)
    outs = [[] for _ in range(6)]
    for l in range(depth):
        lw = _layer_weights(l, norm_ffn1_w, ffn1_w_gate, ffn1_w_up, ffn1_w_down, norm_mix_w, w_in, q_norm_w, k_norm_w,
                            w_out, norm_ffn2_w, ffn2_w_gate, ffn2_w_up, ffn2_w_down)
        mods = _mods(jnp.concatenate([c_prompt, c_sample], axis=0), w_ada[l], b_ada[l])
        mods_p = mods[:batch].reshape(batch, N_MODS, 1, D_MODEL)
        mods_s = mods[batch:].reshape(db, N_MODS, D_MODEL).transpose(1, 0, 2)[None]

        tiles_per_seq = seq // tm_p
        h1, qa, kt, vt, qr, kr, vr, gr = _ffn_proj(
            hp, mods_p, tabs_p, lw, tm=tm_p, mod_idx=lambda i: i // tiles_per_seq, tab_idx=lambda i: i % tiles_per_seq,
            kv_groups=batch, kv_idx=lambda i: (i // tiles_per_seq, 0, i % tiles_per_seq))
        q_norm2_max = jnp.max(jnp.square(q_norm_w[l])) * (LOG2_E * LOG2_E)
        oa = _moba_prompt(qa, kt, vt, q_norm2_max, batch, seq)
        o_ret, st_p = _ret_prompt(qr, kr, vr, gr, batch, seq, tr=512)
        hp = _out_ffn(h1, oa, o_ret, mods_p, lw, tm=tm_p, mod_idx=lambda i: i // tiles_per_seq)

        h1s, qas, kts, vts, qrs, krs, vrs, grs = _ffn_proj(hs, mods_s, tabs_s, lw, tm=db, mod_idx=lambda i: 0,
                                                           tab_idx=lambda i: i, kv_groups=step_len, kv_idx=lambda i: (i, 0, 0))

        def seq_major(t):
            t = t.reshape(step_len, db, ATTN_WIDTH).transpose(1, 0, 2)
            return jnp.pad(t, ((0, 0), (0, tp - step_len), (0, 0)))

        def seq_major_t(t):
            return jnp.pad(t.transpose(2, 0, 1), ((0, 0), (0, tp - step_len), (0, 0)))

        def token_major(t):
            return t[:, :step_len].transpose(1, 0, 2).reshape(step_len * db, ATTN_WIDTH)

        cache_kt = cache_k[l].transpose(0, 2, 3, 1)
        cache_vt = cache_v[l].transpose(0, 2, 3, 1)
        oas = _moba_sample(seq_major(qas), seq_major_t(kts), seq_major_t(vts), cache_kt, cache_vt, page_table, step_len,
                           n_seq=2)
        o_rets, st_s = _ret_sample(seq_major(qrs), seq_major(krs), seq_major(vrs), seq_major(grs), state_ret[l],
                                   step_len, ns=SUBLANES)
        hs = _out_ffn(h1s, token_major(oas), token_major(o_rets), mods_s, lw, tm=db, mod_idx=lambda i: 0)

        heads = (ATTN_HEADS, ATTN_HEAD_DIM)
        outs[0].append(kt.reshape(batch, *heads, seq).transpose(0, 3, 1, 2))
        outs[1].append(vt.reshape(batch, *heads, seq).transpose(0, 3, 1, 2))
        outs[2].append(kts.reshape(step_len, *heads, db).transpose(3, 0, 1, 2))
        outs[3].append(vts.reshape(step_len, *heads, db).transpose(3, 0, 1, 2))
        outs[4].append(st_p)
        outs[5].append(st_s)

    y_prompt = hp.reshape(batch, seq, D_MODEL)
    y_sample = hs.reshape(step_len, db, D_MODEL).transpose(1, 0, 2)
    return (y_prompt, y_sample) + tuple(jnp.stack(o) for o in outs)
```

```python
import functools

import jax
import jax.numpy as jnp
from jax import lax
from jax.experimental import pallas as pl
from jax.experimental.pallas import tpu as pltpu

F32 = jnp.float32
BF16 = jnp.bfloat16

D_MODEL = 1024
ATTN_HEADS = 8
ATTN_HEAD_DIM = 64
ATTN_WIDTH = ATTN_HEADS * ATTN_HEAD_DIM
RET_HEADS = 4
RET_HEAD_DIM = 128
RET_WIDTH = RET_HEADS * RET_HEAD_DIM
IN_WIDTH = 3 * ATTN_WIDTH + 4 * RET_WIDTH
MOBA_BLOCK = 256
MOBA_TOP_K = 3
ROPE_THETA = 500000.0
ROPE_DIMS = ATTN_HEAD_DIM // 4
RET_CHUNK = 128
RET_ROT_BASE = 10000.0
N_MODS = 9
EPS = 1e-6
NEG_INF = -1e30

LANES = 128
SUBLANES = 8
VMEM_LIMIT = 56 * 1024 * 1024
MXU_TILE = 256
FF_CHUNK = MXU_TILE
MOBA_GROUP = 4
LOG2_E = 1.4426950408889634
SHIFT_LANE = 63
SHIFT_LIMIT = 60.0


def _dot(a, b):
    return jnp.dot(a, b, preferred_element_type=F32)


def _dot_nt(a, b):
    return lax.dot_general(a, b, (((1,), (1,)), ((), ())), preferred_element_type=F32)


def _split(a):
    hi = a.astype(BF16)
    lo = (a - hi.astype(F32)).astype(BF16)
    return hi, lo


def _dot3(a, b):
    ah, al = _split(a)
    bh, bl = _split(b)
    return _dot(ah, bh) + (_dot(ah, bl) + _dot(al, bh))


def _dot3_nt(a, b):
    ah, al = _split(a)
    bh, bl = _split(b)
    return _dot_nt(ah, bh) + (_dot_nt(ah, bl) + _dot_nt(al, bh))


def _sigmoid(x):
    return 1.0 / (1.0 + jnp.exp(-x))


def _silu(x):
    return x * _sigmoid(x)


def _rms(x):
    return x * lax.rsqrt(jnp.mean(x * x, axis=-1, keepdims=True) + EPS)


def _top_k_lanes(g, lane_f, k):
    sel = jnp.zeros(g.shape, jnp.bool_)
    for _ in range(k):
        m = jnp.max(g, axis=-1, keepdims=True)
        idx = jnp.min(jnp.where(g == m, lane_f, 1e9), axis=-1, keepdims=True)
        pick = lane_f == idx
        sel = jnp.logical_or(sel, pick)
        g = jnp.where(pick, -jnp.inf, g)
    return sel


def _fold_rows(op, s):
    while s.shape[0] > SUBLANES:
        half = s.shape[0] // 2
        s = op(s[:half], s[half:])
    return s


def _col_max(s):
    return jnp.max(_fold_rows(jnp.maximum, s), axis=0, keepdims=True)


def _col_min(s):
    return jnp.min(_fold_rows(jnp.minimum, s), axis=0, keepdims=True)


def _top_k_rows(g, row_f, k):
    sel = jnp.zeros(g.shape, jnp.bool_)
    for _ in range(k):
        m = _col_max(g)
        idx = _col_min(jnp.where(g == m, row_f, 1e9))
        pick = row_f == idx
        sel = jnp.logical_or(sel, pick)
        g = jnp.where(pick, -jnp.inf, g)
    return sel


def _mods_kernel(c_ref, w_ref, b_ref, o_ref):
    s = _silu(c_ref[...]).astype(BF16)
    o_ref[...] = _dot(s, w_ref[...].astype(BF16)) + b_ref[...]


def _mods(c, w_ada, b_ada):
    n = c.shape[0]
    n_pad = -(-n // SUBLANES) * SUBLANES
    c = jnp.pad(c, ((0, n_pad - n), (0, 0)))
    width = w_ada.shape[1]
    tn = 9 * LANES
    out = pl.pallas_call(
        _mods_kernel,
        grid=(width // tn,),
        in_specs=[pl.BlockSpec((n_pad, D_MODEL), lambda j: (0, 0)),
                  pl.BlockSpec((D_MODEL, tn), lambda j: (0, j)),
                  pl.BlockSpec((1, tn), lambda j: (0, j))],
        out_specs=pl.BlockSpec((n_pad, tn), lambda j: (0, j)),
        out_shape=jax.ShapeDtypeStruct((n_pad, width), F32),
        name="mods",
    )(c, w_ada, b_ada.reshape(1, width))
    return out[:n]


def _swiglu_acc(xb, wg_ref, wu_ref, wd_ref):
    d_ff = wg_ref.shape[1]
    acc = None
    for c in range(d_ff // FF_CHUNK):
        sl = slice(c * FF_CHUNK, (c + 1) * FF_CHUNK)
        g = _dot(xb, wg_ref[:, sl])
        u = _dot(xb, wu_ref[:, sl])
        part = _dot((_silu(g) * u).astype(BF16), wd_ref[sl, :])
        acc = part if acc is None else acc + part
    return acc


def _mod_rows(mod_ref, i, rows):
    m = mod_ref[0, i]
    reps = rows // m.shape[0]
    return m if m.shape[0] == 1 or reps == 1 else jnp.concatenate([m] * reps, axis=0)


def _const_spec(shape):
    nd = len(shape)
    return pl.BlockSpec(shape, lambda *_: (0,) * nd, pipeline_mode=pl.Buffered(1))


def _ffn_proj_kernel(x_ref, mod_ref, n1_ref, wg_ref, wu_ref, wd_ref, nm_ref, win_ref, qn_ref, kn_ref, bd_ref,
                     ca_ref, sa1_ref, sa2_ref, cr_ref, sr1_ref, sr2_ref,
                     h_ref, qa_ref, ka_ref, va_ref, qr_ref, kr_ref, vr_ref, gr_ref):
    x = x_ref[...]
    mod = lambda i: _mod_rows(mod_ref, i, x.shape[0])
    xn = (_rms(x) * n1_ref[...]) * (1.0 + mod(1)) + mod(0)
    acc = _swiglu_acc(xn.astype(BF16), wg_ref, wu_ref, wd_ref)
    h = x + 0.5 * mod(2) * acc
    h_ref[...] = h
    hn = (_rms(h) * nm_ref[...]) * (1.0 + mod(4)) + mod(3)
    hb = hn.astype(BF16)

    def seg(i):
        return _dot(hb, win_ref[:, i * ATTN_WIDTH:(i + 1) * ATTN_WIDTH])

    ca, sa1, sa2 = ca_ref[0], sa1_ref[0], sa2_ref[0]
    cr, sr1, sr2 = cr_ref[0], sr1_ref[0], sr2_ref[0]
    bd = bd_ref[...]

    def store_t(o_ref, g, r):
        width = o_ref.shape[2]
        for t in range(o_ref.shape[0]):
            o_ref[t, g * LANES:(g + 1) * LANES, :] = r[t * width:(t + 1) * width, :].T

    def attn_head_norm_rope(p, w, o_ref, transposed):
        hi, lo = _split(p * p)
        ms = jnp.concatenate([_dot(hi[:, c:c + MXU_TILE], bd) + _dot(lo[:, c:c + MXU_TILE], bd)
                              for c in range(0, ATTN_WIDTH, MXU_TILE)], axis=1)
        pn = (p * lax.rsqrt(ms + EPS)) * w
        for g in range(ATTN_WIDTH // LANES):
            xg = pn[:, g * LANES:(g + 1) * LANES]
            r = xg * ca + pltpu.roll(xg, LANES - ROPE_DIMS // 2, 1) * sa1 + pltpu.roll(xg, ROPE_DIMS // 2, 1) * sa2
            if transposed:
                store_t(o_ref, g, r)
            else:
                o_ref[:, g * LANES:(g + 1) * LANES] = r

    def ret_rotate(p, o_ref, scale):
        for g in range(RET_HEADS):
            xg = p[:, g * LANES:(g + 1) * LANES]
            r = xg * cr + pltpu.roll(xg, LANES - 1, 1) * sr1 + pltpu.roll(xg, 1, 1) * sr2
            o_ref[:, g * LANES:(g + 1) * LANES] = r if scale is None else r * scale

    attn_head_norm_rope(seg(0), qn_ref[...], qa_ref, False)
    attn_head_norm_rope(seg(1), kn_ref[...], ka_ref, True)
    va = seg(2)
    for g in range(ATTN_WIDTH // LANES):
        store_t(va_ref, g, va[:, g * LANES:(g + 1) * LANES])
    ret_rotate(seg(3), qr_ref, None)
    ret_rotate(seg(4), kr_ref, RET_HEAD_DIM ** -0.5)
    vr_ref[...] = seg(5)
    gr_ref[...] = seg(6)


def _ffn_proj(x2d, mods, tabs, lw, *, tm, mod_idx, tab_idx, kv_groups, kv_idx, kv_tiles=1):
    n = x2d.shape[0]
    rm = mods.shape[2]
    rt = tabs[0].shape[1]
    d_ff = lw["wg1"].shape[1]
    row = lambda i: (i, 0)
    tab_spec = pl.BlockSpec((1, rt, LANES), lambda i: (tab_idx(i), 0, 0))
    in_specs = [
        pl.BlockSpec((tm, D_MODEL), row),
        pl.BlockSpec((1, N_MODS, rm, D_MODEL), lambda i: (mod_idx(i), 0, 0, 0)),
        _const_spec((1, D_MODEL)),
        _const_spec((D_MODEL, d_ff)), _const_spec((D_MODEL, d_ff)), _const_spec((d_ff, D_MODEL)),
        _const_spec((1, D_MODEL)),
        _const_spec((D_MODEL, IN_WIDTH)),
        _const_spec((1, ATTN_WIDTH)), _const_spec((1, ATTN_WIDTH)),
        _const_spec((MXU_TILE, MXU_TILE)),
    ] + [tab_spec] * 6
    half = jax.ShapeDtypeStruct((n, ATTN_WIDTH), F32)
    half_t = jax.ShapeDtypeStruct((kv_groups, ATTN_WIDTH, n // kv_groups), F32)
    half_spec = pl.BlockSpec((tm, ATTN_WIDTH), row)
    half_t_spec = pl.BlockSpec((kv_tiles, ATTN_WIDTH, tm // kv_tiles), lambda i: kv_idx(i))
    out_shape = [jax.ShapeDtypeStruct((n, D_MODEL), F32), half, half_t, half_t] + [half] * 4
    out_specs = [pl.BlockSpec((tm, D_MODEL), row), half_spec, half_t_spec, half_t_spec] + [half_spec] * 4
    return pl.pallas_call(
        _ffn_proj_kernel,
        grid=(n // tm,),
        in_specs=in_specs,
        out_specs=out_specs,
        out_shape=out_shape,
        compiler_params=pltpu.CompilerParams(dimension_semantics=("arbitrary",), vmem_limit_bytes=VMEM_LIMIT),
        name="ffn_proj",
    )(x2d, mods, lw["n1"], lw["wg1"], lw["wu1"], lw["wd1"], lw["nm"], lw["win"], lw["qn"], lw["kn"], lw["bd"], *tabs)


def _out_ffn_kernel(h_ref, oa_ref, or_ref, mod_ref, wo_ref, n2_ref, wg_ref, wu_ref, wd_ref, y_ref):
    mix = _dot(oa_ref[...].astype(BF16), wo_ref[:ATTN_WIDTH, :]) + _dot(or_ref[...].astype(BF16), wo_ref[ATTN_WIDTH:, :])
    mod = lambda i: _mod_rows(mod_ref, i, h_ref.shape[0])
    h = h_ref[...] + mod(5) * mix
    hn = (_rms(h) * n2_ref[...]) * (1.0 + mod(7)) + mod(6)
    acc = _swiglu_acc(hn.astype(BF16), wg_ref, wu_ref, wd_ref)
    y_ref[...] = h + 0.5 * mod(8) * acc


def _out_ffn(h2d, oa, o_ret, mods, lw, *, tm, mod_idx):
    n = h2d.shape[0]
    rm = mods.shape[2]
    d_ff = lw["wg2"].shape[1]
    row = lambda i: (i, 0)
    return pl.pallas_call(
        _out_ffn_kernel,
        grid=(n // tm,),
        in_specs=[
            pl.BlockSpec((tm, D_MODEL), row),
            pl.BlockSpec((tm, ATTN_WIDTH), row),
            pl.BlockSpec((tm, RET_WIDTH), row),
            pl.BlockSpec((1, N_MODS, rm, D_MODEL), lambda i: (mod_idx(i), 0, 0, 0)),
            _const_spec((D_MODEL, D_MODEL)),
            _const_spec((1, D_MODEL)),
            _const_spec((D_MODEL, d_ff)), _const_spec((D_MODEL, d_ff)), _const_spec((d_ff, D_MODEL)),
        ],
        out_specs=pl.BlockSpec((tm, D_MODEL), row),
        out_shape=jax.ShapeDtypeStruct((n, D_MODEL), F32),
        compiler_params=pltpu.CompilerParams(dimension_semantics=("arbitrary",), vmem_limit_bytes=VMEM_LIMIT),
        name="out_ffn",
    )(h2d, oa, o_ret, mods, lw["wo"], lw["n2"], lw["wg2"], lw["wu2"], lw["wd2"])


def _moba_prompt_kernel(q_ref, qn_ref, kt_ref, vt_in_ref, qn2_ref, o_ref,
                        kaug_ref, vt_ref, km_ref, bound_ref, flag_ref, acc_ref, shift_ref, bias_ref, *, nb):
    qi = pl.program_id(2)
    tq = MOBA_BLOCK
    hd = ATTN_HEAD_DIM
    grp = MOBA_GROUP
    lane = lax.broadcasted_iota(jnp.int32, (tq, LANES), 1)
    in_head = (lane < hd, lane >= hd)
    field_off = (hd, 0)

    @pl.when(qi == 0)
    def _prepare():
        km_ref[...] = jnp.zeros(km_ref.shape, F32)
        feat = lax.broadcasted_iota(jnp.int32, (LANES, tq), 0)
        feat_in_head = (feat < hd, feat >= hd)
        kn2 = [jnp.zeros((1, 1), F32)] * 2
        for j in range(nb):
            kb = kt_ref[0, :, j * tq:(j + 1) * tq].T
            vtb = vt_in_ref[0, :, j * tq:(j + 1) * tq]
            km_ref[hd + j:hd + j + 1, :] = jnp.mean(kb, axis=0, keepdims=True)
            sq = kb * kb
            for h in range(2):
                norm2 = jnp.sum(jnp.where(in_head[h], sq, 0.0), axis=1, keepdims=True)
                kn2[h] = jnp.maximum(kn2[h], jnp.max(norm2, axis=0, keepdims=True))
                off = field_off[h]
                field = jnp.where(jnp.logical_or(lane == off + j, lane == off + SHIFT_LANE), 1.0, 0.0)
                kaug_ref[h, j * tq:(j + 1) * tq, :] = jnp.where(in_head[h], kb, field).astype(BF16)
                vt_ref[h, j] = jnp.where(feat_in_head[h], vtb, 1.0).astype(BF16)
        bound = [jnp.sqrt(qn2_ref[0:1, 0:1] * kn2[h]) for h in range(2)]
        for h in range(2):
            bound_ref[h] = jnp.broadcast_to(bound[h], bound_ref.shape[1:])
        flag_ref[0] = (jnp.max(jnp.maximum(bound[0], bound[1])) <= SHIFT_LIMIT).astype(jnp.int32)

        bias_ref[...] = jnp.full(bias_ref.shape, NEG_INF, F32)

    q = q_ref[...]
    scale = hd ** -0.5 * LOG2_E
    qs = [jnp.where(in_head[h], q, 0.0) * scale for h in range(2)]

    field_row = lax.broadcasted_iota(jnp.int32, (LANES, tq), 0)
    field_row_f = field_row.astype(F32)

    def block_bias(q_tile, h, n_past):
        off = field_off[h]
        q_head = jnp.where(in_head[h], q_tile, 0.0)
        gate_t = _dot3_nt(km_ref[hd - off:hd - off + LANES, :], q_head)
        valid = jnp.logical_and(field_row >= off, field_row < off + n_past)
        sel = _top_k_rows(jnp.where(valid, gate_t, NEG_INF), field_row_f, MOBA_TOP_K)
        return jnp.where(jnp.logical_and(sel, valid), 0.0, NEG_INF).T

    key_i = lax.broadcasted_iota(jnp.int32, (tq, tq), 0)
    qry_i = lax.broadcasted_iota(jnp.int32, (tq, tq), 1)
    causal = key_i <= qry_i
    own_rows = pl.ds(pl.multiple_of(qi * tq, tq), tq)
    n_groups = (qi + grp - 1) // grp

    def group_rows(g):
        return pl.ds(pl.multiple_of(g * (grp * tq), grp * tq), grp * tq)

    bounded = flag_ref[0] == 1

    @pl.when(bounded)
    def _bound_shift():
        for h in range(2):
            shift_ref[h] = jnp.broadcast_to(bound_ref[h, 0:1, :], (tq, LANES))

    @pl.when(jnp.logical_not(bounded))
    def _exact_shift():
        for h in range(2):
            q_sel = jnp.where(in_head[h], qs[h], bias_ref[h]).astype(BF16)
            m = _col_max(jnp.where(causal, _dot_nt(kaug_ref[h, own_rows, :], qs[h].astype(BF16)), NEG_INF))
            m = lax.fori_loop(
                0, n_groups, lambda g, m: jnp.maximum(m, _col_max(_dot_nt(kaug_ref[h, group_rows(g), :], q_sel))), m)
            shift_ref[h] = jnp.broadcast_to(m, (LANES, tq)).T

    q_own, q_past = [], []
    for h in range(2):
        at_shift = lane == field_off[h] + SHIFT_LANE
        q_own.append(jnp.where(in_head[h], qs[h], jnp.where(at_shift, -shift_ref[h], 0.0)))
        q_past.append(jnp.where(jnp.logical_or(in_head[h], at_shift), q_own[h], bias_ref[h]).astype(BF16))
    acc_ref[...] = jnp.zeros(acc_ref.shape, F32)

    def add_groups(groups, with_own=False):
        scores = [[_dot_nt(kaug_ref[h, group_rows(g), :], q_past[h]) for h in range(2)] for g in groups]
        if with_own:
            own_scores = [_dot_nt(kaug_ref[h, own_rows, :], q_own[h].astype(BF16)) for h in range(2)]
            q_next = qn_ref[...]
            for h in range(2):
                bias_ref[h] = block_bias(q_next, h, qi + 1)
            for h in range(2):
                s = jnp.where(causal, own_scores[h], NEG_INF)
                acc_ref[h] += _dot(vt_ref[h, qi], jnp.exp2(s).astype(BF16))
        for g, sc in zip(groups, scores):
            for h in range(2):
                p = jnp.exp2(sc[h]).astype(BF16)
                pv = None
                for c in range(grp):
                    part = _dot(vt_ref[h, g * grp + c], p[c * tq:(c + 1) * tq, :])
                    pv = part if pv is None else pv + part
                acc_ref[h] += pv

    def pair_body(t, carry):
        add_groups([2 * t, 2 * t + 1])
        return carry

    lax.fori_loop(0, n_groups // 2, pair_body, 0)

    @pl.when(n_groups % 2 == 1)
    def _odd_tail():
        add_groups([n_groups - 1], with_own=True)

    @pl.when(n_groups % 2 == 0)
    def _even_tail():
        add_groups([], with_own=True)

    a0, a1 = acc_ref[0], acc_ref[1]
    o_t = jnp.concatenate([a0[:hd] * (1.0 / a0[hd:hd + 1]), a1[hd:] * (1.0 / a1[0:1])], axis=0)
    o_ref[...] = o_t.T


def _moba_prompt(qa, kt, vt, q_norm2_max, batch, seq):
    tq = MOBA_BLOCK
    assert seq % (tq * MOBA_GROUP) == 0
    nb = seq // tq
    assert nb <= SHIFT_LANE, "side field holds one lane per key block below the shift lane"
    n_pairs = ATTN_WIDTH // LANES
    kv_spec = pl.BlockSpec((1, LANES, seq), lambda b, hp, qi: (b, hp, 0))
    q_spec = pl.BlockSpec((tq, LANES), lambda b, hp, qi: (b * nb + qi, hp))
    q_next_spec = pl.BlockSpec((tq, LANES), lambda b, hp, qi: (b * nb + jnp.minimum(qi + 1, nb - 1), hp))
    return pl.pallas_call(
        functools.partial(_moba_prompt_kernel, nb=nb),
        grid=(batch, n_pairs, nb),
        in_specs=[q_spec, q_next_spec, kv_spec, kv_spec, _const_spec((1, LANES))],
        out_specs=q_spec,
        out_shape=jax.ShapeDtypeStruct(qa.shape, F32),
        scratch_shapes=[
            pltpu.VMEM((2, seq, LANES), BF16),
            pltpu.VMEM((2, nb, LANES, tq), BF16),
            pltpu.VMEM((ATTN_HEAD_DIM + LANES, LANES), F32),
            pltpu.VMEM((2, SUBLANES, LANES), F32),
            pltpu.SMEM((1,), jnp.int32),
            pltpu.VMEM((2, LANES, tq), F32),
            pltpu.VMEM((2, tq, LANES), F32),
            pltpu.VMEM((2, tq, LANES), F32),
        ],
        compiler_params=pltpu.CompilerParams(dimension_semantics=("arbitrary", "arbitrary", "arbitrary"),
                                             vmem_limit_bytes=VMEM_LIMIT),
        name="moba_prompt",
    )(qa, qa, kt, vt, jnp.broadcast_to(q_norm2_max, (1, LANES)).astype(F32))


def _head_norm_gate(out, g):
    on = out * lax.rsqrt(jnp.mean(out * out, axis=-1, keepdims=True) + EPS)
    return on * _silu(g)


def _ret_prompt_kernel(q_ref, k_ref, v_ref, g_ref, dec_ref, qd_ref, kd_ref, cd_ref, o_ref, st_ref, s_ref):
    i = pl.program_id(1)
    c_len = RET_CHUNK

    @pl.when(i == 0)
    def _init():
        s_ref[...] = jnp.zeros(s_ref.shape, F32)

    pairs = [(c, h) for c in range(q_ref.shape[0] // c_len) for h in range(RET_HEADS)]
    rows = lambda c: slice(c * c_len, (c + 1) * c_len)
    cols = lambda h: slice(h * LANES, (h + 1) * LANES)
    scores, updates = {}, {}
    for c, h in pairs:
        q, k, vb = q_ref[rows(c), cols(h)], k_ref[rows(c), cols(h)], v_ref[rows(c), cols(h)].astype(BF16)
        scores[c, h] = (_dot_nt(q.astype(BF16), k.astype(BF16)) * dec_ref[h]).astype(BF16)
        updates[c, h] = _dot((k * kd_ref[:, cols(h)]).T.astype(BF16), vb)
    states = {}
    for h in range(RET_HEADS):
        st = s_ref[h]
        for c in range(q_ref.shape[0] // c_len):
            states[c, h] = st
            st = cd_ref[h] * st + updates[c, h]
        s_ref[h] = st
    for c, h in pairs:
        q, vb = q_ref[rows(c), cols(h)], v_ref[rows(c), cols(h)].astype(BF16)
        out = _dot(scores[c, h], vb) + _dot((q * qd_ref[:, cols(h)]).astype(BF16), states[c, h].astype(BF16))
        o_ref[rows(c), cols(h)] = _head_norm_gate(out, g_ref[rows(c), cols(h)])

    @pl.when(i == pl.num_programs(1) - 1)
    def _flush():
        st_ref[0] = s_ref[...]


def _ret_tables(chunk_len):
    lg = jnp.log(1.0 - 2.0 ** (-5.0 - jnp.arange(RET_HEADS, dtype=F32)))
    i = jnp.arange(RET_CHUNK, dtype=F32)
    diff = i[:, None] - i[None, :]
    decay = jnp.where(diff >= 0, jnp.exp(jnp.maximum(diff, 0.0)[None] * lg[:, None, None]), 0.0)
    qd = jnp.exp((i + 1.0)[None] * lg[:, None])
    kd = jnp.exp((chunk_len - 1.0 - i)[None] * lg[:, None])
    widen = lambda t: jnp.repeat(t.T, RET_HEAD_DIM, axis=1)
    cd = jnp.broadcast_to(jnp.exp(chunk_len * lg)[:, None, None], (RET_HEADS, 1, LANES))
    return decay, widen(qd), widen(kd), cd


def _ret_prompt(qr, kr, vr, gr, batch, seq, *, tr):
    nt = seq // tr
    row = pl.BlockSpec((tr, RET_WIDTH), lambda b, i: (b * nt + i, 0))
    decay, qd, kd, cd = _ret_tables(RET_CHUNK)
    return pl.pallas_call(
        _ret_prompt_kernel,
        grid=(batch, nt),
        in_specs=[row, row, row, row,
                  _const_spec((RET_HEADS, RET_CHUNK, RET_CHUNK)),
                  _const_spec((RET_CHUNK, RET_WIDTH)), _const_spec((RET_CHUNK, RET_WIDTH)),
                  _const_spec((RET_HEADS, 1, LANES))],
        out_specs=[row, pl.BlockSpec((1, RET_HEADS, RET_HEAD_DIM, RET_HEAD_DIM), lambda b, i: (b, 0, 0, 0))],
        out_shape=[jax.ShapeDtypeStruct(qr.shape, F32),
                   jax.ShapeDtypeStruct((batch, RET_HEADS, RET_HEAD_DIM, RET_HEAD_DIM), F32)],
        scratch_shapes=[pltpu.VMEM((RET_HEADS, RET_HEAD_DIM, RET_HEAD_DIM), F32)],
        compiler_params=pltpu.CompilerParams(dimension_semantics=("arbitrary", "arbitrary")),
        name="ret_prompt",
    )(qr, kr, vr, gr, decay, qd, kd, cd)


def _ret_sample_kernel(q_ref, k_ref, v_ref, g_ref, st_ref, dec_ref, qd_ref, kd_ref, cd_ref, o_ref, ns_ref):
    tp = q_ref.shape[1]
    zeros = jnp.zeros((RET_CHUNK - tp, LANES), F32)
    pairs = [(s, h) for s in range(q_ref.shape[0]) for h in range(RET_HEADS)]
    cols = lambda h: slice(h * LANES, (h + 1) * LANES)
    scores, values = {}, {}
    for s, h in pairs:
        k = jnp.concatenate([k_ref[s, :, cols(h)], zeros], axis=0)
        vb = jnp.concatenate([v_ref[s, :, cols(h)], zeros], axis=0).astype(BF16)
        scores[s, h] = (_dot_nt(q_ref[s, :, cols(h)].astype(BF16), k.astype(BF16)) * dec_ref[h, :tp, :]).astype(BF16)
        ns_ref[s, h] = cd_ref[h] * st_ref[s, h] + _dot((k * kd_ref[:, cols(h)]).T.astype(BF16), vb)
        values[s, h] = vb
    for s, h in pairs:
        q_dec = (q_ref[s, :, cols(h)] * qd_ref[:tp, cols(h)]).astype(BF16)
        out = _dot(scores[s, h], values[s, h]) + _dot(q_dec, st_ref[s, h].astype(BF16))
        o_ref[s, :, cols(h)] = _head_norm_gate(out, g_ref[s, :, cols(h)])


def _ret_sample(qr, kr, vr, gr, state, step_len, *, ns):
    db, tp, _ = qr.shape
    tok = pl.BlockSpec((ns, tp, RET_WIDTH), lambda i: (i, 0, 0))
    st = pl.BlockSpec((ns, RET_HEADS, RET_HEAD_DIM, RET_HEAD_DIM), lambda i: (i, 0, 0, 0))
    decay, qd, kd, cd = _ret_tables(step_len)
    return pl.pallas_call(
        _ret_sample_kernel,
        grid=(db // ns,),
        in_specs=[tok, tok, tok, tok, st,
                  _const_spec((RET_HEADS, RET_CHUNK, RET_CHUNK)),
                  _const_spec((RET_CHUNK, RET_WIDTH)), _const_spec((RET_CHUNK, RET_WIDTH)),
                  _const_spec((RET_HEADS, 1, LANES))],
        out_specs=[tok, st],
        out_shape=[jax.ShapeDtypeStruct(qr.shape, F32), jax.ShapeDtypeStruct(state.shape, F32)],
        compiler_params=pltpu.CompilerParams(dimension_semantics=("arbitrary",)),
        name="ret_sample",
    )(qr, kr, vr, gr, state, decay, qd, kd, cd)


def _moba_sample_seq(q, kn, vn, k_refs, v_refs, *, page, step_len):
    n_pages = len(k_refs)
    hd = ATTN_HEAD_DIM
    ppb = MOBA_BLOCK // page
    n_full = n_pages // ppb
    rows = step_len * ATTN_HEADS
    qrep = jnp.concatenate([jnp.broadcast_to(q[t:t + 1, :], (ATTN_HEADS, ATTN_WIDTH)) for t in range(step_len)], axis=0)
    r_i = lax.broadcasted_iota(jnp.int32, (rows, ATTN_WIDTH), 0)
    c_i = lax.broadcasted_iota(jnp.int32, (rows, ATTN_WIDTH), 1)
    own_head = (c_i // hd) == (r_i % ATTN_HEADS)
    qbd = jnp.where(own_head, qrep, 0.0)
    qs = qbd * (hd ** -0.5)

    vt = lambda p: v_refs[p][...].reshape(ATTN_WIDTH, page)
    lane = lax.broadcasted_iota(jnp.int32, (rows, LANES), 1)
    gate = jnp.full((rows, LANES), NEG_INF, F32)
    k_pages = []
    for b in range(n_full):
        ksum = None
        for p in range(b * ppb, (b + 1) * ppb):
            kt = k_refs[p][...].reshape(ATTN_WIDTH, page)
            k_pages.append(kt.astype(BF16))
            ksum = kt if ksum is None else ksum + kt
        g = jnp.sum(_dot3(qbd, ksum), axis=-1, keepdims=True) * (1.0 / MOBA_BLOCK)
        gate = jnp.where(lane == b, g, gate)
    s_past = _dot(qs.astype(BF16), jnp.concatenate(k_pages, axis=1))
    sel = jnp.logical_and(_top_k_lanes(gate, lane.astype(F32), min(MOBA_TOP_K, n_full)), lane < n_full)
    bias = jnp.where(sel, 0.0, NEG_INF)
    bias_cols = [jnp.max(jnp.where(lane == b, bias, NEG_INF), axis=-1, keepdims=True) for b in range(n_full)]
    s_past = s_past + jnp.concatenate([jnp.broadcast_to(c, (rows, MOBA_BLOCK)) for c in bias_cols], axis=1)

    tok_of_row = lax.broadcasted_iota(jnp.int32, (rows, 1), 0) // ATTN_HEADS
    s_own = []
    for t in range(step_len):
        s_t = jnp.sum(qs * kn[t:t + 1, :], axis=-1, keepdims=True)
        s_own.append(jnp.where(tok_of_row >= t, s_t, NEG_INF))

    m = jnp.maximum(functools.reduce(jnp.maximum, s_own), jnp.max(s_past, axis=-1, keepdims=True))
    e_past = jnp.exp(s_past - m)
    l = jnp.sum(e_past, axis=-1, keepdims=True)
    o = _dot_nt(e_past.astype(BF16), jnp.concatenate([vt(p).astype(BF16) for p in range(n_pages)], axis=1))
    for t in range(step_len):
        e = jnp.exp(s_own[t] - m)
        l = l + e
        o = o + e * vn[t:t + 1, :]
    o = jnp.where(own_head, o, 0.0) * (1.0 / l)
    return [jnp.sum(o[t * ATTN_HEADS:(t + 1) * ATTN_HEADS, :], axis=0, keepdims=True) for t in range(step_len)]


def _moba_sample_kernel(pt_ref, q_ref, kn_ref, vn_ref, ck_hbm, cv_hbm, o_ref, kbuf, vbuf, sems, *, n_pages, page, step_len):
    i = pl.program_id(0)
    n_seq = q_ref.shape[0]

    def page_copies(step, slot):
        copies = []
        for j in range(n_seq * n_pages):
            phys = pt_ref[step * n_seq + j // n_pages, j % n_pages]
            copies.append(pltpu.make_async_copy(ck_hbm.at[phys], kbuf.at[slot, j], sems.at[0, slot]))
            copies.append(pltpu.make_async_copy(cv_hbm.at[phys], vbuf.at[slot, j], sems.at[1, slot]))
        return copies

    @pl.when(i == 0)
    def _first():
        for c in page_copies(0, 0):
            c.start()

    @pl.when(i + 1 < pl.num_programs(0))
    def _prefetch():
        for c in page_copies(i + 1, (i + 1) % 2):
            c.start()

    slot = i % 2
    for c in page_copies(i, slot):
        c.wait()

    o_ref[...] = jnp.zeros(o_ref.shape, F32)
    for s in range(n_seq):
        k_refs = [kbuf.at[slot, s * n_pages + p] for p in range(n_pages)]
        v_refs = [vbuf.at[slot, s * n_pages + p] for p in range(n_pages)]
        out_rows = _moba_sample_seq(q_ref[s], kn_ref[s], vn_ref[s], k_refs, v_refs, page=page, step_len=step_len)
        for t, row in enumerate(out_rows):
            o_ref[s, t:t + 1, :] = row


def _moba_sample(qa, ka, va, cache_kt, cache_vt, page_table, step_len, *, n_seq):
    db, tp, _ = qa.shape
    n_pages = page_table.shape[1]
    page = cache_kt.shape[-1]
    assert (n_pages * page) % MOBA_BLOCK == 0, "past length must be whole key blocks"
    assert MOBA_BLOCK % page == 0 and db % n_seq == 0
    tok = pl.BlockSpec((n_seq, tp, ATTN_WIDTH), lambda b, pt: (b, 0, 0))
    hbm = pl.BlockSpec(memory_space=pl.ANY)
    page_buf = pltpu.VMEM((2, n_seq * n_pages) + cache_kt.shape[1:], F32)
    grid_spec = pltpu.PrefetchScalarGridSpec(
        num_scalar_prefetch=1,
        grid=(db // n_seq,),
        in_specs=[tok, tok, tok, hbm, hbm],
        out_specs=tok,
        scratch_shapes=[page_buf, page_buf, pltpu.SemaphoreType.DMA((2, 2))],
    )
    return pl.pallas_call(
        functools.partial(_moba_sample_kernel, n_pages=n_pages, page=page, step_len=step_len),
        grid_spec=grid_spec,
        out_shape=jax.ShapeDtypeStruct(qa.shape, F32),
        compiler_params=pltpu.CompilerParams(dimension_semantics=("arbitrary",), vmem_limit_bytes=VMEM_LIMIT),
        name="moba_sample",
    )(page_table, qa, ka, va, cache_kt, cache_vt)


def _rope_tables(pos):
    half = ROPE_DIMS // 2
    inv = ROPE_THETA ** (-jnp.arange(half, dtype=F32) * 2.0 / ROPE_DIMS)
    ang = pos.astype(F32)[:, None] * inv[None, :]
    cos, sin = jnp.cos(ang), jnp.sin(ang)
    j = jnp.arange(LANES) % ATTN_HEAD_DIM
    first, second = j < half, jnp.logical_and(j >= half, j < ROPE_DIMS)
    cj, sj = cos[:, j % half], sin[:, j % half]
    return (jnp.where(jnp.logical_or(first, second), cj, 1.0),
            jnp.where(first, -sj, 0.0),
            jnp.where(second, sj, 0.0))


def _ret_rot_tables(pos):
    inv = 1.0 / (RET_ROT_BASE ** jnp.linspace(0.0, 1.0, RET_HEAD_DIM // 2, dtype=F32))
    ang = pos.astype(F32)[:, None] * inv[None, :]
    cos, sin = jnp.cos(ang), jnp.sin(ang)
    lane = jnp.arange(LANES)
    cl, sl = cos[:, lane // 2], sin[:, lane // 2]
    even = lane % 2 == 0
    return cl, jnp.where(even, -sl, 0.0), jnp.where(even, 0.0, sl)


def _layer_weights(l, norm_ffn1_w, ffn1_w_gate, ffn1_w_up, ffn1_w_down, norm_mix_w, w_in, q_norm_w, k_norm_w, w_out,
                   norm_ffn2_w, ffn2_w_gate, ffn2_w_up, ffn2_w_down):
    head = jnp.arange(MXU_TILE) // ATTN_HEAD_DIM
    return {
        "n1": norm_ffn1_w[l][None], "nm": norm_mix_w[l][None], "n2": norm_ffn2_w[l][None],
        "wg1": ffn1_w_gate[l].astype(BF16), "wu1": ffn1_w_up[l].astype(BF16), "wd1": ffn1_w_down[l].astype(BF16),
        "wg2": ffn2_w_gate[l].astype(BF16), "wu2": ffn2_w_up[l].astype(BF16), "wd2": ffn2_w_down[l].astype(BF16),
        "win": w_in[l].astype(BF16), "wo": w_out[l].astype(BF16),
        "qn": jnp.tile(q_norm_w[l], ATTN_HEADS)[None], "kn": jnp.tile(k_norm_w[l], ATTN_HEADS)[None],
        "bd": jnp.where(head[:, None] == head[None, :], 1.0 / ATTN_HEAD_DIM, 0.0).astype(BF16),
    }


def kernel(x_prompt, x_sample, cache_k, cache_v, state_ret, page_table, c_prompt, c_sample, w_ada, b_ada, norm_ffn1_w, ffn1_w_gate, ffn1_w_up, ffn1_w_down, norm_mix_w, w_in, q_norm_w, k_norm_w, w_out, norm_ffn2_w, ffn2_w_gate, ffn2_w_up, ffn2_w_down):
    batch, seq, _ = x_prompt.shape
    db, step_len, _ = x_sample.shape
    depth = w_ada.shape[0]
    n_pages, page = page_table.shape[1], cache_k.shape[2]
    past_len = n_pages * page
    tm_p = 512
    tp = SUBLANES
    assert seq % tm_p == 0 and step_len <= tp and db % SUBLANES == 0

    tabs_p = [t.reshape(seq // tm_p, tm_p, LANES)
              for t in _rope_tables(jnp.arange(seq)) + _ret_rot_tables(jnp.arange(seq))]
    pos_s = past_len + jnp.arange(step_len)
    tabs_s = [jnp.repeat(t, db, axis=0)[None] for t in _rope_tables(pos_s) + _ret_rot_tables(pos_s)]

    hp = x_prompt.reshape(batch * seq, D_MODEL)
    hs = x_sample.transpose(1, 0, 2).reshape(step_len * db, D_MODEL)
    outs = [[] for _ in range(6)]
    for l in range(depth):
        lw = _layer_weights(l, norm_ffn1_w, ffn1_w_gate, ffn1_w_up, ffn1_w_down, norm_mix_w, w_in, q_norm_w, k_norm_w,
                            w_out, norm_ffn2_w, ffn2_w_gate, ffn2_w_up, ffn2_w_down)
        mods = _mods(jnp.concatenate([c_prompt, c_sample], axis=0), w_ada[l], b_ada[l])
        mods_p = mods[:batch].reshape(batch, N_MODS, 1, D_MODEL)
        mods_s = mods[batch:].reshape(db, N_MODS, D_MODEL).transpose(1, 0, 2)[None]

        tiles_per_seq = seq // tm_p
        h1, qa, kt, vt, qr, kr, vr, gr = _ffn_proj(
            hp, mods_p, tabs_p, lw, tm=tm_p, mod_idx=lambda i: i // tiles_per_seq, tab_idx=lambda i: i % tiles_per_seq,
            kv_groups=batch, kv_idx=lambda i: (i // tiles_per_seq, 0, i % tiles_per_seq))
        q_norm2_max = jnp.max(jnp.square(q_norm_w[l])) * (LOG2_E * LOG2_E)
        oa = _moba_prompt(qa, kt, vt, q_norm2_max, batch, seq)
        o_ret, st_p = _ret_prompt(qr, kr, vr, gr, batch, seq, tr=512)
        hp = _out_ffn(h1, oa, o_ret, mods_p, lw, tm=tm_p, mod_idx=lambda i: i // tiles_per_seq)

        h1s, qas, kts, vts, qrs, krs, vrs, grs = _ffn_proj(
            hs, mods_s, tabs_s, lw, tm=step_len * db, mod_idx=lambda i: 0, tab_idx=lambda i: 0,
            kv_groups=step_len, kv_idx=lambda i: (0, 0, 0), kv_tiles=step_len)

        def seq_major(t):
            t = t.reshape(step_len, db, ATTN_WIDTH).transpose(1, 0, 2)
            return jnp.pad(t, ((0, 0), (0, tp - step_len), (0, 0)))

        def seq_major_t(t):
            return jnp.pad(t.transpose(2, 0, 1), ((0, 0), (0, tp - step_len), (0, 0)))

        def token_major(t):
            return t[:, :step_len].transpose(1, 0, 2).reshape(step_len * db, ATTN_WIDTH)

        cache_kt = cache_k[l].transpose(0, 2, 3, 1)
        cache_vt = cache_v[l].transpose(0, 2, 3, 1)
        oas = _moba_sample(seq_major(qas), seq_major_t(kts), seq_major_t(vts), cache_kt, cache_vt, page_table, step_len,
                           n_seq=2)
        o_rets, st_s = _ret_sample(seq_major(qrs), seq_major(krs), seq_major(vrs), seq_major(grs), state_ret[l],
                                   step_len, ns=SUBLANES)
        hs = _out_ffn(h1s, token_major(oas), token_major(o_rets), mods_s, lw, tm=step_len * db, mod_idx=lambda i: 0)

        heads = (ATTN_HEADS, ATTN_HEAD_DIM)
        outs[0].append(kt.reshape(batch, *heads, seq).transpose(0, 3, 1, 2))
        outs[1].append(vt.reshape(batch, *heads, seq).transpose(0, 3, 1, 2))
        outs[2].append(kts.reshape(step_len, *heads, db).transpose(3, 0, 1, 2))
        outs[3].append(vts.reshape(step_len, *heads, db).transpose(3, 0, 1, 2))
        outs[4].append(st_p)
        outs[5].append(st_s)

    y_prompt = hp.reshape(batch, seq, D_MODEL)
    y_sample = hs.reshape(step_len, db, D_MODEL).transpose(1, 0, 2)
    return (y_prompt, y_sample) + tuple(jnp.stack(o) for o in outs)
```

```python
import functools

import jax
import jax.numpy as jnp
from jax import lax
from jax.experimental import pallas as pl
from jax.experimental.pallas import tpu as pltpu

F32 = jnp.float32
BF16 = jnp.bfloat16

D_MODEL = 1024
ATTN_HEADS = 8
ATTN_HEAD_DIM = 64
ATTN_WIDTH = ATTN_HEADS * ATTN_HEAD_DIM
RET_HEADS = 4
RET_HEAD_DIM = 128
RET_WIDTH = RET_HEADS * RET_HEAD_DIM
IN_WIDTH = 3 * ATTN_WIDTH + 4 * RET_WIDTH
MOBA_BLOCK = 256
MOBA_TOP_K = 3
ROPE_THETA = 500000.0
ROPE_DIMS = ATTN_HEAD_DIM // 4
RET_CHUNK = 128
RET_ROT_BASE = 10000.0
N_MODS = 9
EPS = 1e-6
NEG_INF = -1e30

LANES = 128
SUBLANES = 8
VMEM_LIMIT = 56 * 1024 * 1024
MXU_TILE = 256
FF_CHUNK = MXU_TILE
MOBA_GROUP = 4
LOG2_E = 1.4426950408889634
SHIFT_LANE = 63
SHIFT_LIMIT = 60.0


def _dot(a, b):
    return jnp.dot(a, b, preferred_element_type=F32)


def _dot_nt(a, b):
    return lax.dot_general(a, b, (((1,), (1,)), ((), ())), preferred_element_type=F32)


def _split(a):
    hi = a.astype(BF16)
    lo = (a - hi.astype(F32)).astype(BF16)
    return hi, lo


def _dot3(a, b):
    ah, al = _split(a)
    bh, bl = _split(b)
    return _dot(ah, bh) + (_dot(ah, bl) + _dot(al, bh))


def _dot3_nt(a, b):
    ah, al = _split(a)
    bh, bl = _split(b)
    return _dot_nt(ah, bh) + (_dot_nt(ah, bl) + _dot_nt(al, bh))


def _sigmoid(x):
    return 1.0 / (1.0 + jnp.exp(-x))


def _silu(x):
    return x * _sigmoid(x)


def _rms(x):
    return x * lax.rsqrt(jnp.mean(x * x, axis=-1, keepdims=True) + EPS)


def _top_k_lanes(g, lane_f, k):
    sel = jnp.zeros(g.shape, jnp.bool_)
    for _ in range(k):
        m = jnp.max(g, axis=-1, keepdims=True)
        idx = jnp.min(jnp.where(g == m, lane_f, 1e9), axis=-1, keepdims=True)
        pick = lane_f == idx
        sel = jnp.logical_or(sel, pick)
        g = jnp.where(pick, -jnp.inf, g)
    return sel


def _fold_rows(op, s):
    while s.shape[0] > SUBLANES:
        half = s.shape[0] // 2
        s = op(s[:half], s[half:])
    return s


def _col_max(s):
    return jnp.max(_fold_rows(jnp.maximum, s), axis=0, keepdims=True)


def _col_min(s):
    return jnp.min(_fold_rows(jnp.minimum, s), axis=0, keepdims=True)


def _top_k_rows(g, row_f, k):
    sel = jnp.zeros(g.shape, jnp.bool_)
    for _ in range(k):
        m = _col_max(g)
        idx = _col_min(jnp.where(g == m, row_f, 1e9))
        pick = row_f == idx
        sel = jnp.logical_or(sel, pick)
        g = jnp.where(pick, -jnp.inf, g)
    return sel


def _mods_kernel(c_ref, w_ref, b_ref, o_ref):
    s = _silu(c_ref[...]).astype(BF16)
    o_ref[...] = _dot(s, w_ref[...].astype(BF16)) + b_ref[...]


def _mods(c, w_ada, b_ada):
    n = c.shape[0]
    n_pad = -(-n // SUBLANES) * SUBLANES
    c = jnp.pad(c, ((0, n_pad - n), (0, 0)))
    width = w_ada.shape[1]
    tn = 9 * LANES
    out = pl.pallas_call(
        _mods_kernel,
        grid=(width // tn,),
        in_specs=[pl.BlockSpec((n_pad, D_MODEL), lambda j: (0, 0)),
                  pl.BlockSpec((D_MODEL, tn), lambda j: (0, j)),
                  pl.BlockSpec((1, tn), lambda j: (0, j))],
        out_specs=pl.BlockSpec((n_pad, tn), lambda j: (0, j)),
        out_shape=jax.ShapeDtypeStruct((n_pad, width), F32),
        name="mods",
    )(c, w_ada, b_ada.reshape(1, width))
    return out[:n]


def _swiglu_acc(xb, wg_ref, wu_ref, wd_ref):
    d_ff = wg_ref.shape[1]
    acc = None
    for c in range(d_ff // FF_CHUNK):
        sl = slice(c * FF_CHUNK, (c + 1) * FF_CHUNK)
        g = _dot(xb, wg_ref[:, sl])
        u = _dot(xb, wu_ref[:, sl])
        part = _dot((_silu(g) * u).astype(BF16), wd_ref[sl, :])
        acc = part if acc is None else acc + part
    return acc


def _mod_rows(mod_ref, i, rows):
    m = mod_ref[0, i]
    reps = rows // m.shape[0]
    return m if m.shape[0] == 1 or reps == 1 else jnp.concatenate([m] * reps, axis=0)


def _const_spec(shape):
    nd = len(shape)
    return pl.BlockSpec(shape, lambda *_: (0,) * nd, pipeline_mode=pl.Buffered(1))


def _head_norm_gate(out, g):
    on = out * lax.rsqrt(jnp.mean(out * out, axis=-1, keepdims=True) + EPS)
    return on * _silu(g)


def _ret_chunks(q_ref, k_ref, v_ref, g_ref, dec_ref, qd_ref, kd_ref, cd_ref, o_ref, s_ref, fresh):
    c_len = RET_CHUNK
    n_chunks = q_ref.shape[0] // c_len
    pairs = [(c, h) for c in range(n_chunks) for h in range(RET_HEADS)]
    rows = lambda c: slice(c * c_len, (c + 1) * c_len)
    cols = lambda h: slice(h * LANES, (h + 1) * LANES)
    scores, updates = {}, {}
    for c, h in pairs:
        q, k, vb = q_ref[rows(c), cols(h)], k_ref[rows(c), cols(h)], v_ref[rows(c), cols(h)].astype(BF16)
        scores[c, h] = (_dot_nt(q.astype(BF16), k.astype(BF16)) * dec_ref[h]).astype(BF16)
        updates[c, h] = _dot((k * kd_ref[:, cols(h)]).T.astype(BF16), vb)
    states = {}
    for h in range(RET_HEADS):
        st = jnp.where(fresh, 0.0, s_ref[h])
        for c in range(n_chunks):
            states[c, h] = st
            st = cd_ref[h] * st + updates[c, h]
        s_ref[h] = st
    for c, h in pairs:
        q, vb = q_ref[rows(c), cols(h)], v_ref[rows(c), cols(h)].astype(BF16)
        out = _dot(scores[c, h], vb) + _dot((q * qd_ref[:, cols(h)]).astype(BF16), states[c, h].astype(BF16))
        o_ref[rows(c), cols(h)] = _head_norm_gate(out, g_ref[rows(c), cols(h)])


def _ret_tables(chunk_len):
    lg = jnp.log(1.0 - 2.0 ** (-5.0 - jnp.arange(RET_HEADS, dtype=F32)))
    i = jnp.arange(RET_CHUNK, dtype=F32)
    diff = i[:, None] - i[None, :]
    decay = jnp.where(diff >= 0, jnp.exp(jnp.maximum(diff, 0.0)[None] * lg[:, None, None]), 0.0)
    qd = jnp.exp((i + 1.0)[None] * lg[:, None])
    kd = jnp.exp((chunk_len - 1.0 - i)[None] * lg[:, None])
    widen = lambda t: jnp.repeat(t.T, RET_HEAD_DIM, axis=1)
    cd = jnp.broadcast_to(jnp.exp(chunk_len * lg)[:, None, None], (RET_HEADS, 1, LANES))
    return decay, widen(qd), widen(kd), cd


def _ffn_proj_kernel(x_ref, mod_ref, n1_ref, wg_ref, wu_ref, wd_ref, nm_ref, win_ref, qn_ref, kn_ref, bd_ref,
                     ca_ref, sa1_ref, sa2_ref, cr_ref, sr1_ref, sr2_ref, *refs, tiles_per_seq):
    if tiles_per_seq is None:
        h_ref, qa_ref, ka_ref, va_ref, qr_ref, kr_ref, vr_ref, gr_ref = refs
    else:
        ret_tabs, (h_ref, qa_ref, ka_ref, va_ref, oret_ref, st_ref), (qr_ref, kr_ref, vr_ref, gr_ref, s_ref) = (
            refs[:4], refs[4:10], refs[10:])

        @pl.when(pl.program_id(0) == 0)
        def _init_state():
            s_ref[...] = jnp.zeros(s_ref.shape, F32)
    x = x_ref[...]
    mod = lambda i: _mod_rows(mod_ref, i, x.shape[0])
    xn = (_rms(x) * n1_ref[...]) * (1.0 + mod(1)) + mod(0)
    acc = _swiglu_acc(xn.astype(BF16), wg_ref, wu_ref, wd_ref)
    h = x + 0.5 * mod(2) * acc
    h_ref[...] = h
    hn = (_rms(h) * nm_ref[...]) * (1.0 + mod(4)) + mod(3)
    hb = hn.astype(BF16)

    def seg(i):
        return _dot(hb, win_ref[:, i * ATTN_WIDTH:(i + 1) * ATTN_WIDTH])

    ca, sa1, sa2 = ca_ref[0], sa1_ref[0], sa2_ref[0]
    cr, sr1, sr2 = cr_ref[0], sr1_ref[0], sr2_ref[0]
    bd = bd_ref[...]

    def store_t(o_ref, g, r):
        width = o_ref.shape[2]
        for t in range(o_ref.shape[0]):
            o_ref[t, g * LANES:(g + 1) * LANES, :] = r[t * width:(t + 1) * width, :].T

    def attn_head_norm_rope(p, w, o_ref, transposed):
        hi, lo = _split(p * p)
        ms = jnp.concatenate([_dot(hi[:, c:c + MXU_TILE], bd) + _dot(lo[:, c:c + MXU_TILE], bd)
                              for c in range(0, ATTN_WIDTH, MXU_TILE)], axis=1)
        pn = (p * lax.rsqrt(ms + EPS)) * w
        for g in range(ATTN_WIDTH // LANES):
            xg = pn[:, g * LANES:(g + 1) * LANES]
            r = xg * ca + pltpu.roll(xg, LANES - ROPE_DIMS // 2, 1) * sa1 + pltpu.roll(xg, ROPE_DIMS // 2, 1) * sa2
            if transposed:
                store_t(o_ref, g, r)
            else:
                o_ref[:, g * LANES:(g + 1) * LANES] = r

    def ret_rotate(p, o_ref, scale):
        for g in range(RET_HEADS):
            xg = p[:, g * LANES:(g + 1) * LANES]
            r = xg * cr + pltpu.roll(xg, LANES - 1, 1) * sr1 + pltpu.roll(xg, 1, 1) * sr2
            o_ref[:, g * LANES:(g + 1) * LANES] = r if scale is None else r * scale

    attn_head_norm_rope(seg(0), qn_ref[...], qa_ref, False)
    attn_head_norm_rope(seg(1), kn_ref[...], ka_ref, True)
    va = seg(2)
    for g in range(ATTN_WIDTH // LANES):
        store_t(va_ref, g, va[:, g * LANES:(g + 1) * LANES])
    ret_rotate(seg(3), qr_ref, None)
    ret_rotate(seg(4), kr_ref, RET_HEAD_DIM ** -0.5)
    vr_ref[...] = seg(5)
    gr_ref[...] = seg(6)
    if tiles_per_seq is not None:
        _ret_chunks(qr_ref, kr_ref, vr_ref, gr_ref, *ret_tabs, oret_ref, s_ref, pl.program_id(0) % tiles_per_seq == 0)
        st_ref[0] = s_ref[...]


def _ffn_proj(x2d, mods, tabs, lw, *, tm, mod_idx, tab_idx, kv_groups, kv_idx, kv_tiles=1, ret_seqs=None):
    n = x2d.shape[0]
    rm = mods.shape[2]
    rt = tabs[0].shape[1]
    d_ff = lw["wg1"].shape[1]
    row = lambda i: (i, 0)
    tab_spec = pl.BlockSpec((1, rt, LANES), lambda i: (tab_idx(i), 0, 0))
    in_specs = [
        pl.BlockSpec((tm, D_MODEL), row),
        pl.BlockSpec((1, N_MODS, rm, D_MODEL), lambda i: (mod_idx(i), 0, 0, 0)),
        _const_spec((1, D_MODEL)),
        _const_spec((D_MODEL, d_ff)), _const_spec((D_MODEL, d_ff)), _const_spec((d_ff, D_MODEL)),
        _const_spec((1, D_MODEL)),
        _const_spec((D_MODEL, IN_WIDTH)),
        _const_spec((1, ATTN_WIDTH)), _const_spec((1, ATTN_WIDTH)),
        _const_spec((MXU_TILE, MXU_TILE)),
    ] + [tab_spec] * 6
    half = jax.ShapeDtypeStruct((n, ATTN_WIDTH), F32)
    half_t = jax.ShapeDtypeStruct((kv_groups, ATTN_WIDTH, n // kv_groups), F32)
    half_spec = pl.BlockSpec((tm, ATTN_WIDTH), row)
    half_t_spec = pl.BlockSpec((kv_tiles, ATTN_WIDTH, tm // kv_tiles), lambda i: kv_idx(i))
    out_shape = [jax.ShapeDtypeStruct((n, D_MODEL), F32), half, half_t, half_t]
    out_specs = [pl.BlockSpec((tm, D_MODEL), row), half_spec, half_t_spec, half_t_spec]
    operands = [x2d, mods, lw["n1"], lw["wg1"], lw["wu1"], lw["wd1"], lw["nm"], lw["win"], lw["qn"], lw["kn"], lw["bd"], *tabs]
    if ret_seqs is None:
        tiles_per_seq, scratch = None, []
        out_shape += [half] * 4
        out_specs += [half_spec] * 4
    else:
        tiles_per_seq = n // ret_seqs // tm
        state = (RET_HEADS, RET_HEAD_DIM, RET_HEAD_DIM)
        operands += list(_ret_tables(RET_CHUNK))
        in_specs += [_const_spec((RET_HEADS, RET_CHUNK, RET_CHUNK)), _const_spec((RET_CHUNK, RET_WIDTH)),
                     _const_spec((RET_CHUNK, RET_WIDTH)), _const_spec((RET_HEADS, 1, LANES))]
        out_shape += [half, jax.ShapeDtypeStruct((ret_seqs,) + state, F32)]
        out_specs += [half_spec, pl.BlockSpec((1,) + state, lambda i: (i // tiles_per_seq, 0, 0, 0))]
        scratch = [pltpu.VMEM((tm, RET_WIDTH), F32)] * 4 + [pltpu.VMEM(state, F32)]
    return pl.pallas_call(
        functools.partial(_ffn_proj_kernel, tiles_per_seq=tiles_per_seq),
        grid=(n // tm,),
        in_specs=in_specs,
        out_specs=out_specs,
        out_shape=out_shape,
        scratch_shapes=scratch,
        compiler_params=pltpu.CompilerParams(dimension_semantics=("arbitrary",), vmem_limit_bytes=VMEM_LIMIT),
        name="ffn_proj",
    )(*operands)


def _out_ffn_kernel(h_ref, oa_ref, or_ref, mod_ref, wo_ref, n2_ref, wg_ref, wu_ref, wd_ref, y_ref):
    mix = _dot(oa_ref[...].astype(BF16), wo_ref[:ATTN_WIDTH, :]) + _dot(or_ref[...].astype(BF16), wo_ref[ATTN_WIDTH:, :])
    mod = lambda i: _mod_rows(mod_ref, i, h_ref.shape[0])
    h = h_ref[...] + mod(5) * mix
    hn = (_rms(h) * n2_ref[...]) * (1.0 + mod(7)) + mod(6)
    acc = _swiglu_acc(hn.astype(BF16), wg_ref, wu_ref, wd_ref)
    y_ref[...] = h + 0.5 * mod(8) * acc


def _out_ffn(h2d, oa, o_ret, mods, lw, *, tm, mod_idx):
    n = h2d.shape[0]
    rm = mods.shape[2]
    d_ff = lw["wg2"].shape[1]
    row = lambda i: (i, 0)
    return pl.pallas_call(
        _out_ffn_kernel,
        grid=(n // tm,),
        in_specs=[
            pl.BlockSpec((tm, D_MODEL), row),
            pl.BlockSpec((tm, ATTN_WIDTH), row),
            pl.BlockSpec((tm, RET_WIDTH), row),
            pl.BlockSpec((1, N_MODS, rm, D_MODEL), lambda i: (mod_idx(i), 0, 0, 0)),
            _const_spec((D_MODEL, D_MODEL)),
            _const_spec((1, D_MODEL)),
            _const_spec((D_MODEL, d_ff)), _const_spec((D_MODEL, d_ff)), _const_spec((d_ff, D_MODEL)),
        ],
        out_specs=pl.BlockSpec((tm, D_MODEL), row),
        out_shape=jax.ShapeDtypeStruct((n, D_MODEL), F32),
        compiler_params=pltpu.CompilerParams(dimension_semantics=("arbitrary",), vmem_limit_bytes=VMEM_LIMIT),
        name="out_ffn",
    )(h2d, oa, o_ret, mods, lw["wo"], lw["n2"], lw["wg2"], lw["wu2"], lw["wd2"])


def _moba_prompt_kernel(q_ref, qn_ref, kt_ref, vt_in_ref, qn2_ref, o_ref,
                        kaug_ref, vt_ref, km_ref, bound_ref, flag_ref, acc_ref, shift_ref, bias_ref, *, nb):
    qi = pl.program_id(2)
    tq = MOBA_BLOCK
    hd = ATTN_HEAD_DIM
    grp = MOBA_GROUP
    lane = lax.broadcasted_iota(jnp.int32, (tq, LANES), 1)
    in_head = (lane < hd, lane >= hd)
    field_off = (hd, 0)

    @pl.when(qi == 0)
    def _prepare():
        km_ref[...] = jnp.zeros(km_ref.shape, F32)
        feat = lax.broadcasted_iota(jnp.int32, (LANES, tq), 0)
        feat_in_head = (feat < hd, feat >= hd)
        kn2 = [jnp.zeros((1, 1), F32)] * 2
        for j in range(nb):
            kb = kt_ref[0, :, j * tq:(j + 1) * tq].T
            vtb = vt_in_ref[0, :, j * tq:(j + 1) * tq]
            km_ref[hd + j:hd + j + 1, :] = jnp.mean(kb, axis=0, keepdims=True)
            sq = kb * kb
            for h in range(2):
                norm2 = jnp.sum(jnp.where(in_head[h], sq, 0.0), axis=1, keepdims=True)
                kn2[h] = jnp.maximum(kn2[h], jnp.max(norm2, axis=0, keepdims=True))
                off = field_off[h]
                field = jnp.where(jnp.logical_or(lane == off + j, lane == off + SHIFT_LANE), 1.0, 0.0)
                kaug_ref[h, j * tq:(j + 1) * tq, :] = jnp.where(in_head[h], kb, field).astype(BF16)
                vt_ref[h, j] = jnp.where(feat_in_head[h], vtb, 1.0).astype(BF16)
        bound = [jnp.sqrt(qn2_ref[0:1, 0:1] * kn2[h]) for h in range(2)]
        for h in range(2):
            bound_ref[h] = jnp.broadcast_to(bound[h], bound_ref.shape[1:])
        flag_ref[0] = (jnp.max(jnp.maximum(bound[0], bound[1])) <= SHIFT_LIMIT).astype(jnp.int32)

        bias_ref[...] = jnp.full(bias_ref.shape, NEG_INF, F32)

    q = q_ref[...]
    scale = hd ** -0.5 * LOG2_E
    qs = [jnp.where(in_head[h], q, 0.0) * scale for h in range(2)]

    field_row = lax.broadcasted_iota(jnp.int32, (LANES, tq), 0)
    field_row_f = field_row.astype(F32)

    def block_bias(q_tile, h, n_past):
        off = field_off[h]
        q_head = jnp.where(in_head[h], q_tile, 0.0)
        gate_t = _dot3_nt(km_ref[hd - off:hd - off + LANES, :], q_head)
        valid = jnp.logical_and(field_row >= off, field_row < off + n_past)
        sel = _top_k_rows(jnp.where(valid, gate_t, NEG_INF), field_row_f, MOBA_TOP_K)
        return jnp.where(jnp.logical_and(sel, valid), 0.0, NEG_INF).T

    key_i = lax.broadcasted_iota(jnp.int32, (tq, tq), 0)
    qry_i = lax.broadcasted_iota(jnp.int32, (tq, tq), 1)
    causal = key_i <= qry_i
    own_rows = pl.ds(pl.multiple_of(qi * tq, tq), tq)
    n_groups = (qi + grp - 1) // grp

    def group_rows(g):
        return pl.ds(pl.multiple_of(g * (grp * tq), grp * tq), grp * tq)

    bounded = flag_ref[0] == 1

    @pl.when(bounded)
    def _bound_shift():
        for h in range(2):
            shift_ref[h] = jnp.broadcast_to(bound_ref[h, 0:1, :], (tq, LANES))

    @pl.when(jnp.logical_not(bounded))
    def _exact_shift():
        for h in range(2):
            q_sel = jnp.where(in_head[h], qs[h], bias_ref[h]).astype(BF16)
            m = _col_max(jnp.where(causal, _dot_nt(kaug_ref[h, own_rows, :], qs[h].astype(BF16)), NEG_INF))
            m = lax.fori_loop(
                0, n_groups, lambda g, m: jnp.maximum(m, _col_max(_dot_nt(kaug_ref[h, group_rows(g), :], q_sel))), m)
            shift_ref[h] = jnp.broadcast_to(m, (LANES, tq)).T

    q_own, q_past = [], []
    for h in range(2):
        at_shift = lane == field_off[h] + SHIFT_LANE
        q_own.append(jnp.where(in_head[h], qs[h], jnp.where(at_shift, -shift_ref[h], 0.0)))
        q_past.append(jnp.where(jnp.logical_or(in_head[h], at_shift), q_own[h], bias_ref[h]).astype(BF16))
    acc_ref[...] = jnp.zeros(acc_ref.shape, F32)

    def add_groups(groups, with_own=False):
        scores = [[_dot_nt(kaug_ref[h, group_rows(g), :], q_past[h]) for h in range(2)] for g in groups]
        if with_own:
            own_scores = [_dot_nt(kaug_ref[h, own_rows, :], q_own[h].astype(BF16)) for h in range(2)]
            q_next = qn_ref[...]
            for h in range(2):
                bias_ref[h] = block_bias(q_next, h, qi + 1)
            for h in range(2):
                s = jnp.where(causal, own_scores[h], NEG_INF)
                acc_ref[h] += _dot(vt_ref[h, qi], jnp.exp2(s).astype(BF16))
        for g, sc in zip(groups, scores):
            for h in range(2):
                p = jnp.exp2(sc[h]).astype(BF16)
                pv = None
                for c in range(grp):
                    part = _dot(vt_ref[h, g * grp + c], p[c * tq:(c + 1) * tq, :])
                    pv = part if pv is None else pv + part
                acc_ref[h] += pv

    def pair_body(t, carry):
        add_groups([2 * t, 2 * t + 1])
        return carry

    lax.fori_loop(0, n_groups // 2, pair_body, 0)

    @pl.when(n_groups % 2 == 1)
    def _odd_tail():
        add_groups([n_groups - 1], with_own=True)

    @pl.when(n_groups % 2 == 0)
    def _even_tail():
        add_groups([], with_own=True)

    a0, a1 = acc_ref[0], acc_ref[1]
    o_t = jnp.concatenate([a0[:hd] * (1.0 / a0[hd:hd + 1]), a1[hd:] * (1.0 / a1[0:1])], axis=0)
    o_ref[...] = o_t.T


def _moba_prompt(qa, kt, vt, q_norm2_max, batch, seq):
    tq = MOBA_BLOCK
    assert seq % (tq * MOBA_GROUP) == 0
    nb = seq // tq
    assert nb <= SHIFT_LANE, "side field holds one lane per key block below the shift lane"
    n_pairs = ATTN_WIDTH // LANES
    kv_spec = pl.BlockSpec((1, LANES, seq), lambda b, hp, qi: (b, hp, 0))
    q_spec = pl.BlockSpec((tq, LANES), lambda b, hp, qi: (b * nb + qi, hp))
    q_next_spec = pl.BlockSpec((tq, LANES), lambda b, hp, qi: (b * nb + jnp.minimum(qi + 1, nb - 1), hp))
    return pl.pallas_call(
        functools.partial(_moba_prompt_kernel, nb=nb),
        grid=(batch, n_pairs, nb),
        in_specs=[q_spec, q_next_spec, kv_spec, kv_spec, _const_spec((1, LANES))],
        out_specs=q_spec,
        out_shape=jax.ShapeDtypeStruct(qa.shape, F32),
        scratch_shapes=[
            pltpu.VMEM((2, seq, LANES), BF16),
            pltpu.VMEM((2, nb, LANES, tq), BF16),
            pltpu.VMEM((ATTN_HEAD_DIM + LANES, LANES), F32),
            pltpu.VMEM((2, SUBLANES, LANES), F32),
            pltpu.SMEM((1,), jnp.int32),
            pltpu.VMEM((2, LANES, tq), F32),
            pltpu.VMEM((2, tq, LANES), F32),
            pltpu.VMEM((2, tq, LANES), F32),
        ],
        compiler_params=pltpu.CompilerParams(dimension_semantics=("arbitrary", "arbitrary", "arbitrary"),
                                             vmem_limit_bytes=VMEM_LIMIT),
        name="moba_prompt",
    )(qa, qa, kt, vt, jnp.broadcast_to(q_norm2_max, (1, LANES)).astype(F32))


def _ret_sample_kernel(q_ref, k_ref, v_ref, g_ref, st_ref, dec_ref, qd_ref, kd_ref, cd_ref, o_ref, ns_ref):
    tp = q_ref.shape[1]
    zeros = jnp.zeros((RET_CHUNK - tp, LANES), F32)
    pairs = [(s, h) for s in range(q_ref.shape[0]) for h in range(RET_HEADS)]
    cols = lambda h: slice(h * LANES, (h + 1) * LANES)
    scores, values = {}, {}
    for s, h in pairs:
        k = jnp.concatenate([k_ref[s, :, cols(h)], zeros], axis=0)
        vb = jnp.concatenate([v_ref[s, :, cols(h)], zeros], axis=0).astype(BF16)
        scores[s, h] = (_dot_nt(q_ref[s, :, cols(h)].astype(BF16), k.astype(BF16)) * dec_ref[h, :tp, :]).astype(BF16)
        ns_ref[s, h] = cd_ref[h] * st_ref[s, h] + _dot((k * kd_ref[:, cols(h)]).T.astype(BF16), vb)
        values[s, h] = vb
    for s, h in pairs:
        q_dec = (q_ref[s, :, cols(h)] * qd_ref[:tp, cols(h)]).astype(BF16)
        out = _dot(scores[s, h], values[s, h]) + _dot(q_dec, st_ref[s, h].astype(BF16))
        o_ref[s, :, cols(h)] = _head_norm_gate(out, g_ref[s, :, cols(h)])


def _ret_sample(qr, kr, vr, gr, state, step_len, *, ns):
    db, tp, _ = qr.shape
    tok = pl.BlockSpec((ns, tp, RET_WIDTH), lambda i: (i, 0, 0))
    st = pl.BlockSpec((ns, RET_HEADS, RET_HEAD_DIM, RET_HEAD_DIM), lambda i: (i, 0, 0, 0))
    decay, qd, kd, cd = _ret_tables(step_len)
    return pl.pallas_call(
        _ret_sample_kernel,
        grid=(db // ns,),
        in_specs=[tok, tok, tok, tok, st,
                  _const_spec((RET_HEADS, RET_CHUNK, RET_CHUNK)),
                  _const_spec((RET_CHUNK, RET_WIDTH)), _const_spec((RET_CHUNK, RET_WIDTH)),
                  _const_spec((RET_HEADS, 1, LANES))],
        out_specs=[tok, st],
        out_shape=[jax.ShapeDtypeStruct(qr.shape, F32), jax.ShapeDtypeStruct(state.shape, F32)],
        compiler_params=pltpu.CompilerParams(dimension_semantics=("arbitrary",)),
        name="ret_sample",
    )(qr, kr, vr, gr, state, decay, qd, kd, cd)


def _moba_sample_seq(q, kn, vn, k_refs, v_refs, *, page, step_len):
    n_pages = len(k_refs)
    hd = ATTN_HEAD_DIM
    ppb = MOBA_BLOCK // page
    n_full = n_pages // ppb
    rows = step_len * ATTN_HEADS
    qrep = jnp.concatenate([jnp.broadcast_to(q[t:t + 1, :], (ATTN_HEADS, ATTN_WIDTH)) for t in range(step_len)], axis=0)
    r_i = lax.broadcasted_iota(jnp.int32, (rows, ATTN_WIDTH), 0)
    c_i = lax.broadcasted_iota(jnp.int32, (rows, ATTN_WIDTH), 1)
    own_head = (c_i // hd) == (r_i % ATTN_HEADS)
    qbd = jnp.where(own_head, qrep, 0.0)
    qs = qbd * (hd ** -0.5)

    vt = lambda p: v_refs[p][...].reshape(ATTN_WIDTH, page)
    lane = lax.broadcasted_iota(jnp.int32, (rows, LANES), 1)
    gate = jnp.full((rows, LANES), NEG_INF, F32)
    k_pages = []
    for b in range(n_full):
        ksum = None
        for p in range(b * ppb, (b + 1) * ppb):
            kt = k_refs[p][...].reshape(ATTN_WIDTH, page)
            k_pages.append(kt.astype(BF16))
            ksum = kt if ksum is None else ksum + kt
        g = jnp.sum(_dot3(qbd, ksum), axis=-1, keepdims=True) * (1.0 / MOBA_BLOCK)
        gate = jnp.where(lane == b, g, gate)
    s_past = _dot(qs.astype(BF16), jnp.concatenate(k_pages, axis=1))
    sel = jnp.logical_and(_top_k_lanes(gate, lane.astype(F32), min(MOBA_TOP_K, n_full)), lane < n_full)
    bias = jnp.where(sel, 0.0, NEG_INF)
    bias_cols = [jnp.max(jnp.where(lane == b, bias, NEG_INF), axis=-1, keepdims=True) for b in range(n_full)]
    s_past = s_past + jnp.concatenate([jnp.broadcast_to(c, (rows, MOBA_BLOCK)) for c in bias_cols], axis=1)

    tok_of_row = lax.broadcasted_iota(jnp.int32, (rows, 1), 0) // ATTN_HEADS
    s_own = []
    for t in range(step_len):
        s_t = jnp.sum(qs * kn[t:t + 1, :], axis=-1, keepdims=True)
        s_own.append(jnp.where(tok_of_row >= t, s_t, NEG_INF))

    m = jnp.maximum(functools.reduce(jnp.maximum, s_own), jnp.max(s_past, axis=-1, keepdims=True))
    e_past = jnp.exp(s_past - m)
    l = jnp.sum(e_past, axis=-1, keepdims=True)
    o = _dot_nt(e_past.astype(BF16), jnp.concatenate([vt(p).astype(BF16) for p in range(n_pages)], axis=1))
    for t in range(step_len):
        e = jnp.exp(s_own[t] - m)
        l = l + e
        o = o + e * vn[t:t + 1, :]
    o = jnp.where(own_head, o, 0.0) * (1.0 / l)
    return [jnp.sum(o[t * ATTN_HEADS:(t + 1) * ATTN_HEADS, :], axis=0, keepdims=True) for t in range(step_len)]


def _moba_sample_kernel(pt_ref, q_ref, kn_ref, vn_ref, ck_hbm, cv_hbm, o_ref, kbuf, vbuf, sems, *, n_pages, page, step_len):
    i = pl.program_id(0)
    n_seq = q_ref.shape[0]

    def page_copies(step, slot):
        copies = []
        for j in range(n_seq * n_pages):
            phys = pt_ref[step * n_seq + j // n_pages, j % n_pages]
            copies.append(pltpu.make_async_copy(ck_hbm.at[phys], kbuf.at[slot, j], sems.at[0, slot]))
            copies.append(pltpu.make_async_copy(cv_hbm.at[phys], vbuf.at[slot, j], sems.at[1, slot]))
        return copies

    @pl.when(i == 0)
    def _first():
        for c in page_copies(0, 0):
            c.start()

    @pl.when(i + 1 < pl.num_programs(0))
    def _prefetch():
        for c in page_copies(i + 1, (i + 1) % 2):
            c.start()

    slot = i % 2
    for c in page_copies(i, slot):
        c.wait()

    o_ref[...] = jnp.zeros(o_ref.shape, F32)
    for s in range(n_seq):
        k_refs = [kbuf.at[slot, s * n_pages + p] for p in range(n_pages)]
        v_refs = [vbuf.at[slot, s * n_pages + p] for p in range(n_pages)]
        out_rows = _moba_sample_seq(q_ref[s], kn_ref[s], vn_ref[s], k_refs, v_refs, page=page, step_len=step_len)
        for t, row in enumerate(out_rows):
            o_ref[s, t:t + 1, :] = row


def _moba_sample(qa, ka, va, cache_kt, cache_vt, page_table, step_len, *, n_seq):
    db, tp, _ = qa.shape
    n_pages = page_table.shape[1]
    page = cache_kt.shape[-1]
    assert (n_pages * page) % MOBA_BLOCK == 0, "past length must be whole key blocks"
    assert MOBA_BLOCK % page == 0 and db % n_seq == 0
    tok = pl.BlockSpec((n_seq, tp, ATTN_WIDTH), lambda b, pt: (b, 0, 0))
    hbm = pl.BlockSpec(memory_space=pl.ANY)
    page_buf = pltpu.VMEM((2, n_seq * n_pages) + cache_kt.shape[1:], F32)
    grid_spec = pltpu.PrefetchScalarGridSpec(
        num_scalar_prefetch=1,
        grid=(db // n_seq,),
        in_specs=[tok, tok, tok, hbm, hbm],
        out_specs=tok,
        scratch_shapes=[page_buf, page_buf, pltpu.SemaphoreType.DMA((2, 2))],
    )
    return pl.pallas_call(
        functools.partial(_moba_sample_kernel, n_pages=n_pages, page=page, step_len=step_len),
        grid_spec=grid_spec,
        out_shape=jax.ShapeDtypeStruct(qa.shape, F32),
        compiler_params=pltpu.CompilerParams(dimension_semantics=("arbitrary",), vmem_limit_bytes=VMEM_LIMIT),
        name="moba_sample",
    )(page_table, qa, ka, va, cache_kt, cache_vt)


def _rope_tables(pos):
    half = ROPE_DIMS // 2
    inv = ROPE_THETA ** (-jnp.arange(half, dtype=F32) * 2.0 / ROPE_DIMS)
    ang = pos.astype(F32)[:, None] * inv[None, :]
    cos, sin = jnp.cos(ang), jnp.sin(ang)
    j = jnp.arange(LANES) % ATTN_HEAD_DIM
    first, second = j < half, jnp.logical_and(j >= half, j < ROPE_DIMS)
    cj, sj = cos[:, j % half], sin[:, j % half]
    return (jnp.where(jnp.logical_or(first, second), cj, 1.0),
            jnp.where(first, -sj, 0.0),
            jnp.where(second, sj, 0.0))


def _ret_rot_tables(pos):
    inv = 1.0 / (RET_ROT_BASE ** jnp.linspace(0.0, 1.0, RET_HEAD_DIM // 2, dtype=F32))
    ang = pos.astype(F32)[:, None] * inv[None, :]
    cos, sin = jnp.cos(ang), jnp.sin(ang)
    lane = jnp.arange(LANES)
    cl, sl = cos[:, lane // 2], sin[:, lane // 2]
    even = lane % 2 == 0
    return cl, jnp.where(even, -sl, 0.0), jnp.where(even, 0.0, sl)


def _layer_weights(l, norm_ffn1_w, ffn1_w_gate, ffn1_w_up, ffn1_w_down, norm_mix_w, w_in, q_norm_w, k_norm_w, w_out,
                   norm_ffn2_w, ffn2_w_gate, ffn2_w_up, ffn2_w_down):
    head = jnp.arange(MXU_TILE) // ATTN_HEAD_DIM
    return {
        "n1": norm_ffn1_w[l][None], "nm": norm_mix_w[l][None], "n2": norm_ffn2_w[l][None],
        "wg1": ffn1_w_gate[l].astype(BF16), "wu1": ffn1_w_up[l].astype(BF16), "wd1": ffn1_w_down[l].astype(BF16),
        "wg2": ffn2_w_gate[l].astype(BF16), "wu2": ffn2_w_up[l].astype(BF16), "wd2": ffn2_w_down[l].astype(BF16),
        "win": w_in[l].astype(BF16), "wo": w_out[l].astype(BF16),
        "qn": jnp.tile(q_norm_w[l], ATTN_HEADS)[None], "kn": jnp.tile(k_norm_w[l], ATTN_HEADS)[None],
        "bd": jnp.where(head[:, None] == head[None, :], 1.0 / ATTN_HEAD_DIM, 0.0).astype(BF16),
    }


def kernel(x_prompt, x_sample, cache_k, cache_v, state_ret, page_table, c_prompt, c_sample, w_ada, b_ada, norm_ffn1_w, ffn1_w_gate, ffn1_w_up, ffn1_w_down, norm_mix_w, w_in, q_norm_w, k_norm_w, w_out, norm_ffn2_w, ffn2_w_gate, ffn2_w_up, ffn2_w_down):
    batch, seq, _ = x_prompt.shape
    db, step_len, _ = x_sample.shape
    depth = w_ada.shape[0]
    n_pages, page = page_table.shape[1], cache_k.shape[2]
    past_len = n_pages * page
    tm_p = 512
    tp = SUBLANES
    assert seq % tm_p == 0 and step_len <= tp and db % SUBLANES == 0

    tabs_p = [t.reshape(seq // tm_p, tm_p, LANES)
              for t in _rope_tables(jnp.arange(seq)) + _ret_rot_tables(jnp.arange(seq))]
    pos_s = past_len + jnp.arange(step_len)
    tabs_s = [jnp.repeat(t, db, axis=0)[None] for t in _rope_tables(pos_s) + _ret_rot_tables(pos_s)]

    hp = x_prompt.reshape(batch * seq, D_MODEL)
    hs = x_sample.transpose(1, 0, 2).reshape(step_len * db, D_MODEL)
    outs = [[] for _ in range(6)]
    for l in range(depth):
        lw = _layer_weights(l, norm_ffn1_w, ffn1_w_gate, ffn1_w_up, ffn1_w_down, norm_mix_w, w_in, q_norm_w, k_norm_w,
                            w_out, norm_ffn2_w, ffn2_w_gate, ffn2_w_up, ffn2_w_down)
        mods = _mods(jnp.concatenate([c_prompt, c_sample], axis=0), w_ada[l], b_ada[l])
        mods_p = mods[:batch].reshape(batch, N_MODS, 1, D_MODEL)
        mods_s = mods[batch:].reshape(db, N_MODS, D_MODEL).transpose(1, 0, 2)[None]

        tiles_per_seq = seq // tm_p
        h1, qa, kt, vt, o_ret, st_p = _ffn_proj(
            hp, mods_p, tabs_p, lw, tm=tm_p, mod_idx=lambda i: i // tiles_per_seq, tab_idx=lambda i: i % tiles_per_seq,
            kv_groups=batch, kv_idx=lambda i: (i // tiles_per_seq, 0, i % tiles_per_seq), ret_seqs=batch)
        q_norm2_max = jnp.max(jnp.square(q_norm_w[l])) * (LOG2_E * LOG2_E)
        oa = _moba_prompt(qa, kt, vt, q_norm2_max, batch, seq)
        hp = _out_ffn(h1, oa, o_ret, mods_p, lw, tm=tm_p, mod_idx=lambda i: i // tiles_per_seq)

        h1s, qas, kts, vts, qrs, krs, vrs, grs = _ffn_proj(
            hs, mods_s, tabs_s, lw, tm=step_len * db, mod_idx=lambda i: 0, tab_idx=lambda i: 0,
            kv_groups=step_len, kv_idx=lambda i: (0, 0, 0), kv_tiles=step_len)

        def seq_major(t):
            t = t.reshape(step_len, db, ATTN_WIDTH).transpose(1, 0, 2)
            return jnp.pad(t, ((0, 0), (0, tp - step_len), (0, 0)))

        def seq_major_t(t):
            return jnp.pad(t.transpose(2, 0, 1), ((0, 0), (0, tp - step_len), (0, 0)))

        def token_major(t):
            return t[:, :step_len].transpose(1, 0, 2).reshape(step_len * db, ATTN_WIDTH)

        cache_kt = cache_k[l].transpose(0, 2, 3, 1)
        cache_vt = cache_v[l].transpose(0, 2, 3, 1)
        oas = _moba_sample(seq_major(qas), seq_major_t(kts), seq_major_t(vts), cache_kt, cache_vt, page_table, step_len,
                           n_seq=2)
        o_rets, st_s = _ret_sample(seq_major(qrs), seq_major(krs), seq_major(vrs), seq_major(grs), state_ret[l],
                                   step_len, ns=SUBLANES)
        hs = _out_ffn(h1s, token_major(oas), token_major(o_rets), mods_s, lw, tm=step_len * db, mod_idx=lambda i: 0)

        heads = (ATTN_HEADS, ATTN_HEAD_DIM)
        outs[0].append(kt.reshape(batch, *heads, seq).transpose(0, 3, 1, 2))
        outs[1].append(vt.reshape(batch, *heads, seq).transpose(0, 3, 1, 2))
        outs[2].append(kts.reshape(step_len, *heads, db).transpose(3, 0, 1, 2))
        outs[3].append(vts.reshape(step_len, *heads, db).transpose(3, 0, 1, 2))
        outs[4].append(st_p)
        outs[5].append(st_s)

    y_prompt = hp.reshape(batch, seq, D_MODEL)
    y_sample = hs.reshape(step_len, db, D_MODEL).transpose(1, 0, 2)
    return (y_prompt, y_sample) + tuple(jnp.stack(o) for o in outs)
```

```python
import functools

import jax
import jax.numpy as jnp
from jax import lax
from jax.experimental import pallas as pl
from jax.experimental.pallas import tpu as pltpu

F32 = jnp.float32
BF16 = jnp.bfloat16

D_MODEL = 1024
ATTN_HEADS = 8
ATTN_HEAD_DIM = 64
ATTN_WIDTH = ATTN_HEADS * ATTN_HEAD_DIM
RET_HEADS = 4
RET_HEAD_DIM = 128
RET_WIDTH = RET_HEADS * RET_HEAD_DIM
IN_WIDTH = 3 * ATTN_WIDTH + 4 * RET_WIDTH
MOBA_BLOCK = 256
MOBA_TOP_K = 3
ROPE_THETA = 500000.0
ROPE_DIMS = ATTN_HEAD_DIM // 4
RET_CHUNK = 128
RET_ROT_BASE = 10000.0
N_MODS = 9
EPS = 1e-6
NEG_INF = -1e30

LANES = 128
SUBLANES = 8
VMEM_LIMIT = 56 * 1024 * 1024
MXU_TILE = 256
FF_CHUNK = MXU_TILE
MOBA_GROUP = 4
LOG2_E = 1.4426950408889634
SHIFT_LANE = 63
SHIFT_LIMIT = 60.0


def _dot(a, b):
    return jnp.dot(a, b, preferred_element_type=F32)


def _dot_nt(a, b):
    return lax.dot_general(a, b, (((1,), (1,)), ((), ())), preferred_element_type=F32)


def _split(a):
    hi = a.astype(BF16)
    lo = (a - hi.astype(F32)).astype(BF16)
    return hi, lo


def _dot3(a, b):
    ah, al = _split(a)
    bh, bl = _split(b)
    return _dot(ah, bh) + (_dot(ah, bl) + _dot(al, bh))


def _dot3_nt(a, b):
    ah, al = _split(a)
    bh, bl = _split(b)
    return _dot_nt(ah, bh) + (_dot_nt(ah, bl) + _dot_nt(al, bh))


def _sigmoid(x):
    return 1.0 / (1.0 + jnp.exp(-x))


def _silu(x):
    return x * _sigmoid(x)


def _rms(x):
    return x * lax.rsqrt(jnp.mean(x * x, axis=-1, keepdims=True) + EPS)


def _top_k_lanes(g, lane_f, k):
    sel = jnp.zeros(g.shape, jnp.bool_)
    for _ in range(k):
        m = jnp.max(g, axis=-1, keepdims=True)
        idx = jnp.min(jnp.where(g == m, lane_f, 1e9), axis=-1, keepdims=True)
        pick = lane_f == idx
        sel = jnp.logical_or(sel, pick)
        g = jnp.where(pick, -jnp.inf, g)
    return sel


def _fold_rows(op, s):
    while s.shape[0] > SUBLANES:
        half = s.shape[0] // 2
        s = op(s[:half], s[half:])
    return s


def _col_max(s):
    return jnp.max(_fold_rows(jnp.maximum, s), axis=0, keepdims=True)


def _col_min(s):
    return jnp.min(_fold_rows(jnp.minimum, s), axis=0, keepdims=True)


def _top_k_rows(g, row_f, k):
    sel = jnp.zeros(g.shape, jnp.bool_)
    for _ in range(k):
        m = _col_max(g)
        idx = _col_min(jnp.where(g == m, row_f, 1e9))
        pick = row_f == idx
        sel = jnp.logical_or(sel, pick)
        g = jnp.where(pick, -jnp.inf, g)
    return sel


def _mods_kernel(c_ref, w_ref, b_ref, o_ref):
    s = _silu(c_ref[...]).astype(BF16)
    o_ref[...] = _dot(s, w_ref[...].astype(BF16)) + b_ref[...]


def _mods(c, w_ada, b_ada):
    n = c.shape[0]
    n_pad = -(-n // SUBLANES) * SUBLANES
    c = jnp.pad(c, ((0, n_pad - n), (0, 0)))
    width = w_ada.shape[1]
    tn = 9 * LANES
    out = pl.pallas_call(
        _mods_kernel,
        grid=(width // tn,),
        in_specs=[pl.BlockSpec((n_pad, D_MODEL), lambda j: (0, 0)),
                  pl.BlockSpec((D_MODEL, tn), lambda j: (0, j)),
                  pl.BlockSpec((1, tn), lambda j: (0, j))],
        out_specs=pl.BlockSpec((n_pad, tn), lambda j: (0, j)),
        out_shape=jax.ShapeDtypeStruct((n_pad, width), F32),
        name="mods",
    )(c, w_ada, b_ada.reshape(1, width))
    return out[:n]


def _swiglu_acc(xb, wg_ref, wu_ref, wd_ref):
    d_ff = wg_ref.shape[1]
    acc = None
    for c in range(d_ff // FF_CHUNK):
        sl = slice(c * FF_CHUNK, (c + 1) * FF_CHUNK)
        g = _dot(xb, wg_ref[:, sl])
        u = _dot(xb, wu_ref[:, sl])
        part = _dot((_silu(g) * u).astype(BF16), wd_ref[sl, :])
        acc = part if acc is None else acc + part
    return acc


def _mod_rows(mod_ref, i, rows):
    m = mod_ref[0, i]
    reps = rows // m.shape[0]
    return m if m.shape[0] == 1 or reps == 1 else jnp.concatenate([m] * reps, axis=0)


def _const_spec(shape):
    nd = len(shape)
    return pl.BlockSpec(shape, lambda *_: (0,) * nd, pipeline_mode=pl.Buffered(1))


def _head_norm_gate(out, g):
    on = out * lax.rsqrt(jnp.mean(out * out, axis=-1, keepdims=True) + EPS)
    return on * _silu(g)


def _ret_chunks(q_ref, k_ref, v_ref, g_ref, dec_ref, qd_ref, kd_ref, cd_ref, o_ref, s_ref, fresh):
    c_len = RET_CHUNK
    n_chunks = q_ref.shape[0] // c_len
    pairs = [(c, h) for c in range(n_chunks) for h in range(RET_HEADS)]
    rows = lambda c: slice(c * c_len, (c + 1) * c_len)
    cols = lambda h: slice(h * LANES, (h + 1) * LANES)
    scores, updates = {}, {}
    for c, h in pairs:
        q, k, vb = q_ref[rows(c), cols(h)], k_ref[rows(c), cols(h)], v_ref[rows(c), cols(h)].astype(BF16)
        scores[c, h] = (_dot_nt(q.astype(BF16), k.astype(BF16)) * dec_ref[h]).astype(BF16)
        updates[c, h] = _dot((k * kd_ref[:, cols(h)]).T.astype(BF16), vb)
    states = {}
    for h in range(RET_HEADS):
        st = jnp.where(fresh, 0.0, s_ref[h])
        for c in range(n_chunks):
            states[c, h] = st
            st = cd_ref[h] * st + updates[c, h]
        s_ref[h] = st
    for c, h in pairs:
        q, vb = q_ref[rows(c), cols(h)], v_ref[rows(c), cols(h)].astype(BF16)
        out = _dot(scores[c, h], vb) + _dot((q * qd_ref[:, cols(h)]).astype(BF16), states[c, h].astype(BF16))
        o_ref[rows(c), cols(h)] = _head_norm_gate(out, g_ref[rows(c), cols(h)])


def _ret_tables(chunk_len):
    lg = jnp.log(1.0 - 2.0 ** (-5.0 - jnp.arange(RET_HEADS, dtype=F32)))
    i = jnp.arange(RET_CHUNK, dtype=F32)
    diff = i[:, None] - i[None, :]
    decay = jnp.where(diff >= 0, jnp.exp(jnp.maximum(diff, 0.0)[None] * lg[:, None, None]), 0.0)
    qd = jnp.exp((i + 1.0)[None] * lg[:, None])
    kd = jnp.exp((chunk_len - 1.0 - i)[None] * lg[:, None])
    widen = lambda t: jnp.broadcast_to(t.T[:, :, None], (RET_CHUNK, RET_HEADS, RET_HEAD_DIM)).reshape(RET_CHUNK, RET_WIDTH)
    cd = jnp.broadcast_to(jnp.exp(chunk_len * lg)[:, None, None], (RET_HEADS, 1, LANES))
    return decay, widen(qd), widen(kd), cd


def _ffn_proj_kernel(x_ref, mod_ref, n1_ref, wg_ref, wu_ref, wd_ref, nm_ref, win_ref, qn_ref, kn_ref, bd_ref,
                     ca_ref, sa1_ref, sa2_ref, cr_ref, sr1_ref, sr2_ref, *refs, tiles_per_seq):
    if tiles_per_seq is None:
        h_ref, qa_ref, ka_ref, va_ref, qr_ref, kr_ref, vr_ref, gr_ref = refs
    else:
        ret_tabs, (h_ref, qa_ref, ka_ref, va_ref, oret_ref, st_ref), (qr_ref, kr_ref, vr_ref, gr_ref, s_ref) = (
            refs[:4], refs[4:10], refs[10:])

        @pl.when(pl.program_id(0) == 0)
        def _init_state():
            s_ref[...] = jnp.zeros(s_ref.shape, F32)
    x = x_ref[...]
    mod = lambda i: _mod_rows(mod_ref, i, x.shape[0])
    xn = (_rms(x) * n1_ref[...]) * (1.0 + mod(1)) + mod(0)
    acc = _swiglu_acc(xn.astype(BF16), wg_ref, wu_ref, wd_ref)
    h = x + 0.5 * mod(2) * acc
    h_ref[...] = h
    hn = (_rms(h) * nm_ref[...]) * (1.0 + mod(4)) + mod(3)
    hb = hn.astype(BF16)

    def seg(i):
        return _dot(hb, win_ref[:, i * ATTN_WIDTH:(i + 1) * ATTN_WIDTH])

    ca, sa1, sa2 = ca_ref[0], sa1_ref[0], sa2_ref[0]
    cr, sr1, sr2 = cr_ref[0], sr1_ref[0], sr2_ref[0]
    bd = bd_ref[...]

    def store_t(o_ref, g, r):
        width = o_ref.shape[2]
        for t in range(o_ref.shape[0]):
            o_ref[t, g * LANES:(g + 1) * LANES, :] = r[t * width:(t + 1) * width, :].T

    def attn_head_norm_rope(p, w, o_ref, transposed):
        hi, lo = _split(p * p)
        ms = jnp.concatenate([_dot(hi[:, c:c + MXU_TILE], bd) + _dot(lo[:, c:c + MXU_TILE], bd)
                              for c in range(0, ATTN_WIDTH, MXU_TILE)], axis=1)
        pn = (p * lax.rsqrt(ms + EPS)) * w
        for g in range(ATTN_WIDTH // LANES):
            xg = pn[:, g * LANES:(g + 1) * LANES]
            r = xg * ca + pltpu.roll(xg, LANES - ROPE_DIMS // 2, 1) * sa1 + pltpu.roll(xg, ROPE_DIMS // 2, 1) * sa2
            if transposed:
                store_t(o_ref, g, r)
            else:
                o_ref[:, g * LANES:(g + 1) * LANES] = r

    def ret_rotate(p, o_ref, scale):
        for g in range(RET_HEADS):
            xg = p[:, g * LANES:(g + 1) * LANES]
            r = xg * cr + pltpu.roll(xg, LANES - 1, 1) * sr1 + pltpu.roll(xg, 1, 1) * sr2
            o_ref[:, g * LANES:(g + 1) * LANES] = r if scale is None else r * scale

    attn_head_norm_rope(seg(0), qn_ref[...], qa_ref, False)
    attn_head_norm_rope(seg(1), kn_ref[...], ka_ref, True)
    va = seg(2)
    for g in range(ATTN_WIDTH // LANES):
        store_t(va_ref, g, va[:, g * LANES:(g + 1) * LANES])
    ret_rotate(seg(3), qr_ref, None)
    ret_rotate(seg(4), kr_ref, RET_HEAD_DIM ** -0.5)
    vr_ref[...] = seg(5)
    gr_ref[...] = seg(6)
    if tiles_per_seq is not None:
        _ret_chunks(qr_ref, kr_ref, vr_ref, gr_ref, *ret_tabs, oret_ref, s_ref, pl.program_id(0) % tiles_per_seq == 0)
        st_ref[0] = s_ref[...]


def _ffn_proj(x2d, mods, tabs, lw, *, tm, mod_idx, tab_idx, kv_groups, kv_idx, kv_tiles=1, ret_seqs=None):
    n = x2d.shape[0]
    rm = mods.shape[2]
    rt = tabs[0].shape[1]
    d_ff = lw["wg1"].shape[1]
    row = lambda i: (i, 0)
    tab_spec = pl.BlockSpec((1, rt, LANES), lambda i: (tab_idx(i), 0, 0))
    in_specs = [
        pl.BlockSpec((tm, D_MODEL), row),
        pl.BlockSpec((1, N_MODS, rm, D_MODEL), lambda i: (mod_idx(i), 0, 0, 0)),
        _const_spec((1, D_MODEL)),
        _const_spec((D_MODEL, d_ff)), _const_spec((D_MODEL, d_ff)), _const_spec((d_ff, D_MODEL)),
        _const_spec((1, D_MODEL)),
        _const_spec((D_MODEL, IN_WIDTH)),
        _const_spec((1, ATTN_WIDTH)), _const_spec((1, ATTN_WIDTH)),
        _const_spec((MXU_TILE, MXU_TILE)),
    ] + [tab_spec] * 6
    half = jax.ShapeDtypeStruct((n, ATTN_WIDTH), F32)
    half_t = jax.ShapeDtypeStruct((kv_groups, ATTN_WIDTH, n // kv_groups), F32)
    half_spec = pl.BlockSpec((tm, ATTN_WIDTH), row)
    half_t_spec = pl.BlockSpec((kv_tiles, ATTN_WIDTH, tm // kv_tiles), lambda i: kv_idx(i))
    out_shape = [jax.ShapeDtypeStruct((n, D_MODEL), F32), half, half_t, half_t]
    out_specs = [pl.BlockSpec((tm, D_MODEL), row), half_spec, half_t_spec, half_t_spec]
    operands = [x2d, mods, lw["n1"], lw["wg1"], lw["wu1"], lw["wd1"], lw["nm"], lw["win"], lw["qn"], lw["kn"], lw["bd"], *tabs]
    if ret_seqs is None:
        tiles_per_seq, scratch = None, []
        out_shape += [half] * 4
        out_specs += [half_spec] * 4
    else:
        tiles_per_seq = n // ret_seqs // tm
        state = (RET_HEADS, RET_HEAD_DIM, RET_HEAD_DIM)
        operands += list(_ret_tables(RET_CHUNK))
        in_specs += [_const_spec((RET_HEADS, RET_CHUNK, RET_CHUNK)), _const_spec((RET_CHUNK, RET_WIDTH)),
                     _const_spec((RET_CHUNK, RET_WIDTH)), _const_spec((RET_HEADS, 1, LANES))]
        out_shape += [half, jax.ShapeDtypeStruct((ret_seqs,) + state, F32)]
        out_specs += [half_spec, pl.BlockSpec((1,) + state, lambda i: (i // tiles_per_seq, 0, 0, 0))]
        scratch = [pltpu.VMEM((tm, RET_WIDTH), F32)] * 4 + [pltpu.VMEM(state, F32)]
    return pl.pallas_call(
        functools.partial(_ffn_proj_kernel, tiles_per_seq=tiles_per_seq),
        grid=(n // tm,),
        in_specs=in_specs,
        out_specs=out_specs,
        out_shape=out_shape,
        scratch_shapes=scratch,
        compiler_params=pltpu.CompilerParams(dimension_semantics=("arbitrary",), vmem_limit_bytes=VMEM_LIMIT),
        name="ffn_proj",
    )(*operands)


def _out_ffn_kernel(h_ref, oa_ref, or_ref, mod_ref, wo_ref, n2_ref, wg_ref, wu_ref, wd_ref, y_ref):
    mix = _dot(oa_ref[...].astype(BF16), wo_ref[:ATTN_WIDTH, :]) + _dot(or_ref[...].astype(BF16), wo_ref[ATTN_WIDTH:, :])
    mod = lambda i: _mod_rows(mod_ref, i, h_ref.shape[0])
    h = h_ref[...] + mod(5) * mix
    hn = (_rms(h) * n2_ref[...]) * (1.0 + mod(7)) + mod(6)
    acc = _swiglu_acc(hn.astype(BF16), wg_ref, wu_ref, wd_ref)
    y_ref[...] = h + 0.5 * mod(8) * acc


def _out_ffn(h2d, oa, o_ret, mods, lw, *, tm, mod_idx):
    n = h2d.shape[0]
    rm = mods.shape[2]
    d_ff = lw["wg2"].shape[1]
    row = lambda i: (i, 0)
    return pl.pallas_call(
        _out_ffn_kernel,
        grid=(n // tm,),
        in_specs=[
            pl.BlockSpec((tm, D_MODEL), row),
            pl.BlockSpec((tm, ATTN_WIDTH), row),
            pl.BlockSpec((tm, RET_WIDTH), row),
            pl.BlockSpec((1, N_MODS, rm, D_MODEL), lambda i: (mod_idx(i), 0, 0, 0)),
            _const_spec((D_MODEL, D_MODEL)),
            _const_spec((1, D_MODEL)),
            _const_spec((D_MODEL, d_ff)), _const_spec((D_MODEL, d_ff)), _const_spec((d_ff, D_MODEL)),
        ],
        out_specs=pl.BlockSpec((tm, D_MODEL), row),
        out_shape=jax.ShapeDtypeStruct((n, D_MODEL), F32),
        compiler_params=pltpu.CompilerParams(dimension_semantics=("arbitrary",), vmem_limit_bytes=VMEM_LIMIT),
        name="out_ffn",
    )(h2d, oa, o_ret, mods, lw["wo"], lw["n2"], lw["wg2"], lw["wu2"], lw["wd2"])


def _moba_prompt_kernel(q_ref, qn_ref, kt_ref, vt_in_ref, qn2_ref, o_ref,
                        kaug_ref, vt_ref, km_ref, bound_ref, flag_ref, acc_ref, shift_ref, bias_ref, *, nb):
    qi = pl.program_id(2)
    tq = MOBA_BLOCK
    hd = ATTN_HEAD_DIM
    grp = MOBA_GROUP
    lane = lax.broadcasted_iota(jnp.int32, (tq, LANES), 1)
    in_head = (lane < hd, lane >= hd)
    field_off = (hd, 0)

    @pl.when(qi == 0)
    def _prepare():
        km_ref[...] = jnp.zeros(km_ref.shape, F32)
        feat = lax.broadcasted_iota(jnp.int32, (LANES, tq), 0)
        feat_in_head = (feat < hd, feat >= hd)
        kn2 = [jnp.zeros((1, 1), F32)] * 2
        for j in range(nb):
            kb = kt_ref[0, :, j * tq:(j + 1) * tq].T
            vtb = vt_in_ref[0, :, j * tq:(j + 1) * tq]
            km_ref[hd + j:hd + j + 1, :] = jnp.mean(kb, axis=0, keepdims=True)
            sq = kb * kb
            for h in range(2):
                norm2 = jnp.sum(jnp.where(in_head[h], sq, 0.0), axis=1, keepdims=True)
                kn2[h] = jnp.maximum(kn2[h], jnp.max(norm2, axis=0, keepdims=True))
                off = field_off[h]
                field = jnp.where(jnp.logical_or(lane == off + j, lane == off + SHIFT_LANE), 1.0, 0.0)
                kaug_ref[h, j * tq:(j + 1) * tq, :] = jnp.where(in_head[h], kb, field).astype(BF16)
                vt_ref[h, j] = jnp.where(feat_in_head[h], vtb, 1.0).astype(BF16)
        bound = [jnp.sqrt(qn2_ref[0:1, 0:1] * kn2[h]) for h in range(2)]
        for h in range(2):
            bound_ref[h] = jnp.broadcast_to(bound[h], bound_ref.shape[1:])
        flag_ref[0] = (jnp.max(jnp.maximum(bound[0], bound[1])) <= SHIFT_LIMIT).astype(jnp.int32)

        bias_ref[...] = jnp.full(bias_ref.shape, NEG_INF, F32)

    q = q_ref[...]
    scale = hd ** -0.5 * LOG2_E
    qs = [jnp.where(in_head[h], q, 0.0) * scale for h in range(2)]

    field_row = lax.broadcasted_iota(jnp.int32, (LANES, tq), 0)
    field_row_f = field_row.astype(F32)

    def block_bias(q_tile, h, n_past):
        off = field_off[h]
        q_head = jnp.where(in_head[h], q_tile, 0.0)
        gate_t = _dot3_nt(km_ref[hd - off:hd - off + LANES, :], q_head)
        valid = jnp.logical_and(field_row >= off, field_row < off + n_past)
        sel = _top_k_rows(jnp.where(valid, gate_t, NEG_INF), field_row_f, MOBA_TOP_K)
        return jnp.where(jnp.logical_and(sel, valid), 0.0, NEG_INF).T

    key_i = lax.broadcasted_iota(jnp.int32, (tq, tq), 0)
    qry_i = lax.broadcasted_iota(jnp.int32, (tq, tq), 1)
    causal = key_i <= qry_i
    own_rows = pl.ds(pl.multiple_of(qi * tq, tq), tq)
    n_groups = (qi + grp - 1) // grp

    def group_rows(g):
        return pl.ds(pl.multiple_of(g * (grp * tq), grp * tq), grp * tq)

    bounded = flag_ref[0] == 1

    @pl.when(bounded)
    def _bound_shift():
        for h in range(2):
            shift_ref[h] = jnp.broadcast_to(bound_ref[h, 0:1, :], (tq, LANES))

    @pl.when(jnp.logical_not(bounded))
    def _exact_shift():
        for h in range(2):
            q_sel = jnp.where(in_head[h], qs[h], bias_ref[h]).astype(BF16)
            m = _col_max(jnp.where(causal, _dot_nt(kaug_ref[h, own_rows, :], qs[h].astype(BF16)), NEG_INF))
            m = lax.fori_loop(
                0, n_groups, lambda g, m: jnp.maximum(m, _col_max(_dot_nt(kaug_ref[h, group_rows(g), :], q_sel))), m)
            shift_ref[h] = jnp.broadcast_to(m, (LANES, tq)).T

    q_own, q_past = [], []
    for h in range(2):
        at_shift = lane == field_off[h] + SHIFT_LANE
        q_own.append(jnp.where(in_head[h], qs[h], jnp.where(at_shift, -shift_ref[h], 0.0)))
        q_past.append(jnp.where(jnp.logical_or(in_head[h], at_shift), q_own[h], bias_ref[h]).astype(BF16))
    acc_ref[...] = jnp.zeros(acc_ref.shape, F32)

    def add_groups(groups, with_own=False):
        scores = [[_dot_nt(kaug_ref[h, group_rows(g), :], q_past[h]) for h in range(2)] for g in groups]
        if with_own:
            own_scores = [_dot_nt(kaug_ref[h, own_rows, :], q_own[h].astype(BF16)) for h in range(2)]
            q_next = qn_ref[...]
            for h in range(2):
                bias_ref[h] = block_bias(q_next, h, qi + 1)
            for h in range(2):
                s = jnp.where(causal, own_scores[h], NEG_INF)
                acc_ref[h] += _dot(vt_ref[h, qi], jnp.exp2(s).astype(BF16))
        for g, sc in zip(groups, scores):
            for h in range(2):
                p = jnp.exp2(sc[h]).astype(BF16)
                pv = None
                for c in range(grp):
                    part = _dot(vt_ref[h, g * grp + c], p[c * tq:(c + 1) * tq, :])
                    pv = part if pv is None else pv + part
                acc_ref[h] += pv

    def pair_body(t, carry):
        add_groups([2 * t, 2 * t + 1])
        return carry

    has_pair = n_groups >= 2
    odd = n_groups % 2 == 1

    @pl.when(has_pair)
    def _first_pair():
        add_groups([0, 1], with_own=True)

    lax.fori_loop(1, n_groups // 2, pair_body, 0)

    @pl.when(jnp.logical_and(has_pair, odd))
    def _last_group():
        add_groups([n_groups - 1])

    @pl.when(n_groups == 1)
    def _only_group():
        add_groups([0], with_own=True)

    @pl.when(n_groups == 0)
    def _no_group():
        add_groups([], with_own=True)

    a0, a1 = acc_ref[0], acc_ref[1]
    o_t = jnp.concatenate([a0[:hd] * (1.0 / a0[hd:hd + 1]), a1[hd:] * (1.0 / a1[0:1])], axis=0)
    o_ref[...] = o_t.T


def _moba_prompt(qa, kt, vt, q_norm2_max, batch, seq):
    tq = MOBA_BLOCK
    assert seq % (tq * MOBA_GROUP) == 0
    nb = seq // tq
    assert nb <= SHIFT_LANE, "side field holds one lane per key block below the shift lane"
    n_pairs = ATTN_WIDTH // LANES
    kv_spec = pl.BlockSpec((1, LANES, seq), lambda b, hp, qi: (b, hp, 0))
    q_spec = pl.BlockSpec((tq, LANES), lambda b, hp, qi: (b * nb + qi, hp))
    q_next_spec = pl.BlockSpec((tq, LANES), lambda b, hp, qi: (b * nb + jnp.minimum(qi + 1, nb - 1), hp))
    return pl.pallas_call(
        functools.partial(_moba_prompt_kernel, nb=nb),
        grid=(batch, n_pairs, nb),
        in_specs=[q_spec, q_next_spec, kv_spec, kv_spec, _const_spec((1, LANES))],
        out_specs=q_spec,
        out_shape=jax.ShapeDtypeStruct(qa.shape, F32),
        scratch_shapes=[
            pltpu.VMEM((2, seq, LANES), BF16),
            pltpu.VMEM((2, nb, LANES, tq), BF16),
            pltpu.VMEM((ATTN_HEAD_DIM + LANES, LANES), F32),
            pltpu.VMEM((2, SUBLANES, LANES), F32),
            pltpu.SMEM((1,), jnp.int32),
            pltpu.VMEM((2, LANES, tq), F32),
            pltpu.VMEM((2, tq, LANES), F32),
            pltpu.VMEM((2, tq, LANES), F32),
        ],
        compiler_params=pltpu.CompilerParams(dimension_semantics=("arbitrary", "arbitrary", "arbitrary"),
                                             vmem_limit_bytes=VMEM_LIMIT),
        name="moba_prompt",
    )(qa, qa, kt, vt, jnp.broadcast_to(q_norm2_max, (1, LANES)).astype(F32))


def _ret_sample_kernel(q_ref, k_ref, v_ref, g_ref, st_ref, dec_ref, qd_ref, kd_ref, cd_ref, o_ref, ns_ref):
    tp = q_ref.shape[1]
    zeros = jnp.zeros((RET_CHUNK - tp, LANES), F32)
    pairs = [(s, h) for s in range(q_ref.shape[0]) for h in range(RET_HEADS)]
    cols = lambda h: slice(h * LANES, (h + 1) * LANES)
    scores, values = {}, {}
    for s, h in pairs:
        k = jnp.concatenate([k_ref[s, :, cols(h)], zeros], axis=0)
        vb = jnp.concatenate([v_ref[s, :, cols(h)], zeros], axis=0).astype(BF16)
        scores[s, h] = (_dot_nt(q_ref[s, :, cols(h)].astype(BF16), k.astype(BF16)) * dec_ref[h, :tp, :]).astype(BF16)
        ns_ref[s, h] = cd_ref[h] * st_ref[s, h] + _dot((k * kd_ref[:, cols(h)]).T.astype(BF16), vb)
        values[s, h] = vb
    for s, h in pairs:
        q_dec = (q_ref[s, :, cols(h)] * qd_ref[:tp, cols(h)]).astype(BF16)
        out = _dot(scores[s, h], values[s, h]) + _dot(q_dec, st_ref[s, h].astype(BF16))
        o_ref[s, :, cols(h)] = _head_norm_gate(out, g_ref[s, :, cols(h)])


def _ret_sample(qr, kr, vr, gr, state, step_len, *, ns):
    db, tp, _ = qr.shape
    tok = pl.BlockSpec((ns, tp, RET_WIDTH), lambda i: (i, 0, 0))
    st = pl.BlockSpec((ns, RET_HEADS, RET_HEAD_DIM, RET_HEAD_DIM), lambda i: (i, 0, 0, 0))
    decay, qd, kd, cd = _ret_tables(step_len)
    return pl.pallas_call(
        _ret_sample_kernel,
        grid=(db // ns,),
        in_specs=[tok, tok, tok, tok, st,
                  _const_spec((RET_HEADS, RET_CHUNK, RET_CHUNK)),
                  _const_spec((RET_CHUNK, RET_WIDTH)), _const_spec((RET_CHUNK, RET_WIDTH)),
                  _const_spec((RET_HEADS, 1, LANES))],
        out_specs=[tok, st],
        out_shape=[jax.ShapeDtypeStruct(qr.shape, F32), jax.ShapeDtypeStruct(state.shape, F32)],
        compiler_params=pltpu.CompilerParams(dimension_semantics=("arbitrary",)),
        name="ret_sample",
    )(qr, kr, vr, gr, state, decay, qd, kd, cd)


def _moba_sample_seq(q, kn, vn, k_refs, v_refs, *, page, step_len):
    n_pages = len(k_refs)
    hd = ATTN_HEAD_DIM
    ppb = MOBA_BLOCK // page
    n_full = n_pages // ppb
    rows = step_len * ATTN_HEADS
    qrep = jnp.concatenate([jnp.broadcast_to(q[t:t + 1, :], (ATTN_HEADS, ATTN_WIDTH)) for t in range(step_len)], axis=0)
    r_i = lax.broadcasted_iota(jnp.int32, (rows, ATTN_WIDTH), 0)
    c_i = lax.broadcasted_iota(jnp.int32, (rows, ATTN_WIDTH), 1)
    own_head = (c_i // hd) == (r_i % ATTN_HEADS)
    qbd = jnp.where(own_head, qrep, 0.0)
    qs = qbd * (hd ** -0.5)

    vt = lambda p: v_refs[p][...].reshape(ATTN_WIDTH, page)
    lane = lax.broadcasted_iota(jnp.int32, (rows, LANES), 1)
    gate = jnp.full((rows, LANES), NEG_INF, F32)
    k_pages = []
    for b in range(n_full):
        ksum = None
        for p in range(b * ppb, (b + 1) * ppb):
            kt = k_refs[p][...].reshape(ATTN_WIDTH, page)
            k_pages.append(kt.astype(BF16))
            ksum = kt if ksum is None else ksum + kt
        g = jnp.sum(_dot3(qbd, ksum), axis=-1, keepdims=True) * (1.0 / MOBA_BLOCK)
        gate = jnp.where(lane == b, g, gate)
    s_past = _dot(qs.astype(BF16), jnp.concatenate(k_pages, axis=1))
    sel = jnp.logical_and(_top_k_lanes(gate, lane.astype(F32), min(MOBA_TOP_K, n_full)), lane < n_full)
    bias = jnp.where(sel, 0.0, NEG_INF)
    bias_cols = [jnp.max(jnp.where(lane == b, bias, NEG_INF), axis=-1, keepdims=True) for b in range(n_full)]
    s_past = s_past + jnp.concatenate([jnp.broadcast_to(c, (rows, MOBA_BLOCK)) for c in bias_cols], axis=1)

    tok_of_row = lax.broadcasted_iota(jnp.int32, (rows, 1), 0) // ATTN_HEADS
    s_own = []
    for t in range(step_len):
        s_t = jnp.sum(qs * kn[t:t + 1, :], axis=-1, keepdims=True)
        s_own.append(jnp.where(tok_of_row >= t, s_t, NEG_INF))

    m = jnp.maximum(functools.reduce(jnp.maximum, s_own), jnp.max(s_past, axis=-1, keepdims=True))
    e_past = jnp.exp(s_past - m)
    l = jnp.sum(e_past, axis=-1, keepdims=True)
    o = _dot_nt(e_past.astype(BF16), jnp.concatenate([vt(p).astype(BF16) for p in range(n_pages)], axis=1))
    for t in range(step_len):
        e = jnp.exp(s_own[t] - m)
        l = l + e
        o = o + e * vn[t:t + 1, :]
    o = jnp.where(own_head, o, 0.0) * (1.0 / l)
    return [jnp.sum(o[t * ATTN_HEADS:(t + 1) * ATTN_HEADS, :], axis=0, keepdims=True) for t in range(step_len)]


def _moba_sample_kernel(pt_ref, q_ref, kn_ref, vn_ref, ck_hbm, cv_hbm, o_ref, kbuf, vbuf, sems, *, n_pages, page, step_len):
    i = pl.program_id(0)
    n_seq = q_ref.shape[0]

    def page_copies(step, slot):
        copies = []
        for j in range(n_seq * n_pages):
            phys = pt_ref[step * n_seq + j // n_pages, j % n_pages]
            copies.append(pltpu.make_async_copy(ck_hbm.at[phys], kbuf.at[slot, j], sems.at[0, slot]))
            copies.append(pltpu.make_async_copy(cv_hbm.at[phys], vbuf.at[slot, j], sems.at[1, slot]))
        return copies

    @pl.when(i == 0)
    def _first():
        for c in page_copies(0, 0):
            c.start()

    @pl.when(i + 1 < pl.num_programs(0))
    def _prefetch():
        for c in page_copies(i + 1, (i + 1) % 2):
            c.start()

    slot = i % 2
    for c in page_copies(i, slot):
        c.wait()

    o_ref[...] = jnp.zeros(o_ref.shape, F32)
    for s in range(n_seq):
        k_refs = [kbuf.at[slot, s * n_pages + p] for p in range(n_pages)]
        v_refs = [vbuf.at[slot, s * n_pages + p] for p in range(n_pages)]
        out_rows = _moba_sample_seq(q_ref[s], kn_ref[s], vn_ref[s], k_refs, v_refs, page=page, step_len=step_len)
        for t, row in enumerate(out_rows):
            o_ref[s, t:t + 1, :] = row


def _moba_sample(qa, ka, va, cache_kt, cache_vt, page_table, step_len, *, n_seq):
    db, tp, _ = qa.shape
    n_pages = page_table.shape[1]
    page = cache_kt.shape[-1]
    assert (n_pages * page) % MOBA_BLOCK == 0, "past length must be whole key blocks"
    assert MOBA_BLOCK % page == 0 and db % n_seq == 0
    tok = pl.BlockSpec((n_seq, tp, ATTN_WIDTH), lambda b, pt: (b, 0, 0))
    hbm = pl.BlockSpec(memory_space=pl.ANY)
    page_buf = pltpu.VMEM((2, n_seq * n_pages) + cache_kt.shape[1:], F32)
    grid_spec = pltpu.PrefetchScalarGridSpec(
        num_scalar_prefetch=1,
        grid=(db // n_seq,),
        in_specs=[tok, tok, tok, hbm, hbm],
        out_specs=tok,
        scratch_shapes=[page_buf, page_buf, pltpu.SemaphoreType.DMA((2, 2))],
    )
    return pl.pallas_call(
        functools.partial(_moba_sample_kernel, n_pages=n_pages, page=page, step_len=step_len),
        grid_spec=grid_spec,
        out_shape=jax.ShapeDtypeStruct(qa.shape, F32),
        compiler_params=pltpu.CompilerParams(dimension_semantics=("arbitrary",), vmem_limit_bytes=VMEM_LIMIT),
        name="moba_sample",
    )(page_table, qa, ka, va, cache_kt, cache_vt)


def _rope_tables(pos):
    half = ROPE_DIMS // 2
    inv = ROPE_THETA ** (-jnp.arange(half, dtype=F32) * 2.0 / ROPE_DIMS)
    ang = pos.astype(F32)[:, None] * inv[None, :]
    cos, sin = jnp.cos(ang), jnp.sin(ang)
    j = jnp.arange(LANES) % ATTN_HEAD_DIM
    first, second = j < half, jnp.logical_and(j >= half, j < ROPE_DIMS)
    cj, sj = jnp.tile(cos, (1, LANES // half)), jnp.tile(sin, (1, LANES // half))
    return (jnp.where(jnp.logical_or(first, second), cj, 1.0),
            jnp.where(first, -sj, 0.0),
            jnp.where(second, sj, 0.0))


def _ret_rot_tables(pos):
    inv = 1.0 / (RET_ROT_BASE ** jnp.linspace(0.0, 1.0, RET_HEAD_DIM // 2, dtype=F32))
    ang = pos.astype(F32)[:, None] * inv[None, :]
    cos, sin = jnp.cos(ang), jnp.sin(ang)
    pairwise = lambda t: jnp.stack([t, t], axis=-1).reshape(t.shape[0], LANES)
    cl, sl = pairwise(cos), pairwise(sin)
    even = jnp.arange(LANES) % 2 == 0
    return cl, jnp.where(even, -sl, 0.0), jnp.where(even, 0.0, sl)


def _layer_weights(l, norm_ffn1_w, ffn1_w_gate, ffn1_w_up, ffn1_w_down, norm_mix_w, w_in, q_norm_w, k_norm_w, w_out,
                   norm_ffn2_w, ffn2_w_gate, ffn2_w_up, ffn2_w_down):
    head = jnp.arange(MXU_TILE) // ATTN_HEAD_DIM
    return {
        "n1": norm_ffn1_w[l][None], "nm": norm_mix_w[l][None], "n2": norm_ffn2_w[l][None],
        "wg1": ffn1_w_gate[l].astype(BF16), "wu1": ffn1_w_up[l].astype(BF16), "wd1": ffn1_w_down[l].astype(BF16),
        "wg2": ffn2_w_gate[l].astype(BF16), "wu2": ffn2_w_up[l].astype(BF16), "wd2": ffn2_w_down[l].astype(BF16),
        "win": w_in[l].astype(BF16), "wo": w_out[l].astype(BF16),
        "qn": jnp.tile(q_norm_w[l], ATTN_HEADS)[None], "kn": jnp.tile(k_norm_w[l], ATTN_HEADS)[None],
        "bd": jnp.where(head[:, None] == head[None, :], 1.0 / ATTN_HEAD_DIM, 0.0).astype(BF16),
    }


def kernel(x_prompt, x_sample, cache_k, cache_v, state_ret, page_table, c_prompt, c_sample, w_ada, b_ada, norm_ffn1_w, ffn1_w_gate, ffn1_w_up, ffn1_w_down, norm_mix_w, w_in, q_norm_w, k_norm_w, w_out, norm_ffn2_w, ffn2_w_gate, ffn2_w_up, ffn2_w_down):
    batch, seq, _ = x_prompt.shape
    db, step_len, _ = x_sample.shape
    depth = w_ada.shape[0]
    n_pages, page = page_table.shape[1], cache_k.shape[2]
    past_len = n_pages * page
    tm_p = 512
    tp = SUBLANES
    assert seq % tm_p == 0 and step_len <= tp and db % SUBLANES == 0

    tabs_p = [t.reshape(seq // tm_p, tm_p, LANES)
              for t in _rope_tables(jnp.arange(seq)) + _ret_rot_tables(jnp.arange(seq))]
    pos_s = past_len + jnp.arange(step_len)
    tabs_s = [jnp.broadcast_to(t[:, None, :], (step_len, db, LANES)).reshape(1, step_len * db, LANES)
              for t in _rope_tables(pos_s) + _ret_rot_tables(pos_s)]

    hp = x_prompt.reshape(batch * seq, D_MODEL)
    hs = x_sample.transpose(1, 0, 2).reshape(step_len * db, D_MODEL)
    outs = [[] for _ in range(6)]
    for l in range(depth):
        lw = _layer_weights(l, norm_ffn1_w, ffn1_w_gate, ffn1_w_up, ffn1_w_down, norm_mix_w, w_in, q_norm_w, k_norm_w,
                            w_out, norm_ffn2_w, ffn2_w_gate, ffn2_w_up, ffn2_w_down)
        mods = _mods(jnp.concatenate([c_prompt, c_sample], axis=0), w_ada[l], b_ada[l])
        mods_p = mods[:batch].reshape(batch, N_MODS, 1, D_MODEL)
        mods_s = mods[batch:].reshape(db, N_MODS, D_MODEL).transpose(1, 0, 2)[None]

        tiles_per_seq = seq // tm_p
        h1, qa, kt, vt, o_ret, st_p = _ffn_proj(
            hp, mods_p, tabs_p, lw, tm=tm_p, mod_idx=lambda i: i // tiles_per_seq, tab_idx=lambda i: i % tiles_per_seq,
            kv_groups=batch, kv_idx=lambda i: (i // tiles_per_seq, 0, i % tiles_per_seq), ret_seqs=batch)
        q_norm2_max = jnp.max(jnp.square(q_norm_w[l])) * (LOG2_E * LOG2_E)
        oa = _moba_prompt(qa, kt, vt, q_norm2_max, batch, seq)
        hp = _out_ffn(h1, oa, o_ret, mods_p, lw, tm=tm_p, mod_idx=lambda i: i // tiles_per_seq)

        h1s, qas, kts, vts, qrs, krs, vrs, grs = _ffn_proj(
            hs, mods_s, tabs_s, lw, tm=step_len * db, mod_idx=lambda i: 0, tab_idx=lambda i: 0,
            kv_groups=step_len, kv_idx=lambda i: (0, 0, 0), kv_tiles=step_len)

        def seq_major(t):
            t = t.reshape(step_len, db, ATTN_WIDTH).transpose(1, 0, 2)
            return jnp.pad(t, ((0, 0), (0, tp - step_len), (0, 0)))

        def seq_major_t(t):
            return jnp.pad(t.transpose(2, 0, 1), ((0, 0), (0, tp - step_len), (0, 0)))

        def token_major(t):
            return t[:, :step_len].transpose(1, 0, 2).reshape(step_len * db, ATTN_WIDTH)

        cache_kt = cache_k[l].transpose(0, 2, 3, 1)
        cache_vt = cache_v[l].transpose(0, 2, 3, 1)
        oas = _moba_sample(seq_major(qas), seq_major_t(kts), seq_major_t(vts), cache_kt, cache_vt, page_table, step_len,
                           n_seq=2)
        o_rets, st_s = _ret_sample(seq_major(qrs), seq_major(krs), seq_major(vrs), seq_major(grs), state_ret[l],
                                   step_len, ns=SUBLANES)
        hs = _out_ffn(h1s, token_major(oas), token_major(o_rets), mods_s, lw, tm=step_len * db, mod_idx=lambda i: 0)

        heads = (ATTN_HEADS, ATTN_HEAD_DIM)
        outs[0].append(kt.reshape(batch, *heads, seq).transpose(0, 3, 1, 2))
        outs[1].append(vt.reshape(batch, *heads, seq).transpose(0, 3, 1, 2))
        outs[2].append(kts.reshape(step_len, *heads, db).transpose(3, 0, 1, 2))
        outs[3].append(vts.reshape(step_len, *heads, db).transpose(3, 0, 1, 2))
        outs[4].append(st_p)
        outs[5].append(st_s)

    y_prompt = hp.reshape(batch, seq, D_MODEL)
    y_sample = hs.reshape(step_len, db, D_MODEL).transpose(1, 0, 2)
    return (y_prompt, y_sample) + tuple(jnp.stack(o) for o in outs)
```

```python
import functools

import jax
import jax.numpy as jnp
import numpy as np
from jax import lax
from jax.experimental import pallas as pl
from jax.experimental.pallas import tpu as pltpu

F32 = jnp.float32
BF16 = jnp.bfloat16

D_MODEL = 1024
ATTN_HEADS = 8
ATTN_HEAD_DIM = 64
ATTN_WIDTH = ATTN_HEADS * ATTN_HEAD_DIM
RET_HEADS = 4
RET_HEAD_DIM = 128
RET_WIDTH = RET_HEADS * RET_HEAD_DIM
IN_WIDTH = 3 * ATTN_WIDTH + 4 * RET_WIDTH
MOBA_BLOCK = 256
MOBA_TOP_K = 3
ROPE_THETA = 500000.0
ROPE_DIMS = ATTN_HEAD_DIM // 4
RET_CHUNK = 128
RET_ROT_BASE = 10000.0
N_MODS = 9
EPS = 1e-6
NEG_INF = -1e30

LANES = 128
SUBLANES = 8
VMEM_LIMIT = 56 * 1024 * 1024
MXU_TILE = 256
FF_CHUNK = MXU_TILE
MOBA_GROUP = 4
LOG2_E = 1.4426950408889634
SHIFT_LANE = 63
SHIFT_LIMIT = 60.0


def _dot(a, b):
    return jnp.dot(a, b, preferred_element_type=F32)


def _dot_nt(a, b):
    return lax.dot_general(a, b, (((1,), (1,)), ((), ())), preferred_element_type=F32)


def _split(a):
    hi = a.astype(BF16)
    lo = (a - hi.astype(F32)).astype(BF16)
    return hi, lo


def _dot3(a, b):
    ah, al = _split(a)
    bh, bl = _split(b)
    return _dot(ah, bh) + (_dot(ah, bl) + _dot(al, bh))


def _dot3_nt(a, b):
    ah, al = _split(a)
    bh, bl = _split(b)
    return _dot_nt(ah, bh) + (_dot_nt(ah, bl) + _dot_nt(al, bh))


def _sigmoid(x):
    return 1.0 / (1.0 + jnp.exp(-x))


def _silu(x):
    return x * _sigmoid(x)


def _rms(x):
    return x * lax.rsqrt(jnp.mean(x * x, axis=-1, keepdims=True) + EPS)


def _top_k_lanes(g, lane_f, k):
    sel = jnp.zeros(g.shape, jnp.bool_)
    for _ in range(k):
        m = jnp.max(g, axis=-1, keepdims=True)
        idx = jnp.min(jnp.where(g == m, lane_f, 1e9), axis=-1, keepdims=True)
        pick = lane_f == idx
        sel = jnp.logical_or(sel, pick)
        g = jnp.where(pick, -jnp.inf, g)
    return sel


def _fold_rows(op, s):
    while s.shape[0] > SUBLANES:
        half = s.shape[0] // 2
        s = op(s[:half], s[half:])
    return s


def _col_max(s):
    return jnp.max(_fold_rows(jnp.maximum, s), axis=0, keepdims=True)


def _col_min(s):
    return jnp.min(_fold_rows(jnp.minimum, s), axis=0, keepdims=True)


def _top_k_rows(g, row_f, k):
    sel = jnp.zeros(g.shape, jnp.bool_)
    for _ in range(k):
        m = _col_max(g)
        idx = _col_min(jnp.where(g == m, row_f, 1e9))
        pick = row_f == idx
        sel = jnp.logical_or(sel, pick)
        g = jnp.where(pick, -jnp.inf, g)
    return sel


def _mods_kernel(c_ref, w_ref, b_ref, o_ref):
    s = _silu(c_ref[...]).astype(BF16)
    o_ref[...] = _dot(s, w_ref[...].astype(BF16)) + b_ref[...]


def _mods(c, w_ada, b_ada):
    n = c.shape[0]
    n_pad = -(-n // SUBLANES) * SUBLANES
    c = jnp.pad(c, ((0, n_pad - n), (0, 0)))
    width = w_ada.shape[1]
    tn = 9 * LANES
    out = pl.pallas_call(
        _mods_kernel,
        grid=(width // tn,),
        in_specs=[pl.BlockSpec((n_pad, D_MODEL), lambda j: (0, 0)),
                  pl.BlockSpec((D_MODEL, tn), lambda j: (0, j)),
                  pl.BlockSpec((1, tn), lambda j: (0, j))],
        out_specs=pl.BlockSpec((n_pad, tn), lambda j: (0, j)),
        out_shape=jax.ShapeDtypeStruct((n_pad, width), F32),
        name="mods",
    )(c, w_ada, b_ada.reshape(1, width))
    return out[:n]


def _swiglu_acc(xb, wg_ref, wu_ref, wd_ref):
    d_ff = wg_ref.shape[1]
    acc = None
    for c in range(d_ff // FF_CHUNK):
        sl = slice(c * FF_CHUNK, (c + 1) * FF_CHUNK)
        g = _dot(xb, wg_ref[:, sl])
        u = _dot(xb, wu_ref[:, sl])
        part = _dot((_silu(g) * u).astype(BF16), wd_ref[sl, :])
        acc = part if acc is None else acc + part
    return acc


def _mod_rows(mod_ref, i, rows):
    m = mod_ref[0, i]
    reps = rows // m.shape[0]
    return m if m.shape[0] == 1 or reps == 1 else jnp.concatenate([m] * reps, axis=0)


def _const_spec(shape):
    nd = len(shape)
    return pl.BlockSpec(shape, lambda *_: (0,) * nd, pipeline_mode=pl.Buffered(1))


def _head_norm_gate(out, g):
    on = out * lax.rsqrt(jnp.mean(out * out, axis=-1, keepdims=True) + EPS)
    return on * _silu(g)


def _ret_chunks(q_ref, k_ref, v_ref, g_ref, dec_ref, qd_ref, kd_ref, cd_ref, o_ref, s_ref, fresh):
    c_len = RET_CHUNK
    n_chunks = q_ref.shape[0] // c_len
    pairs = [(c, h) for c in range(n_chunks) for h in range(RET_HEADS)]
    rows = lambda c: slice(c * c_len, (c + 1) * c_len)
    cols = lambda h: slice(h * LANES, (h + 1) * LANES)
    scores, updates = {}, {}
    for c, h in pairs:
        q, k, vb = q_ref[rows(c), cols(h)], k_ref[rows(c), cols(h)], v_ref[rows(c), cols(h)].astype(BF16)
        scores[c, h] = (_dot_nt(q.astype(BF16), k.astype(BF16)) * dec_ref[h]).astype(BF16)
        updates[c, h] = _dot((k * kd_ref[:, cols(h)]).T.astype(BF16), vb)
    states = {}
    for h in range(RET_HEADS):
        st = jnp.where(fresh, 0.0, s_ref[h])
        for c in range(n_chunks):
            states[c, h] = st
            st = cd_ref[h] * st + updates[c, h]
        s_ref[h] = st
    for c, h in pairs:
        q, vb = q_ref[rows(c), cols(h)], v_ref[rows(c), cols(h)].astype(BF16)
        out = _dot(scores[c, h], vb) + _dot((q * qd_ref[:, cols(h)]).astype(BF16), states[c, h].astype(BF16))
        o_ref[rows(c), cols(h)] = _head_norm_gate(out, g_ref[rows(c), cols(h)])


def _ret_tables(chunk_len):
    lg = np.log(1.0 - 2.0 ** (-5.0 - np.arange(RET_HEADS)))
    i = np.arange(RET_CHUNK, dtype=np.float64)
    diff = i[:, None] - i[None, :]
    decay = np.where(diff >= 0, np.exp(np.maximum(diff, 0.0)[None] * lg[:, None, None]), 0.0)
    qd = np.exp((i + 1.0)[None] * lg[:, None])
    kd = np.exp((chunk_len - 1.0 - i)[None] * lg[:, None])
    widen = lambda t: np.repeat(t.T, RET_HEAD_DIM, axis=1)
    cd = np.broadcast_to(np.exp(chunk_len * lg)[:, None, None], (RET_HEADS, 1, LANES))
    return tuple(jnp.asarray(t, F32) for t in (decay, widen(qd), widen(kd), cd))


def _ffn_proj_kernel(x_ref, mod_ref, n1_ref, wg_ref, wu_ref, wd_ref, nm_ref, win_ref, qn_ref, kn_ref, bd_ref,
                     ca_ref, sa1_ref, sa2_ref, cr_ref, sr1_ref, sr2_ref, *refs, tiles_per_seq):
    if tiles_per_seq is None:
        h_ref, ka_ref, va_ref, packed_ref = refs
        qa_ref, ka_row_ref, va_row_ref, qr_ref, kr_ref, vr_ref, gr_ref = (
            packed_ref.at[:, pl.ds(j * ATTN_WIDTH, ATTN_WIDTH)] for j in range(IN_WIDTH // ATTN_WIDTH))
    else:
        ka_row_ref = va_row_ref = None
        ret_tabs, (h_ref, qa_ref, ka_ref, va_ref, oret_ref, st_ref), (qr_ref, kr_ref, vr_ref, gr_ref, s_ref) = (
            refs[:4], refs[4:10], refs[10:])

        @pl.when(pl.program_id(0) == 0)
        def _init_state():
            s_ref[...] = jnp.zeros(s_ref.shape, F32)
    x = x_ref[...]
    mod = lambda i: _mod_rows(mod_ref, i, x.shape[0])
    xn = (_rms(x) * n1_ref[...]) * (1.0 + mod(1)) + mod(0)
    acc = _swiglu_acc(xn.astype(BF16), wg_ref, wu_ref, wd_ref)
    h = x + 0.5 * mod(2) * acc
    h_ref[...] = h
    hn = (_rms(h) * nm_ref[...]) * (1.0 + mod(4)) + mod(3)
    hb = hn.astype(BF16)

    def seg(i):
        return _dot(hb, win_ref[:, i * ATTN_WIDTH:(i + 1) * ATTN_WIDTH])

    ca, sa1, sa2 = ca_ref[0], sa1_ref[0], sa2_ref[0]
    cr, sr1, sr2 = cr_ref[0], sr1_ref[0], sr2_ref[0]
    bd = bd_ref[...]

    def store_t(o_ref, g, r):
        width = o_ref.shape[2]
        for t in range(o_ref.shape[0]):
            o_ref[t, g * LANES:(g + 1) * LANES, :] = r[t * width:(t + 1) * width, :].T

    def attn_head_norm_rope(p, w, row_ref, t_ref):
        hi, lo = _split(p * p)
        ms = jnp.concatenate([_dot(hi[:, c:c + MXU_TILE], bd) + _dot(lo[:, c:c + MXU_TILE], bd)
                              for c in range(0, ATTN_WIDTH, MXU_TILE)], axis=1)
        pn = (p * lax.rsqrt(ms + EPS)) * w
        for g in range(ATTN_WIDTH // LANES):
            xg = pn[:, g * LANES:(g + 1) * LANES]
            r = xg * ca + pltpu.roll(xg, LANES - ROPE_DIMS // 2, 1) * sa1 + pltpu.roll(xg, ROPE_DIMS // 2, 1) * sa2
            if t_ref is not None:
                store_t(t_ref, g, r)
            if row_ref is not None:
                row_ref[:, g * LANES:(g + 1) * LANES] = r

    def ret_rotate(p, o_ref, scale):
        for g in range(RET_HEADS):
            xg = p[:, g * LANES:(g + 1) * LANES]
            r = xg * cr + pltpu.roll(xg, LANES - 1, 1) * sr1 + pltpu.roll(xg, 1, 1) * sr2
            o_ref[:, g * LANES:(g + 1) * LANES] = r if scale is None else r * scale

    attn_head_norm_rope(seg(0), qn_ref[...], qa_ref, None)
    attn_head_norm_rope(seg(1), kn_ref[...], ka_row_ref, ka_ref)
    va = seg(2)
    for g in range(ATTN_WIDTH // LANES):
        store_t(va_ref, g, va[:, g * LANES:(g + 1) * LANES])
    if va_row_ref is not None:
        va_row_ref[...] = va
    ret_rotate(seg(3), qr_ref, None)
    ret_rotate(seg(4), kr_ref, RET_HEAD_DIM ** -0.5)
    vr_ref[...] = seg(5)
    gr_ref[...] = seg(6)
    if tiles_per_seq is not None:
        _ret_chunks(qr_ref, kr_ref, vr_ref, gr_ref, *ret_tabs, oret_ref, s_ref, pl.program_id(0) % tiles_per_seq == 0)
        st_ref[0] = s_ref[...]


def _ffn_proj(x2d, mods, tabs, lw, *, tm, mod_idx, tab_idx, kv_groups, kv_idx, kv_tiles=1, ret_seqs=None):
    n = x2d.shape[0]
    rm = mods.shape[2]
    rt = tabs[0].shape[1]
    d_ff = lw["wg1"].shape[1]
    row = lambda i: (i, 0)
    tab_spec = pl.BlockSpec((1, rt, LANES), lambda i: (tab_idx(i), 0, 0))
    in_specs = [
        pl.BlockSpec((tm, D_MODEL), row),
        pl.BlockSpec((1, N_MODS, rm, D_MODEL), lambda i: (mod_idx(i), 0, 0, 0)),
        _const_spec((1, D_MODEL)),
        _const_spec((D_MODEL, d_ff)), _const_spec((D_MODEL, d_ff)), _const_spec((d_ff, D_MODEL)),
        _const_spec((1, D_MODEL)),
        _const_spec((D_MODEL, IN_WIDTH)),
        _const_spec((1, ATTN_WIDTH)), _const_spec((1, ATTN_WIDTH)),
        _const_spec((MXU_TILE, MXU_TILE)),
    ] + [tab_spec] * 6
    half = jax.ShapeDtypeStruct((n, ATTN_WIDTH), F32)
    half_t = jax.ShapeDtypeStruct((kv_groups, ATTN_WIDTH, n // kv_groups), F32)
    half_spec = pl.BlockSpec((tm, ATTN_WIDTH), row)
    half_t_spec = pl.BlockSpec((kv_tiles, ATTN_WIDTH, tm // kv_tiles), lambda i: kv_idx(i))
    out_shape = [jax.ShapeDtypeStruct((n, D_MODEL), F32), half, half_t, half_t]
    out_specs = [pl.BlockSpec((tm, D_MODEL), row), half_spec, half_t_spec, half_t_spec]
    operands = [x2d, mods, lw["n1"], lw["wg1"], lw["wu1"], lw["wd1"], lw["nm"], lw["win"], lw["qn"], lw["kn"], lw["bd"], *tabs]
    if ret_seqs is None:
        tiles_per_seq, scratch = None, []
        out_shape = [out_shape[0], half_t, half_t, jax.ShapeDtypeStruct((n, IN_WIDTH), F32)]
        out_specs = [out_specs[0], half_t_spec, half_t_spec, pl.BlockSpec((tm, IN_WIDTH), row)]
    else:
        tiles_per_seq = n // ret_seqs // tm
        state = (RET_HEADS, RET_HEAD_DIM, RET_HEAD_DIM)
        operands += list(_ret_tables(RET_CHUNK))
        in_specs += [_const_spec((RET_HEADS, RET_CHUNK, RET_CHUNK)), _const_spec((RET_CHUNK, RET_WIDTH)),
                     _const_spec((RET_CHUNK, RET_WIDTH)), _const_spec((RET_HEADS, 1, LANES))]
        out_shape += [half, jax.ShapeDtypeStruct((ret_seqs,) + state, F32)]
        out_specs += [half_spec, pl.BlockSpec((1,) + state, lambda i: (i // tiles_per_seq, 0, 0, 0))]
        scratch = [pltpu.VMEM((tm, RET_WIDTH), F32)] * 4 + [pltpu.VMEM(state, F32)]
    return pl.pallas_call(
        functools.partial(_ffn_proj_kernel, tiles_per_seq=tiles_per_seq),
        grid=(n // tm,),
        in_specs=in_specs,
        out_specs=out_specs,
        out_shape=out_shape,
        scratch_shapes=scratch,
        compiler_params=pltpu.CompilerParams(dimension_semantics=("arbitrary",), vmem_limit_bytes=VMEM_LIMIT),
        name="ffn_proj",
    )(*operands)


def _out_ffn_kernel(h_ref, oa_ref, or_ref, mod_ref, wo_ref, n2_ref, wg_ref, wu_ref, wd_ref, y_ref):
    mix = _dot(oa_ref[...].astype(BF16), wo_ref[:ATTN_WIDTH, :]) + _dot(or_ref[...].astype(BF16), wo_ref[ATTN_WIDTH:, :])
    mod = lambda i: _mod_rows(mod_ref, i, h_ref.shape[0])
    h = h_ref[...] + mod(5) * mix
    hn = (_rms(h) * n2_ref[...]) * (1.0 + mod(7)) + mod(6)
    acc = _swiglu_acc(hn.astype(BF16), wg_ref, wu_ref, wd_ref)
    y_ref[...] = h + 0.5 * mod(8) * acc


def _out_ffn(h2d, oa, o_ret, mods, lw, *, tm, mod_idx):
    n = h2d.shape[0]
    rm = mods.shape[2]
    d_ff = lw["wg2"].shape[1]
    row = lambda i: (i, 0)
    return pl.pallas_call(
        _out_ffn_kernel,
        grid=(n // tm,),
        in_specs=[
            pl.BlockSpec((tm, D_MODEL), row),
            pl.BlockSpec((tm, ATTN_WIDTH), row),
            pl.BlockSpec((tm, RET_WIDTH), row),
            pl.BlockSpec((1, N_MODS, rm, D_MODEL), lambda i: (mod_idx(i), 0, 0, 0)),
            _const_spec((D_MODEL, D_MODEL)),
            _const_spec((1, D_MODEL)),
            _const_spec((D_MODEL, d_ff)), _const_spec((D_MODEL, d_ff)), _const_spec((d_ff, D_MODEL)),
        ],
        out_specs=pl.BlockSpec((tm, D_MODEL), row),
        out_shape=jax.ShapeDtypeStruct((n, D_MODEL), F32),
        compiler_params=pltpu.CompilerParams(dimension_semantics=("arbitrary",), vmem_limit_bytes=VMEM_LIMIT),
        name="out_ffn",
    )(h2d, oa, o_ret, mods, lw["wo"], lw["n2"], lw["wg2"], lw["wu2"], lw["wd2"])


def _moba_prompt_kernel(q_ref, qn_ref, kt_ref, vt_in_ref, qn2_ref, o_ref,
                        kaug_ref, vt_ref, km_ref, bound_ref, flag_ref, acc_ref, shift_ref, bias_ref, *, nb):
    qi = pl.program_id(2)
    tq = MOBA_BLOCK
    hd = ATTN_HEAD_DIM
    grp = MOBA_GROUP
    lane = lax.broadcasted_iota(jnp.int32, (tq, LANES), 1)
    in_head = (lane < hd, lane >= hd)
    field_off = (hd, 0)

    @pl.when(qi == 0)
    def _prepare():
        km_ref[...] = jnp.zeros(km_ref.shape, F32)
        feat = lax.broadcasted_iota(jnp.int32, (LANES, tq), 0)
        feat_in_head = (feat < hd, feat >= hd)
        kn2 = [jnp.zeros((1, 1), F32)] * 2
        for j in range(nb):
            kb = kt_ref[0, :, j * tq:(j + 1) * tq].T
            vtb = vt_in_ref[0, :, j * tq:(j + 1) * tq]
            km_ref[hd + j:hd + j + 1, :] = jnp.mean(kb, axis=0, keepdims=True)
            sq = kb * kb
            for h in range(2):
                norm2 = jnp.sum(jnp.where(in_head[h], sq, 0.0), axis=1, keepdims=True)
                kn2[h] = jnp.maximum(kn2[h], jnp.max(norm2, axis=0, keepdims=True))
                off = field_off[h]
                field = jnp.where(jnp.logical_or(lane == off + j, lane == off + SHIFT_LANE), 1.0, 0.0)
                kaug_ref[h, j * tq:(j + 1) * tq, :] = jnp.where(in_head[h], kb, field).astype(BF16)
                vt_ref[h, j] = jnp.where(feat_in_head[h], vtb, 1.0).astype(BF16)
        bound = [jnp.sqrt(qn2_ref[0:1, 0:1] * kn2[h]) for h in range(2)]
        for h in range(2):
            bound_ref[h] = jnp.broadcast_to(bound[h], bound_ref.shape[1:])
        flag_ref[0] = (jnp.max(jnp.maximum(bound[0], bound[1])) <= SHIFT_LIMIT).astype(jnp.int32)

        bias_ref[...] = jnp.full(bias_ref.shape, NEG_INF, F32)

    q = q_ref[...]
    scale = hd ** -0.5 * LOG2_E
    qs = [jnp.where(in_head[h], q, 0.0) * scale for h in range(2)]

    field_row = lax.broadcasted_iota(jnp.int32, (LANES, tq), 0)
    field_row_f = field_row.astype(F32)

    def block_bias(q_tile, h, n_past):
        off = field_off[h]
        q_head = jnp.where(in_head[h], q_tile, 0.0)
        gate_t = _dot3_nt(km_ref[hd - off:hd - off + LANES, :], q_head)
        valid = jnp.logical_and(field_row >= off, field_row < off + n_past)
        sel = _top_k_rows(jnp.where(valid, gate_t, NEG_INF), field_row_f, MOBA_TOP_K)
        return jnp.where(jnp.logical_and(sel, valid), 0.0, NEG_INF).T

    key_i = lax.broadcasted_iota(jnp.int32, (tq, tq), 0)
    qry_i = lax.broadcasted_iota(jnp.int32, (tq, tq), 1)
    causal = key_i <= qry_i
    own_rows = pl.ds(pl.multiple_of(qi * tq, tq), tq)
    n_groups = (qi + grp - 1) // grp

    def group_rows(g):
        return pl.ds(pl.multiple_of(g * (grp * tq), grp * tq), grp * tq)

    bounded = flag_ref[0] == 1

    @pl.when(bounded)
    def _bound_shift():
        for h in range(2):
            shift_ref[h] = jnp.broadcast_to(bound_ref[h, 0:1, :], (tq, LANES))

    @pl.when(jnp.logical_not(bounded))
    def _exact_shift():
        for h in range(2):
            q_sel = jnp.where(in_head[h], qs[h], bias_ref[h]).astype(BF16)
            m = _col_max(jnp.where(causal, _dot_nt(kaug_ref[h, own_rows, :], qs[h].astype(BF16)), NEG_INF))
            m = lax.fori_loop(
                0, n_groups, lambda g, m: jnp.maximum(m, _col_max(_dot_nt(kaug_ref[h, group_rows(g), :], q_sel))), m)
            shift_ref[h] = jnp.broadcast_to(m, (LANES, tq)).T

    q_own, q_past = [], []
    for h in range(2):
        at_shift = lane == field_off[h] + SHIFT_LANE
        q_own.append(jnp.where(in_head[h], qs[h], jnp.where(at_shift, -shift_ref[h], 0.0)))
        q_past.append(jnp.where(jnp.logical_or(in_head[h], at_shift), q_own[h], bias_ref[h]).astype(BF16))
    acc_ref[...] = jnp.zeros(acc_ref.shape, F32)

    def add_groups(groups, with_own=False):
        scores = [[_dot_nt(kaug_ref[h, group_rows(g), :], q_past[h]) for h in range(2)] for g in groups]
        if with_own:
            own_scores = [_dot_nt(kaug_ref[h, own_rows, :], q_own[h].astype(BF16)) for h in range(2)]
            q_next = qn_ref[...]
            for h in range(2):
                bias_ref[h] = block_bias(q_next, h, qi + 1)
            for h in range(2):
                s = jnp.where(causal, own_scores[h], NEG_INF)
                acc_ref[h] += _dot(vt_ref[h, qi], jnp.exp2(s).astype(BF16))
        for g, sc in zip(groups, scores):
            for h in range(2):
                p = jnp.exp2(sc[h]).astype(BF16)
                pv = None
                for c in range(grp):
                    part = _dot(vt_ref[h, g * grp + c], p[c * tq:(c + 1) * tq, :])
                    pv = part if pv is None else pv + part
                acc_ref[h] += pv

    def pair_body(t, carry):
        add_groups([2 * t, 2 * t + 1])
        return carry

    has_pair = n_groups >= 2
    odd = n_groups % 2 == 1

    @pl.when(has_pair)
    def _first_pair():
        add_groups([0, 1], with_own=True)

    lax.fori_loop(1, n_groups // 2, pair_body, 0)

    @pl.when(jnp.logical_and(has_pair, odd))
    def _last_group():
        add_groups([n_groups - 1])

    @pl.when(n_groups == 1)
    def _only_group():
        add_groups([0], with_own=True)

    @pl.when(n_groups == 0)
    def _no_group():
        add_groups([], with_own=True)

    a0, a1 = acc_ref[0], acc_ref[1]
    o_t = jnp.concatenate([a0[:hd] * (1.0 / a0[hd:hd + 1]), a1[hd:] * (1.0 / a1[0:1])], axis=0)
    o_ref[...] = o_t.T


def _moba_prompt(qa, kt, vt, q_norm2_max, batch, seq):
    tq = MOBA_BLOCK
    assert seq % (tq * MOBA_GROUP) == 0
    nb = seq // tq
    assert nb <= SHIFT_LANE, "side field holds one lane per key block below the shift lane"
    n_pairs = ATTN_WIDTH // LANES
    kv_spec = pl.BlockSpec((1, LANES, seq), lambda b, hp, qi: (b, hp, 0))
    q_spec = pl.BlockSpec((tq, LANES), lambda b, hp, qi: (b * nb + qi, hp))
    q_next_spec = pl.BlockSpec((tq, LANES), lambda b, hp, qi: (b * nb + jnp.minimum(qi + 1, nb - 1), hp))
    return pl.pallas_call(
        functools.partial(_moba_prompt_kernel, nb=nb),
        grid=(batch, n_pairs, nb),
        in_specs=[q_spec, q_next_spec, kv_spec, kv_spec, _const_spec((1, LANES))],
        out_specs=q_spec,
        out_shape=jax.ShapeDtypeStruct(qa.shape, F32),
        scratch_shapes=[
            pltpu.VMEM((2, seq, LANES), BF16),
            pltpu.VMEM((2, nb, LANES, tq), BF16),
            pltpu.VMEM((ATTN_HEAD_DIM + LANES, LANES), F32),
            pltpu.VMEM((2, SUBLANES, LANES), F32),
            pltpu.SMEM((1,), jnp.int32),
            pltpu.VMEM((2, LANES, tq), F32),
            pltpu.VMEM((2, tq, LANES), F32),
            pltpu.VMEM((2, tq, LANES), F32),
        ],
        compiler_params=pltpu.CompilerParams(dimension_semantics=("arbitrary", "arbitrary", "arbitrary"),
                                             vmem_limit_bytes=VMEM_LIMIT),
        name="moba_prompt",
    )(qa, qa, kt, vt, jnp.broadcast_to(q_norm2_max, (1, LANES)).astype(F32))


def _ret_sample_kernel(q_ref, k_ref, v_ref, g_ref, st_ref, dec_ref, qd_ref, kd_ref, cd_ref, o_ref, ns_ref):
    tp = q_ref.shape[1]
    zeros = jnp.zeros((RET_CHUNK - tp, LANES), F32)
    pairs = [(s, h) for s in range(q_ref.shape[0]) for h in range(RET_HEADS)]
    cols = lambda h: slice(h * LANES, (h + 1) * LANES)
    scores, values = {}, {}
    for s, h in pairs:
        k = jnp.concatenate([k_ref[s, :, cols(h)], zeros], axis=0)
        vb = jnp.concatenate([v_ref[s, :, cols(h)], zeros], axis=0).astype(BF16)
        scores[s, h] = (_dot_nt(q_ref[s, :, cols(h)].astype(BF16), k.astype(BF16)) * dec_ref[h, :tp, :]).astype(BF16)
        ns_ref[s, h] = cd_ref[h] * st_ref[s, h] + _dot((k * kd_ref[:, cols(h)]).T.astype(BF16), vb)
        values[s, h] = vb
    for s, h in pairs:
        q_dec = (q_ref[s, :, cols(h)] * qd_ref[:tp, cols(h)]).astype(BF16)
        out = _dot(scores[s, h], values[s, h]) + _dot(q_dec, st_ref[s, h].astype(BF16))
        o_ref[s, :, cols(h)] = _head_norm_gate(out, g_ref[s, :, cols(h)])


def _ret_sample(packed, cols, state, step_len, *, ns):
    db, tp, _ = packed.shape
    tok = pl.BlockSpec((ns, tp, RET_WIDTH), lambda i: (i, 0, 0))
    tok_in = [pl.BlockSpec((ns, tp, RET_WIDTH), functools.partial(lambda i, c: (i, 0, c), c=c)) for c in cols]
    st = pl.BlockSpec((ns, RET_HEADS, RET_HEAD_DIM, RET_HEAD_DIM), lambda i: (i, 0, 0, 0))
    decay, qd, kd, cd = _ret_tables(step_len)
    return pl.pallas_call(
        _ret_sample_kernel,
        grid=(db // ns,),
        in_specs=tok_in + [st,
                           _const_spec((RET_HEADS, RET_CHUNK, RET_CHUNK)),
                           _const_spec((RET_CHUNK, RET_WIDTH)), _const_spec((RET_CHUNK, RET_WIDTH)),
                           _const_spec((RET_HEADS, 1, LANES))],
        out_specs=[tok, st],
        out_shape=[jax.ShapeDtypeStruct((db, tp, RET_WIDTH), F32), jax.ShapeDtypeStruct(state.shape, F32)],
        compiler_params=pltpu.CompilerParams(dimension_semantics=("arbitrary",)),
        name="ret_sample",
    )(packed, packed, packed, packed, state, decay, qd, kd, cd)


def _moba_sample_seq(q, kn, vn, k_refs, v_refs, *, page, step_len):
    n_pages = len(k_refs)
    hd = ATTN_HEAD_DIM
    ppb = MOBA_BLOCK // page
    n_full = n_pages // ppb
    rows = step_len * ATTN_HEADS
    qrep = jnp.concatenate([jnp.broadcast_to(q[t:t + 1, :], (ATTN_HEADS, ATTN_WIDTH)) for t in range(step_len)], axis=0)
    r_i = lax.broadcasted_iota(jnp.int32, (rows, ATTN_WIDTH), 0)
    c_i = lax.broadcasted_iota(jnp.int32, (rows, ATTN_WIDTH), 1)
    own_head = (c_i // hd) == (r_i % ATTN_HEADS)
    qbd = jnp.where(own_head, qrep, 0.0)
    qs = qbd * (hd ** -0.5)

    vt = lambda p: v_refs[p][...].reshape(ATTN_WIDTH, page)
    lane = lax.broadcasted_iota(jnp.int32, (rows, LANES), 1)
    gate = jnp.full((rows, LANES), NEG_INF, F32)
    k_pages = []
    for b in range(n_full):
        ksum = None
        for p in range(b * ppb, (b + 1) * ppb):
            kt = k_refs[p][...].reshape(ATTN_WIDTH, page)
            k_pages.append(kt.astype(BF16))
            ksum = kt if ksum is None else ksum + kt
        g = jnp.sum(_dot3(qbd, ksum), axis=-1, keepdims=True) * (1.0 / MOBA_BLOCK)
        gate = jnp.where(lane == b, g, gate)
    s_past = _dot(qs.astype(BF16), jnp.concatenate(k_pages, axis=1))
    sel = jnp.logical_and(_top_k_lanes(gate, lane.astype(F32), min(MOBA_TOP_K, n_full)), lane < n_full)
    bias = jnp.where(sel, 0.0, NEG_INF)
    bias_cols = [jnp.max(jnp.where(lane == b, bias, NEG_INF), axis=-1, keepdims=True) for b in range(n_full)]
    s_past = s_past + jnp.concatenate([jnp.broadcast_to(c, (rows, MOBA_BLOCK)) for c in bias_cols], axis=1)

    tok_of_row = lax.broadcasted_iota(jnp.int32, (rows, 1), 0) // ATTN_HEADS
    s_own = []
    for t in range(step_len):
        s_t = jnp.sum(qs * kn[t:t + 1, :], axis=-1, keepdims=True)
        s_own.append(jnp.where(tok_of_row >= t, s_t, NEG_INF))

    m = jnp.maximum(functools.reduce(jnp.maximum, s_own), jnp.max(s_past, axis=-1, keepdims=True))
    e_past = jnp.exp(s_past - m)
    l = jnp.sum(e_past, axis=-1, keepdims=True)
    o = _dot_nt(e_past.astype(BF16), jnp.concatenate([vt(p).astype(BF16) for p in range(n_pages)], axis=1))
    for t in range(step_len):
        e = jnp.exp(s_own[t] - m)
        l = l + e
        o = o + e * vn[t:t + 1, :]
    o = jnp.where(own_head, o, 0.0) * (1.0 / l)
    return [jnp.sum(o[t * ATTN_HEADS:(t + 1) * ATTN_HEADS, :], axis=0, keepdims=True) for t in range(step_len)]


def _moba_sample_kernel(pt_ref, q_ref, kn_ref, vn_ref, ck_hbm, cv_hbm, o_ref, kbuf, vbuf, sems, *, n_pages, page, step_len):
    i = pl.program_id(0)
    n_seq = q_ref.shape[0]

    def page_copies(step, slot):
        copies = []
        for j in range(n_seq * n_pages):
            phys = pt_ref[step * n_seq + j // n_pages, j % n_pages]
            copies.append(pltpu.make_async_copy(ck_hbm.at[phys], kbuf.at[slot, j], sems.at[0, slot]))
            copies.append(pltpu.make_async_copy(cv_hbm.at[phys], vbuf.at[slot, j], sems.at[1, slot]))
        return copies

    @pl.when(i == 0)
    def _first():
        for c in page_copies(0, 0):
            c.start()

    @pl.when(i + 1 < pl.num_programs(0))
    def _prefetch():
        for c in page_copies(i + 1, (i + 1) % 2):
            c.start()

    slot = i % 2
    for c in page_copies(i, slot):
        c.wait()

    o_ref[...] = jnp.zeros(o_ref.shape, F32)
    for s in range(n_seq):
        k_refs = [kbuf.at[slot, s * n_pages + p] for p in range(n_pages)]
        v_refs = [vbuf.at[slot, s * n_pages + p] for p in range(n_pages)]
        out_rows = _moba_sample_seq(q_ref[s], kn_ref[s], vn_ref[s], k_refs, v_refs, page=page, step_len=step_len)
        for t, row in enumerate(out_rows):
            o_ref[s, t:t + 1, :] = row


def _moba_sample(packed, cols, cache_kt, cache_vt, page_table, step_len, *, n_seq):
    db, tp, _ = packed.shape
    n_pages = page_table.shape[1]
    page = cache_kt.shape[-1]
    assert (n_pages * page) % MOBA_BLOCK == 0, "past length must be whole key blocks"
    assert MOBA_BLOCK % page == 0 and db % n_seq == 0
    tok = pl.BlockSpec((n_seq, tp, ATTN_WIDTH), lambda b, pt: (b, 0, 0))
    tok_in = [pl.BlockSpec((n_seq, tp, ATTN_WIDTH), functools.partial(lambda b, pt, c: (b, 0, c), c=c)) for c in cols]
    hbm = pl.BlockSpec(memory_space=pl.ANY)
    page_buf = pltpu.VMEM((2, n_seq * n_pages) + cache_kt.shape[1:], F32)
    grid_spec = pltpu.PrefetchScalarGridSpec(
        num_scalar_prefetch=1,
        grid=(db // n_seq,),
        in_specs=tok_in + [hbm, hbm],
        out_specs=tok,
        scratch_shapes=[page_buf, page_buf, pltpu.SemaphoreType.DMA((2, 2))],
    )
    return pl.pallas_call(
        functools.partial(_moba_sample_kernel, n_pages=n_pages, page=page, step_len=step_len),
        grid_spec=grid_spec,
        out_shape=jax.ShapeDtypeStruct((db, tp, ATTN_WIDTH), F32),
        compiler_params=pltpu.CompilerParams(dimension_semantics=("arbitrary",), vmem_limit_bytes=VMEM_LIMIT),
        name="moba_sample",
    )(page_table, packed, packed, packed, cache_kt, cache_vt)


def _rope_tables(pos):
    half = ROPE_DIMS // 2
    j = np.arange(LANES) % ATTN_HEAD_DIM
    inv = np.where(j < ROPE_DIMS, ROPE_THETA ** (-(j % half) * 2.0 / ROPE_DIMS), 0.0).astype(np.float32)
    first = (j < half).astype(np.float32)
    second = np.logical_and(j >= half, j < ROPE_DIMS).astype(np.float32)
    ang = pos.astype(F32)[:, None] * inv[None, :]
    sin = jnp.sin(ang)
    return jnp.cos(ang), sin * -first, sin * second


def _ret_rot_tables(pos):
    lane = np.arange(LANES)
    inv = (1.0 / RET_ROT_BASE ** np.linspace(0.0, 1.0, RET_HEAD_DIM // 2))[lane // 2].astype(np.float32)
    even = (lane % 2 == 0).astype(np.float32)
    ang = pos.astype(F32)[:, None] * inv[None, :]
    sin = jnp.sin(ang)
    return jnp.cos(ang), sin * -even, sin * (1.0 - even)


def _layer_weights(l, norm_ffn1_w, ffn1_w_gate, ffn1_w_up, ffn1_w_down, norm_mix_w, w_in, q_norm_w, k_norm_w, w_out,
                   norm_ffn2_w, ffn2_w_gate, ffn2_w_up, ffn2_w_down):
    head = np.arange(MXU_TILE) // ATTN_HEAD_DIM
    return {
        "n1": norm_ffn1_w[l][None], "nm": norm_mix_w[l][None], "n2": norm_ffn2_w[l][None],
        "wg1": ffn1_w_gate[l].astype(BF16), "wu1": ffn1_w_up[l].astype(BF16), "wd1": ffn1_w_down[l].astype(BF16),
        "wg2": ffn2_w_gate[l].astype(BF16), "wu2": ffn2_w_up[l].astype(BF16), "wd2": ffn2_w_down[l].astype(BF16),
        "win": w_in[l].astype(BF16), "wo": w_out[l].astype(BF16),
        "qn": jnp.tile(q_norm_w[l], ATTN_HEADS)[None], "kn": jnp.tile(k_norm_w[l], ATTN_HEADS)[None],
        "bd": jnp.asarray(np.where(head[:, None] == head[None, :], 1.0 / ATTN_HEAD_DIM, 0.0), BF16),
    }


def kernel(x_prompt, x_sample, cache_k, cache_v, state_ret, page_table, c_prompt, c_sample, w_ada, b_ada, norm_ffn1_w, ffn1_w_gate, ffn1_w_up, ffn1_w_down, norm_mix_w, w_in, q_norm_w, k_norm_w, w_out, norm_ffn2_w, ffn2_w_gate, ffn2_w_up, ffn2_w_down):
    batch, seq, _ = x_prompt.shape
    db, step_len, _ = x_sample.shape
    depth = w_ada.shape[0]
    n_pages, page = page_table.shape[1], cache_k.shape[2]
    past_len = n_pages * page
    tm_p = 512
    tp = SUBLANES
    assert seq % tm_p == 0 and step_len <= tp and db % SUBLANES == 0

    tabs_p = [t.reshape(seq // tm_p, tm_p, LANES)
              for t in _rope_tables(jnp.arange(seq)) + _ret_rot_tables(jnp.arange(seq))]
    pos_s = past_len + jnp.arange(step_len)
    tabs_s = [jnp.broadcast_to(t[:, None, :], (step_len, db, LANES)).reshape(1, step_len * db, LANES)
              for t in _rope_tables(pos_s) + _ret_rot_tables(pos_s)]

    hp = x_prompt.reshape(batch * seq, D_MODEL)
    hs = x_sample.transpose(1, 0, 2).reshape(step_len * db, D_MODEL)
    outs = [[] for _ in range(6)]
    for l in range(depth):
        lw = _layer_weights(l, norm_ffn1_w, ffn1_w_gate, ffn1_w_up, ffn1_w_down, norm_mix_w, w_in, q_norm_w, k_norm_w,
                            w_out, norm_ffn2_w, ffn2_w_gate, ffn2_w_up, ffn2_w_down)
        mods = _mods(jnp.concatenate([c_prompt, c_sample], axis=0), w_ada[l], b_ada[l])
        mods_p = mods[:batch].reshape(batch, N_MODS, 1, D_MODEL)
        mods_s = mods[batch:].reshape(db, N_MODS, D_MODEL).transpose(1, 0, 2)[None]

        tiles_per_seq = seq // tm_p
        h1, qa, kt, vt, o_ret, st_p = _ffn_proj(
            hp, mods_p, tabs_p, lw, tm=tm_p, mod_idx=lambda i: i // tiles_per_seq, tab_idx=lambda i: i % tiles_per_seq,
            kv_groups=batch, kv_idx=lambda i: (i // tiles_per_seq, 0, i % tiles_per_seq), ret_seqs=batch)
        q_norm2_max = jnp.max(jnp.square(q_norm_w[l])) * (LOG2_E * LOG2_E)
        oa = _moba_prompt(qa, kt, vt, q_norm2_max, batch, seq)
        hp = _out_ffn(h1, oa, o_ret, mods_p, lw, tm=tm_p, mod_idx=lambda i: i // tiles_per_seq)

        h1s, kts, vts, proj_s = _ffn_proj(
            hs, mods_s, tabs_s, lw, tm=step_len * db, mod_idx=lambda i: 0, tab_idx=lambda i: 0,
            kv_groups=step_len, kv_idx=lambda i: (0, 0, 0), kv_tiles=step_len)
        proj_s = jnp.pad(proj_s.reshape(step_len, db, IN_WIDTH).transpose(1, 0, 2), ((0, 0), (0, tp - step_len), (0, 0)))

        def token_major(t):
            return t[:, :step_len].transpose(1, 0, 2).reshape(step_len * db, ATTN_WIDTH)

        cache_kt = cache_k[l].transpose(0, 2, 3, 1)
        cache_vt = cache_v[l].transpose(0, 2, 3, 1)
        oas = _moba_sample(proj_s, (0, 1, 2), cache_kt, cache_vt, page_table, step_len, n_seq=2)
        o_rets, st_s = _ret_sample(proj_s, (3, 4, 5, 6), state_ret[l], step_len, ns=SUBLANES)
        hs = _out_ffn(h1s, token_major(oas), token_major(o_rets), mods_s, lw, tm=step_len * db, mod_idx=lambda i: 0)

        heads = (ATTN_HEADS, ATTN_HEAD_DIM)
        outs[0].append(kt.reshape(batch, *heads, seq).transpose(0, 3, 1, 2))
        outs[1].append(vt.reshape(batch, *heads, seq).transpose(0, 3, 1, 2))
        outs[2].append(kts.reshape(step_len, *heads, db).transpose(3, 0, 1, 2))
        outs[3].append(vts.reshape(step_len, *heads, db).transpose(3, 0, 1, 2))
        outs[4].append(st_p)
        outs[5].append(st_s)

    y_prompt = hp.reshape(batch, seq, D_MODEL)
    y_sample = hs.reshape(step_len, db, D_MODEL).transpose(1, 0, 2)
    return (y_prompt, y_sample) + tuple(jnp.stack(o) for o in outs)
```

```python
import functools

import jax
import jax.numpy as jnp
import numpy as np
from jax import lax
from jax.experimental import pallas as pl
from jax.experimental.pallas import tpu as pltpu

F32 = jnp.float32
BF16 = jnp.bfloat16

D_MODEL = 1024
ATTN_HEADS = 8
ATTN_HEAD_DIM = 64
ATTN_WIDTH = ATTN_HEADS * ATTN_HEAD_DIM
RET_HEADS = 4
RET_HEAD_DIM = 128
RET_WIDTH = RET_HEADS * RET_HEAD_DIM
IN_WIDTH = 3 * ATTN_WIDTH + 4 * RET_WIDTH
MOBA_BLOCK = 256
MOBA_TOP_K = 3
ROPE_THETA = 500000.0
ROPE_DIMS = ATTN_HEAD_DIM // 4
RET_CHUNK = 128
RET_ROT_BASE = 10000.0
N_MODS = 9
EPS = 1e-6
NEG_INF = -1e30

LANES = 128
SUBLANES = 8
VMEM_LIMIT = 56 * 1024 * 1024
MXU_TILE = 256
FF_CHUNK = MXU_TILE
MOBA_GROUP = 4
LOG2_E = 1.4426950408889634
SHIFT_LANE = 63
SHIFT_LIMIT = 60.0


def _dot(a, b):
    return jnp.dot(a, b, preferred_element_type=F32)


def _dot_nt(a, b):
    return lax.dot_general(a, b, (((1,), (1,)), ((), ())), preferred_element_type=F32)


def _split(a):
    hi = a.astype(BF16)
    lo = (a - hi.astype(F32)).astype(BF16)
    return hi, lo


def _dot3(a, b):
    ah, al = _split(a)
    bh, bl = _split(b)
    return _dot(ah, bh) + (_dot(ah, bl) + _dot(al, bh))


def _dot3_nt(a, b):
    ah, al = _split(a)
    bh, bl = _split(b)
    return _dot_nt(ah, bh) + (_dot_nt(ah, bl) + _dot_nt(al, bh))


def _sigmoid(x):
    return 1.0 / (1.0 + jnp.exp(-x))


def _silu(x):
    return x * _sigmoid(x)


def _rms(x):
    return x * lax.rsqrt(jnp.mean(x * x, axis=-1, keepdims=True) + EPS)


def _top_k_lanes(g, lane_f, k):
    sel = jnp.zeros(g.shape, jnp.bool_)
    for _ in range(k):
        m = jnp.max(g, axis=-1, keepdims=True)
        idx = jnp.min(jnp.where(g == m, lane_f, 1e9), axis=-1, keepdims=True)
        pick = lane_f == idx
        sel = jnp.logical_or(sel, pick)
        g = jnp.where(pick, -jnp.inf, g)
    return sel


def _fold_rows(op, s):
    while s.shape[0] > SUBLANES:
        half = s.shape[0] // 2
        s = op(s[:half], s[half:])
    return s


def _col_max(s):
    return jnp.max(_fold_rows(jnp.maximum, s), axis=0, keepdims=True)


def _col_min(s):
    return jnp.min(_fold_rows(jnp.minimum, s), axis=0, keepdims=True)


def _top_k_rows(g, row_f, k):
    sel = jnp.zeros(g.shape, jnp.bool_)
    for _ in range(k):
        m = _col_max(g)
        idx = _col_min(jnp.where(g == m, row_f, 1e9))
        pick = row_f == idx
        sel = jnp.logical_or(sel, pick)
        g = jnp.where(pick, -jnp.inf, g)
    return sel


def _mods_kernel(c_ref, w_ref, b_ref, o_ref):
    s = _silu(c_ref[...]).astype(BF16)
    o_ref[...] = _dot(s, w_ref[...].astype(BF16)) + b_ref[...]


def _mods(c, w_ada, b_ada):
    n = c.shape[0]
    n_pad = -(-n // SUBLANES) * SUBLANES
    c = jnp.pad(c, ((0, n_pad - n), (0, 0)))
    width = w_ada.shape[1]
    tn = 9 * LANES
    out = pl.pallas_call(
        _mods_kernel,
        grid=(width // tn,),
        in_specs=[pl.BlockSpec((n_pad, D_MODEL), lambda j: (0, 0)),
                  pl.BlockSpec((D_MODEL, tn), lambda j: (0, j)),
                  pl.BlockSpec((1, tn), lambda j: (0, j))],
        out_specs=pl.BlockSpec((n_pad, tn), lambda j: (0, j)),
        out_shape=jax.ShapeDtypeStruct((n_pad, width), F32),
        name="mods",
    )(c, w_ada, b_ada.reshape(1, width))
    return out[:n]


def _swiglu_acc(xb, wg_ref, wu_ref, wd_ref):
    d_ff = wg_ref.shape[1]
    acc = None
    for c in range(d_ff // FF_CHUNK):
        sl = slice(c * FF_CHUNK, (c + 1) * FF_CHUNK)
        g = _dot(xb, wg_ref[:, sl])
        u = _dot(xb, wu_ref[:, sl])
        part = _dot((_silu(g) * u).astype(BF16), wd_ref[sl, :])
        acc = part if acc is None else acc + part
    return acc


def _mod_rows(mod_ref, i, rows):
    m = mod_ref[0, i]
    reps = rows // m.shape[0]
    return m if m.shape[0] == 1 or reps == 1 else jnp.concatenate([m] * reps, axis=0)


def _const_spec(shape):
    nd = len(shape)
    return pl.BlockSpec(shape, lambda *_: (0,) * nd, pipeline_mode=pl.Buffered(1))


def _head_norm_gate(out, g):
    on = out * lax.rsqrt(jnp.mean(out * out, axis=-1, keepdims=True) + EPS)
    return on * _silu(g)


def _ret_chunks(q_ref, k_ref, v_ref, g_ref, dec_ref, qd_ref, kd_ref, cd_ref, o_ref, s_ref, fresh):
    c_len = RET_CHUNK
    n_chunks = q_ref.shape[0] // c_len
    pairs = [(c, h) for c in range(n_chunks) for h in range(RET_HEADS)]
    rows = lambda c: slice(c * c_len, (c + 1) * c_len)
    cols = lambda h: slice(h * LANES, (h + 1) * LANES)
    scores, updates = {}, {}
    for c, h in pairs:
        q, k, vb = q_ref[rows(c), cols(h)], k_ref[rows(c), cols(h)], v_ref[rows(c), cols(h)].astype(BF16)
        scores[c, h] = (_dot_nt(q.astype(BF16), k.astype(BF16)) * dec_ref[h]).astype(BF16)
        updates[c, h] = _dot((k * kd_ref[:, cols(h)]).T.astype(BF16), vb)
    states = {}
    for h in range(RET_HEADS):
        st = jnp.where(fresh, 0.0, s_ref[h])
        for c in range(n_chunks):
            states[c, h] = st
            st = cd_ref[h] * st + updates[c, h]
        s_ref[h] = st
    for c, h in pairs:
        q, vb = q_ref[rows(c), cols(h)], v_ref[rows(c), cols(h)].astype(BF16)
        out = _dot(scores[c, h], vb) + _dot((q * qd_ref[:, cols(h)]).astype(BF16), states[c, h].astype(BF16))
        o_ref[rows(c), cols(h)] = _head_norm_gate(out, g_ref[rows(c), cols(h)])


def _ret_tables(chunk_len):
    lg = np.log(1.0 - 2.0 ** (-5.0 - np.arange(RET_HEADS)))
    i = np.arange(RET_CHUNK, dtype=np.float64)
    diff = i[:, None] - i[None, :]
    decay = np.where(diff >= 0, np.exp(np.maximum(diff, 0.0)[None] * lg[:, None, None]), 0.0)
    qd = np.exp((i + 1.0)[None] * lg[:, None])
    kd = np.exp((chunk_len - 1.0 - i)[None] * lg[:, None])
    widen = lambda t: np.repeat(t.T, RET_HEAD_DIM, axis=1)
    cd = np.broadcast_to(np.exp(chunk_len * lg)[:, None, None], (RET_HEADS, 1, LANES))
    return tuple(jnp.asarray(t, F32) for t in (decay, widen(qd), widen(kd), cd))


def _ffn_proj_kernel(x_ref, mod_ref, n1_ref, wg_ref, wu_ref, wd_ref, nm_ref, win_ref, qn_ref, kn_ref, bd_ref,
                     ca_ref, sa1_ref, sa2_ref, cr_ref, sr1_ref, sr2_ref, *refs, tiles_per_seq):
    if tiles_per_seq is None:
        h_ref, ka_ref, va_ref, packed_ref = refs
        qa_ref, ka_row_ref, va_row_ref, qr_ref, kr_ref, vr_ref, gr_ref = (
            packed_ref.at[:, pl.ds(j * ATTN_WIDTH, ATTN_WIDTH)] for j in range(IN_WIDTH // ATTN_WIDTH))
    else:
        ka_row_ref = va_row_ref = None
        ret_tabs, (h_ref, qa_ref, ka_ref, va_ref, oret_ref, st_ref), (qr_ref, kr_ref, vr_ref, gr_ref, s_ref) = (
            refs[:4], refs[4:10], refs[10:])

        @pl.when(pl.program_id(0) == 0)
        def _init_state():
            s_ref[...] = jnp.zeros(s_ref.shape, F32)
    x = x_ref[...]
    mod = lambda i: _mod_rows(mod_ref, i, x.shape[0])
    xn = (_rms(x) * n1_ref[...]) * (1.0 + mod(1)) + mod(0)
    acc = _swiglu_acc(xn.astype(BF16), wg_ref, wu_ref, wd_ref)
    h = x + 0.5 * mod(2) * acc
    h_ref[...] = h
    hn = (_rms(h) * nm_ref[...]) * (1.0 + mod(4)) + mod(3)
    hb = hn.astype(BF16)

    def seg(i):
        return _dot(hb, win_ref[:, i * ATTN_WIDTH:(i + 1) * ATTN_WIDTH])

    ca, sa1, sa2 = ca_ref[0], sa1_ref[0], sa2_ref[0]
    cr, sr1, sr2 = cr_ref[0], sr1_ref[0], sr2_ref[0]
    bd = bd_ref[...]

    def store_t(o_ref, g, r):
        width = o_ref.shape[2]
        for t in range(o_ref.shape[0]):
            o_ref[t, g * LANES:(g + 1) * LANES, :] = r[t * width:(t + 1) * width, :].T

    def attn_head_norm_rope(p, w, row_ref, t_ref):
        hi, lo = _split(p * p)
        ms = jnp.concatenate([_dot(hi[:, c:c + MXU_TILE], bd) + _dot(lo[:, c:c + MXU_TILE], bd)
                              for c in range(0, ATTN_WIDTH, MXU_TILE)], axis=1)
        pn = (p * lax.rsqrt(ms + EPS)) * w
        for g in range(ATTN_WIDTH // LANES):
            xg = pn[:, g * LANES:(g + 1) * LANES]
            r = xg * ca + pltpu.roll(xg, LANES - ROPE_DIMS // 2, 1) * sa1 + pltpu.roll(xg, ROPE_DIMS // 2, 1) * sa2
            if t_ref is not None:
                store_t(t_ref, g, r)
            if row_ref is not None:
                row_ref[:, g * LANES:(g + 1) * LANES] = r

    def ret_rotate(p, o_ref, scale):
        for g in range(RET_HEADS):
            xg = p[:, g * LANES:(g + 1) * LANES]
            r = xg * cr + pltpu.roll(xg, LANES - 1, 1) * sr1 + pltpu.roll(xg, 1, 1) * sr2
            o_ref[:, g * LANES:(g + 1) * LANES] = r if scale is None else r * scale

    attn_head_norm_rope(seg(0), qn_ref[...], qa_ref, None)
    attn_head_norm_rope(seg(1), kn_ref[...], ka_row_ref, ka_ref)
    va = seg(2)
    for g in range(ATTN_WIDTH // LANES):
        store_t(va_ref, g, va[:, g * LANES:(g + 1) * LANES])
    if va_row_ref is not None:
        va_row_ref[...] = va
    ret_rotate(seg(3), qr_ref, None)
    ret_rotate(seg(4), kr_ref, RET_HEAD_DIM ** -0.5)
    vr_ref[...] = seg(5)
    gr_ref[...] = seg(6)
    if tiles_per_seq is not None:
        _ret_chunks(qr_ref, kr_ref, vr_ref, gr_ref, *ret_tabs, oret_ref, s_ref, pl.program_id(0) % tiles_per_seq == 0)
        st_ref[0] = s_ref[...]


def _ffn_proj(x2d, mods, tabs, lw, *, tm, mod_idx, tab_idx, kv_groups, kv_idx, kv_tiles=1, ret_seqs=None):
    n = x2d.shape[0]
    rm = mods.shape[2]
    rt = tabs[0].shape[1]
    d_ff = lw["wg1"].shape[1]
    row = lambda i: (i, 0)
    tab_spec = pl.BlockSpec((1, rt, LANES), lambda i: (tab_idx(i), 0, 0))
    in_specs = [
        pl.BlockSpec((tm, D_MODEL), row),
        pl.BlockSpec((1, N_MODS, rm, D_MODEL), lambda i: (mod_idx(i), 0, 0, 0)),
        _const_spec((1, D_MODEL)),
        _const_spec((D_MODEL, d_ff)), _const_spec((D_MODEL, d_ff)), _const_spec((d_ff, D_MODEL)),
        _const_spec((1, D_MODEL)),
        _const_spec((D_MODEL, IN_WIDTH)),
        _const_spec((1, ATTN_WIDTH)), _const_spec((1, ATTN_WIDTH)),
        _const_spec((MXU_TILE, MXU_TILE)),
    ] + [tab_spec] * 6
    half = jax.ShapeDtypeStruct((n, ATTN_WIDTH), F32)
    half_t = jax.ShapeDtypeStruct((kv_groups, ATTN_WIDTH, n // kv_groups), F32)
    half_spec = pl.BlockSpec((tm, ATTN_WIDTH), row)
    half_t_spec = pl.BlockSpec((kv_tiles, ATTN_WIDTH, tm // kv_tiles), lambda i: kv_idx(i))
    out_shape = [jax.ShapeDtypeStruct((n, D_MODEL), F32), half, half_t, half_t]
    out_specs = [pl.BlockSpec((tm, D_MODEL), row), half_spec, half_t_spec, half_t_spec]
    operands = [x2d, mods, lw["n1"], lw["wg1"], lw["wu1"], lw["wd1"], lw["nm"], lw["win"], lw["qn"], lw["kn"], lw["bd"], *tabs]
    if ret_seqs is None:
        tiles_per_seq, scratch = None, []
        out_shape = [out_shape[0], half_t, half_t, jax.ShapeDtypeStruct((n, IN_WIDTH), F32)]
        out_specs = [out_specs[0], half_t_spec, half_t_spec, pl.BlockSpec((tm, IN_WIDTH), row)]
    else:
        tiles_per_seq = n // ret_seqs // tm
        state = (RET_HEADS, RET_HEAD_DIM, RET_HEAD_DIM)
        operands += list(_ret_tables(RET_CHUNK))
        in_specs += [_const_spec((RET_HEADS, RET_CHUNK, RET_CHUNK)), _const_spec((RET_CHUNK, RET_WIDTH)),
                     _const_spec((RET_CHUNK, RET_WIDTH)), _const_spec((RET_HEADS, 1, LANES))]
        out_shape += [half, jax.ShapeDtypeStruct((ret_seqs,) + state, F32)]
        out_specs += [half_spec, pl.BlockSpec((1,) + state, lambda i: (i // tiles_per_seq, 0, 0, 0))]
        scratch = [pltpu.VMEM((tm, RET_WIDTH), F32)] * 4 + [pltpu.VMEM(state, F32)]
    return pl.pallas_call(
        functools.partial(_ffn_proj_kernel, tiles_per_seq=tiles_per_seq),
        grid=(n // tm,),
        in_specs=in_specs,
        out_specs=out_specs,
        out_shape=out_shape,
        scratch_shapes=scratch,
        compiler_params=pltpu.CompilerParams(dimension_semantics=("arbitrary",), vmem_limit_bytes=VMEM_LIMIT),
        name="ffn_proj",
    )(*operands)


def _out_ffn_kernel(h_ref, oa_ref, or_ref, mod_ref, wo_ref, n2_ref, wg_ref, wu_ref, wd_ref, y_ref):
    mix = _dot(oa_ref[...].astype(BF16), wo_ref[:ATTN_WIDTH, :]) + _dot(or_ref[...].astype(BF16), wo_ref[ATTN_WIDTH:, :])
    mod = lambda i: _mod_rows(mod_ref, i, h_ref.shape[0])
    h = h_ref[...] + mod(5) * mix
    hn = (_rms(h) * n2_ref[...]) * (1.0 + mod(7)) + mod(6)
    acc = _swiglu_acc(hn.astype(BF16), wg_ref, wu_ref, wd_ref)
    y_ref[...] = h + 0.5 * mod(8) * acc


def _out_ffn(h2d, oa, o_ret, mods, lw, *, tm, mod_idx):
    n = h2d.shape[0]
    rm = mods.shape[2]
    d_ff = lw["wg2"].shape[1]
    row = lambda i: (i, 0)
    return pl.pallas_call(
        _out_ffn_kernel,
        grid=(n // tm,),
        in_specs=[
            pl.BlockSpec((tm, D_MODEL), row),
            pl.BlockSpec((tm, ATTN_WIDTH), row),
            pl.BlockSpec((tm, RET_WIDTH), row),
            pl.BlockSpec((1, N_MODS, rm, D_MODEL), lambda i: (mod_idx(i), 0, 0, 0)),
            _const_spec((D_MODEL, D_MODEL)),
            _const_spec((1, D_MODEL)),
            _const_spec((D_MODEL, d_ff)), _const_spec((D_MODEL, d_ff)), _const_spec((d_ff, D_MODEL)),
        ],
        out_specs=pl.BlockSpec((tm, D_MODEL), row),
        out_shape=jax.ShapeDtypeStruct((n, D_MODEL), F32),
        compiler_params=pltpu.CompilerParams(dimension_semantics=("arbitrary",), vmem_limit_bytes=VMEM_LIMIT),
        name="out_ffn",
    )(h2d, oa, o_ret, mods, lw["wo"], lw["n2"], lw["wg2"], lw["wu2"], lw["wd2"])


def _moba_prompt_kernel(q_ref, qn_ref, kt_ref, vt_in_ref, qn2_ref, o_ref,
                        kaug_ref, vt_ref, km_ref, bound_ref, flag_ref, acc_ref, shift_ref, bias_ref, *, nb):
    qi = pl.program_id(2)
    tq = MOBA_BLOCK
    hd = ATTN_HEAD_DIM
    grp = MOBA_GROUP
    lane = lax.broadcasted_iota(jnp.int32, (tq, LANES), 1)
    in_head = (lane < hd, lane >= hd)
    field_off = (hd, 0)

    @pl.when(qi == 0)
    def _prepare():
        km_ref[...] = jnp.zeros(km_ref.shape, F32)
        feat = lax.broadcasted_iota(jnp.int32, (LANES, tq), 0)
        feat_in_head = (feat < hd, feat >= hd)
        kn2 = [jnp.zeros((1, 1), F32)] * 2
        for j in range(nb):
            kb = kt_ref[0, :, j * tq:(j + 1) * tq].T
            vtb = vt_in_ref[0, :, j * tq:(j + 1) * tq]
            km_ref[hd + j:hd + j + 1, :] = jnp.mean(kb, axis=0, keepdims=True)
            sq = kb * kb
            for h in range(2):
                norm2 = jnp.sum(jnp.where(in_head[h], sq, 0.0), axis=1, keepdims=True)
                kn2[h] = jnp.maximum(kn2[h], jnp.max(norm2, axis=0, keepdims=True))
                off = field_off[h]
                field = jnp.where(jnp.logical_or(lane == off + j, lane == off + SHIFT_LANE), 1.0, 0.0)
                kaug_ref[h, j * tq:(j + 1) * tq, :] = jnp.where(in_head[h], kb, field).astype(BF16)
                vt_ref[h, j] = jnp.where(feat_in_head[h], vtb, 1.0).astype(BF16)
        bound = [jnp.sqrt(qn2_ref[0:1, 0:1] * kn2[h]) for h in range(2)]
        for h in range(2):
            bound_ref[h] = jnp.broadcast_to(bound[h], bound_ref.shape[1:])
        flag_ref[0] = (jnp.max(jnp.maximum(bound[0], bound[1])) <= SHIFT_LIMIT).astype(jnp.int32)

        bias_ref[...] = jnp.full(bias_ref.shape, NEG_INF, F32)

    q = q_ref[...]
    scale = hd ** -0.5 * LOG2_E
    qs = [jnp.where(in_head[h], q, 0.0) * scale for h in range(2)]

    field_row = lax.broadcasted_iota(jnp.int32, (LANES, tq), 0)
    field_row_f = field_row.astype(F32)

    def block_bias(q_tile, h, n_past):
        off = field_off[h]
        q_head = jnp.where(in_head[h], q_tile, 0.0)
        gate_t = _dot3_nt(km_ref[hd - off:hd - off + LANES, :], q_head)
        valid = jnp.logical_and(field_row >= off, field_row < off + n_past)
        sel = _top_k_rows(jnp.where(valid, gate_t, NEG_INF), field_row_f, MOBA_TOP_K)
        return jnp.where(jnp.logical_and(sel, valid), 0.0, NEG_INF).T

    key_i = lax.broadcasted_iota(jnp.int32, (tq, tq), 0)
    qry_i = lax.broadcasted_iota(jnp.int32, (tq, tq), 1)
    causal = key_i <= qry_i
    own_rows = pl.ds(pl.multiple_of(qi * tq, tq), tq)
    n_groups = (qi + grp - 1) // grp

    def group_rows(g):
        return pl.ds(pl.multiple_of(g * (grp * tq), grp * tq), grp * tq)

    bounded = flag_ref[0] == 1

    @pl.when(bounded)
    def _bound_shift():
        for h in range(2):
            shift_ref[h] = jnp.broadcast_to(bound_ref[h, 0:1, :], (tq, LANES))

    @pl.when(jnp.logical_not(bounded))
    def _exact_shift():
        for h in range(2):
            q_sel = jnp.where(in_head[h], qs[h], bias_ref[h]).astype(BF16)
            m = _col_max(jnp.where(causal, _dot_nt(kaug_ref[h, own_rows, :], qs[h].astype(BF16)), NEG_INF))
            m = lax.fori_loop(
                0, n_groups, lambda g, m: jnp.maximum(m, _col_max(_dot_nt(kaug_ref[h, group_rows(g), :], q_sel))), m)
            shift_ref[h] = jnp.broadcast_to(m, (LANES, tq)).T

    q_own, q_past = [], []
    for h in range(2):
        at_shift = lane == field_off[h] + SHIFT_LANE
        q_own.append(jnp.where(in_head[h], qs[h], jnp.where(at_shift, -shift_ref[h], 0.0)))
        q_past.append(jnp.where(jnp.logical_or(in_head[h], at_shift), q_own[h], bias_ref[h]).astype(BF16))
    acc_ref[...] = jnp.zeros(acc_ref.shape, F32)

    def add_groups(groups, with_own=False):
        scores = [[_dot_nt(kaug_ref[h, group_rows(g), :], q_past[h]) for h in range(2)] for g in groups]
        if with_own:
            own_scores = [_dot_nt(kaug_ref[h, own_rows, :], q_own[h].astype(BF16)) for h in range(2)]
            q_next = qn_ref[...]
            for h in range(2):
                bias_ref[h] = block_bias(q_next, h, qi + 1)
            for h in range(2):
                s = jnp.where(causal, own_scores[h], NEG_INF)
                acc_ref[h] += _dot(vt_ref[h, qi], jnp.exp2(s).astype(BF16))
        for g, sc in zip(groups, scores):
            for h in range(2):
                p = jnp.exp2(sc[h]).astype(BF16)
                pv = None
                for c in range(grp):
                    part = _dot(vt_ref[h, g * grp + c], p[c * tq:(c + 1) * tq, :])
                    pv = part if pv is None else pv + part
                acc_ref[h] += pv

    def pair_body(t, carry):
        add_groups([2 * t, 2 * t + 1])
        return carry

    has_pair = n_groups >= 2
    odd = n_groups % 2 == 1

    @pl.when(has_pair)
    def _first_pair():
        add_groups([0, 1], with_own=True)

    lax.fori_loop(1, n_groups // 2, pair_body, 0)

    @pl.when(jnp.logical_and(has_pair, odd))
    def _last_group():
        add_groups([n_groups - 1])

    @pl.when(n_groups == 1)
    def _only_group():
        add_groups([0], with_own=True)

    @pl.when(n_groups == 0)
    def _no_group():
        add_groups([], with_own=True)

    a0, a1 = acc_ref[0], acc_ref[1]
    o_t = jnp.concatenate([a0[:hd] * (1.0 / a0[hd:hd + 1]), a1[hd:] * (1.0 / a1[0:1])], axis=0)
    o_ref[...] = o_t.T


def _moba_prompt(qa, kt, vt, q_norm2_max, batch, seq):
    tq = MOBA_BLOCK
    assert seq % (tq * MOBA_GROUP) == 0
    nb = seq // tq
    assert nb <= SHIFT_LANE, "side field holds one lane per key block below the shift lane"
    n_pairs = ATTN_WIDTH // LANES
    kv_spec = pl.BlockSpec((1, LANES, seq), lambda b, hp, qi: (b, hp, 0))
    q_spec = pl.BlockSpec((tq, LANES), lambda b, hp, qi: (b * nb + qi, hp))
    q_next_spec = pl.BlockSpec((tq, LANES), lambda b, hp, qi: (b * nb + jnp.minimum(qi + 1, nb - 1), hp))
    return pl.pallas_call(
        functools.partial(_moba_prompt_kernel, nb=nb),
        grid=(batch, n_pairs, nb),
        in_specs=[q_spec, q_next_spec, kv_spec, kv_spec, _const_spec((1, LANES))],
        out_specs=q_spec,
        out_shape=jax.ShapeDtypeStruct(qa.shape, F32),
        scratch_shapes=[
            pltpu.VMEM((2, seq, LANES), BF16),
            pltpu.VMEM((2, nb, LANES, tq), BF16),
            pltpu.VMEM((ATTN_HEAD_DIM + LANES, LANES), F32),
            pltpu.VMEM((2, SUBLANES, LANES), F32),
            pltpu.SMEM((1,), jnp.int32),
            pltpu.VMEM((2, LANES, tq), F32),
            pltpu.VMEM((2, tq, LANES), F32),
            pltpu.VMEM((2, tq, LANES), F32),
        ],
        compiler_params=pltpu.CompilerParams(dimension_semantics=("arbitrary", "arbitrary", "arbitrary"),
                                             vmem_limit_bytes=VMEM_LIMIT),
        name="moba_prompt",
    )(qa, qa, kt, vt, jnp.broadcast_to(q_norm2_max, (1, LANES)).astype(F32))


def _ret_sample_kernel(q_ref, k_ref, v_ref, g_ref, st_ref, dec_ref, qd_ref, kd_ref, cd_ref, o_ref, ns_ref):
    tp = q_ref.shape[1]
    zeros = jnp.zeros((RET_CHUNK - tp, LANES), F32)
    pairs = [(s, h) for s in range(q_ref.shape[0]) for h in range(RET_HEADS)]
    cols = lambda h: slice(h * LANES, (h + 1) * LANES)
    scores, values = {}, {}
    for s, h in pairs:
        k = jnp.concatenate([k_ref[s, :, cols(h)], zeros], axis=0)
        vb = jnp.concatenate([v_ref[s, :, cols(h)], zeros], axis=0).astype(BF16)
        scores[s, h] = (_dot_nt(q_ref[s, :, cols(h)].astype(BF16), k.astype(BF16)) * dec_ref[h, :tp, :]).astype(BF16)
        ns_ref[s, h] = cd_ref[h] * st_ref[s, h] + _dot((k * kd_ref[:, cols(h)]).T.astype(BF16), vb)
        values[s, h] = vb
    for s, h in pairs:
        q_dec = (q_ref[s, :, cols(h)] * qd_ref[:tp, cols(h)]).astype(BF16)
        out = _dot(scores[s, h], values[s, h]) + _dot(q_dec, st_ref[s, h].astype(BF16))
        o_ref[s, :, cols(h)] = _head_norm_gate(out, g_ref[s, :, cols(h)])


def _ret_sample(packed, cols, state, step_len, *, ns):
    db, tp, _ = packed.shape
    tok = pl.BlockSpec((ns, tp, RET_WIDTH), lambda i: (i, 0, 0))
    tok_in = [pl.BlockSpec((ns, tp, RET_WIDTH), functools.partial(lambda i, c: (i, 0, c), c=c)) for c in cols]
    st = pl.BlockSpec((ns, RET_HEADS, RET_HEAD_DIM, RET_HEAD_DIM), lambda i: (i, 0, 0, 0))
    decay, qd, kd, cd = _ret_tables(step_len)
    return pl.pallas_call(
        _ret_sample_kernel,
        grid=(db // ns,),
        in_specs=tok_in + [st,
                           _const_spec((RET_HEADS, RET_CHUNK, RET_CHUNK)),
                           _const_spec((RET_CHUNK, RET_WIDTH)), _const_spec((RET_CHUNK, RET_WIDTH)),
                           _const_spec((RET_HEADS, 1, LANES))],
        out_specs=[tok, st],
        out_shape=[jax.ShapeDtypeStruct((db, tp, RET_WIDTH), F32), jax.ShapeDtypeStruct(state.shape, F32)],
        compiler_params=pltpu.CompilerParams(dimension_semantics=("arbitrary",)),
        name="ret_sample",
    )(packed, packed, packed, packed, state, decay, qd, kd, cd)


def _moba_sample_seq(q, kn, vn, k_refs, v_refs, *, page, step_len):
    n_pages = len(k_refs)
    hd = ATTN_HEAD_DIM
    ppb = MOBA_BLOCK // page
    n_full = n_pages // ppb
    rows = step_len * ATTN_HEADS
    qrep = jnp.concatenate([jnp.broadcast_to(q[t:t + 1, :], (ATTN_HEADS, ATTN_WIDTH)) for t in range(step_len)], axis=0)
    r_i = lax.broadcasted_iota(jnp.int32, (rows, ATTN_WIDTH), 0)
    c_i = lax.broadcasted_iota(jnp.int32, (rows, ATTN_WIDTH), 1)
    own_head = (c_i // hd) == (r_i % ATTN_HEADS)
    qbd = jnp.where(own_head, qrep, 0.0)
    qs = qbd * (hd ** -0.5)

    vt = lambda p: v_refs[p][...].reshape(ATTN_WIDTH, page)
    lane = lax.broadcasted_iota(jnp.int32, (rows, LANES), 1)
    gate = jnp.full((rows, LANES), NEG_INF, F32)
    k_pages = []
    for b in range(n_full):
        ksum = None
        for p in range(b * ppb, (b + 1) * ppb):
            kt = k_refs[p][...].reshape(ATTN_WIDTH, page)
            k_pages.append(kt.astype(BF16))
            ksum = kt if ksum is None else ksum + kt
        g = jnp.sum(_dot3(qbd, ksum), axis=-1, keepdims=True) * (1.0 / MOBA_BLOCK)
        gate = jnp.where(lane == b, g, gate)
    s_past = _dot(qs.astype(BF16), jnp.concatenate(k_pages, axis=1))
    sel = jnp.logical_and(_top_k_lanes(gate, lane.astype(F32), min(MOBA_TOP_K, n_full)), lane < n_full)
    bias = jnp.where(sel, 0.0, NEG_INF)
    bias_cols = [jnp.max(jnp.where(lane == b, bias, NEG_INF), axis=-1, keepdims=True) for b in range(n_full)]
    s_past = s_past + jnp.concatenate([jnp.broadcast_to(c, (rows, MOBA_BLOCK)) for c in bias_cols], axis=1)

    tok_of_row = lax.broadcasted_iota(jnp.int32, (rows, 1), 0) // ATTN_HEADS
    s_own = []
    for t in range(step_len):
        s_t = jnp.sum(qs * kn[t:t + 1, :], axis=-1, keepdims=True)
        s_own.append(jnp.where(tok_of_row >= t, s_t, NEG_INF))

    m = jnp.maximum(functools.reduce(jnp.maximum, s_own), jnp.max(s_past, axis=-1, keepdims=True))
    e_past = jnp.exp(s_past - m)
    l = jnp.sum(e_past, axis=-1, keepdims=True)
    o = _dot_nt(e_past.astype(BF16), jnp.concatenate([vt(p).astype(BF16) for p in range(n_pages)], axis=1))
    for t in range(step_len):
        e = jnp.exp(s_own[t] - m)
        l = l + e
        o = o + e * vn[t:t + 1, :]
    o = jnp.where(own_head, o, 0.0) * (1.0 / l)
    return [jnp.sum(o[t * ATTN_HEADS:(t + 1) * ATTN_HEADS, :], axis=0, keepdims=True) for t in range(step_len)]


def _moba_sample_kernel(pt_ref, q_ref, kn_ref, vn_ref, ck_hbm, cv_hbm, o_ref, kbuf, vbuf, sems, *, n_pages, page, step_len):
    i = pl.program_id(0)
    n_seq = q_ref.shape[0]

    def page_copies(step, slot):
        copies = []
        for j in range(n_seq * n_pages):
            phys = pt_ref[step * n_seq + j // n_pages, j % n_pages]
            copies.append(pltpu.make_async_copy(ck_hbm.at[phys], kbuf.at[slot, j], sems.at[0, slot]))
            copies.append(pltpu.make_async_copy(cv_hbm.at[phys], vbuf.at[slot, j], sems.at[1, slot]))
        return copies

    @pl.when(i == 0)
    def _first():
        for c in page_copies(0, 0):
            c.start()

    @pl.when(i + 1 < pl.num_programs(0))
    def _prefetch():
        for c in page_copies(i + 1, (i + 1) % 2):
            c.start()

    slot = i % 2
    for c in page_copies(i, slot):
        c.wait()

    o_ref[...] = jnp.zeros(o_ref.shape, F32)
    for s in range(n_seq):
        k_refs = [kbuf.at[slot, s * n_pages + p] for p in range(n_pages)]
        v_refs = [vbuf.at[slot, s * n_pages + p] for p in range(n_pages)]
        out_rows = _moba_sample_seq(q_ref[s], kn_ref[s], vn_ref[s], k_refs, v_refs, page=page, step_len=step_len)
        for t, row in enumerate(out_rows):
            o_ref[s, t:t + 1, :] = row


def _moba_sample(packed, cols, cache_kt, cache_vt, page_table, step_len, *, n_seq):
    db, tp, _ = packed.shape
    n_pages = page_table.shape[1]
    page = cache_kt.shape[-1]
    assert (n_pages * page) % MOBA_BLOCK == 0, "past length must be whole key blocks"
    assert MOBA_BLOCK % page == 0 and db % n_seq == 0
    tok = pl.BlockSpec((n_seq, tp, ATTN_WIDTH), lambda b, pt: (b, 0, 0))
    tok_in = [pl.BlockSpec((n_seq, tp, ATTN_WIDTH), functools.partial(lambda b, pt, c: (b, 0, c), c=c)) for c in cols]
    hbm = pl.BlockSpec(memory_space=pl.ANY)
    page_buf = pltpu.VMEM((2, n_seq * n_pages) + cache_kt.shape[1:], F32)
    grid_spec = pltpu.PrefetchScalarGridSpec(
        num_scalar_prefetch=1,
        grid=(db // n_seq,),
        in_specs=tok_in + [hbm, hbm],
        out_specs=tok,
        scratch_shapes=[page_buf, page_buf, pltpu.SemaphoreType.DMA((2, 2))],
    )
    return pl.pallas_call(
        functools.partial(_moba_sample_kernel, n_pages=n_pages, page=page, step_len=step_len),
        grid_spec=grid_spec,
        out_shape=jax.ShapeDtypeStruct((db, tp, ATTN_WIDTH), F32),
        compiler_params=pltpu.CompilerParams(dimension_semantics=("arbitrary",), vmem_limit_bytes=VMEM_LIMIT),
        name="moba_sample",
    )(page_table, packed, packed, packed, cache_kt, cache_vt)


def _cos_sin(inv, n_tiles, tile, start):
    base = (start + np.arange(n_tiles) * tile)[:, None, None] * inv
    off = np.arange(tile)[None, :, None] * inv
    cb, sb, co, so = (jnp.asarray(f(a), F32) for a in (base, off) for f in (np.cos, np.sin))
    return cb * co - sb * so, sb * co + cb * so


def _rope_tables(n_tiles, tile, start):
    half = ROPE_DIMS // 2
    j = np.arange(LANES) % ATTN_HEAD_DIM
    inv = np.where(j < ROPE_DIMS, ROPE_THETA ** (-(j % half) * 2.0 / ROPE_DIMS), 0.0)
    first = (j < half).astype(np.float32)
    second = np.logical_and(j >= half, j < ROPE_DIMS).astype(np.float32)
    cos, sin = _cos_sin(inv, n_tiles, tile, start)
    return cos, sin * -first, sin * second


def _ret_rot_tables(n_tiles, tile, start):
    lane = np.arange(LANES)
    inv = (1.0 / RET_ROT_BASE ** np.linspace(0.0, 1.0, RET_HEAD_DIM // 2))[lane // 2]
    even = (lane % 2 == 0).astype(np.float32)
    cos, sin = _cos_sin(inv, n_tiles, tile, start)
    return cos, sin * -even, sin * (1.0 - even)


def _layer_weights(l, norm_ffn1_w, ffn1_w_gate, ffn1_w_up, ffn1_w_down, norm_mix_w, w_in, q_norm_w, k_norm_w, w_out,
                   norm_ffn2_w, ffn2_w_gate, ffn2_w_up, ffn2_w_down):
    head = np.arange(MXU_TILE) // ATTN_HEAD_DIM
    return {
        "n1": norm_ffn1_w[l][None], "nm": norm_mix_w[l][None], "n2": norm_ffn2_w[l][None],
        "wg1": ffn1_w_gate[l].astype(BF16), "wu1": ffn1_w_up[l].astype(BF16), "wd1": ffn1_w_down[l].astype(BF16),
        "wg2": ffn2_w_gate[l].astype(BF16), "wu2": ffn2_w_up[l].astype(BF16), "wd2": ffn2_w_down[l].astype(BF16),
        "win": w_in[l].astype(BF16), "wo": w_out[l].astype(BF16),
        "qn": jnp.tile(q_norm_w[l], ATTN_HEADS)[None], "kn": jnp.tile(k_norm_w[l], ATTN_HEADS)[None],
        "bd": jnp.asarray(np.where(head[:, None] == head[None, :], 1.0 / ATTN_HEAD_DIM, 0.0), BF16),
    }


def kernel(x_prompt, x_sample, cache_k, cache_v, state_ret, page_table, c_prompt, c_sample, w_ada, b_ada, norm_ffn1_w, ffn1_w_gate, ffn1_w_up, ffn1_w_down, norm_mix_w, w_in, q_norm_w, k_norm_w, w_out, norm_ffn2_w, ffn2_w_gate, ffn2_w_up, ffn2_w_down):
    batch, seq, _ = x_prompt.shape
    db, step_len, _ = x_sample.shape
    depth = w_ada.shape[0]
    n_pages, page = page_table.shape[1], cache_k.shape[2]
    past_len = n_pages * page
    tm_p = 512
    tp = SUBLANES
    assert seq % tm_p == 0 and step_len <= tp and db % SUBLANES == 0

    tabs_p = list(_rope_tables(seq // tm_p, tm_p, 0) + _ret_rot_tables(seq // tm_p, tm_p, 0))
    tabs_s = [jnp.broadcast_to(t[0][:, None, :], (step_len, db, LANES)).reshape(1, step_len * db, LANES)
              for t in _rope_tables(1, step_len, past_len) + _ret_rot_tables(1, step_len, past_len)]

    hp = x_prompt.reshape(batch * seq, D_MODEL)
    hs = x_sample.transpose(1, 0, 2).reshape(step_len * db, D_MODEL)
    outs = [[] for _ in range(6)]
    for l in range(depth):
        lw = _layer_weights(l, norm_ffn1_w, ffn1_w_gate, ffn1_w_up, ffn1_w_down, norm_mix_w, w_in, q_norm_w, k_norm_w,
                            w_out, norm_ffn2_w, ffn2_w_gate, ffn2_w_up, ffn2_w_down)
        mods = _mods(jnp.concatenate([c_prompt, c_sample], axis=0), w_ada[l], b_ada[l])
        mods_p = mods[:batch].reshape(batch, N_MODS, 1, D_MODEL)
        mods_s = mods[batch:].reshape(db, N_MODS, D_MODEL).transpose(1, 0, 2)[None]

        tiles_per_seq = seq // tm_p
        h1, qa, kt, vt, o_ret, st_p = _ffn_proj(
            hp, mods_p, tabs_p, lw, tm=tm_p, mod_idx=lambda i: i // tiles_per_seq, tab_idx=lambda i: i % tiles_per_seq,
            kv_groups=batch, kv_idx=lambda i: (i // tiles_per_seq, 0, i % tiles_per_seq), ret_seqs=batch)
        q_norm2_max = jnp.max(jnp.square(q_norm_w[l])) * (LOG2_E * LOG2_E)
        oa = _moba_prompt(qa, kt, vt, q_norm2_max, batch, seq)
        hp = _out_ffn(h1, oa, o_ret, mods_p, lw, tm=tm_p, mod_idx=lambda i: i // tiles_per_seq)

        h1s, kts, vts, proj_s = _ffn_proj(
            hs, mods_s, tabs_s, lw, tm=step_len * db, mod_idx=lambda i: 0, tab_idx=lambda i: 0,
            kv_groups=step_len, kv_idx=lambda i: (0, 0, 0), kv_tiles=step_len)
        proj_s = jnp.pad(proj_s.reshape(step_len, db, IN_WIDTH).transpose(1, 0, 2), ((0, 0), (0, tp - step_len), (0, 0)))

        def token_major(t):
            return t[:, :step_len].transpose(1, 0, 2).reshape(step_len * db, ATTN_WIDTH)

        cache_kt = cache_k[l].transpose(0, 2, 3, 1)
        cache_vt = cache_v[l].transpose(0, 2, 3, 1)
        oas = _moba_sample(proj_s, (0, 1, 2), cache_kt, cache_vt, page_table, step_len, n_seq=2)
        o_rets, st_s = _ret_sample(proj_s, (3, 4, 5, 6), state_ret[l], step_len, ns=SUBLANES)
        hs = _out_ffn(h1s, token_major(oas), token_major(o_rets), mods_s, lw, tm=step_len * db, mod_idx=lambda i: 0)

        heads = (ATTN_HEADS, ATTN_HEAD_DIM)
        outs[0].append(kt.reshape(batch, *heads, seq).transpose(0, 3, 1, 2))
        outs[1].append(vt.reshape(batch, *heads, seq).transpose(0, 3, 1, 2))
        outs[2].append(kts.reshape(step_len, *heads, db).transpose(3, 0, 1, 2))
        outs[3].append(vts.reshape(step_len, *heads, db).transpose(3, 0, 1, 2))
        outs[4].append(st_p)
        outs[5].append(st_s)

    y_prompt = hp.reshape(batch, seq, D_MODEL)
    y_sample = hs.reshape(step_len, db, D_MODEL).transpose(1, 0, 2)
    return (y_prompt, y_sample) + tuple(jnp.stack(o) for o in outs)
```

```python
import functools

import jax
import jax.numpy as jnp
import numpy as np
from jax import lax
from jax.experimental import pallas as pl
from jax.experimental.pallas import tpu as pltpu

F32 = jnp.float32
BF16 = jnp.bfloat16

D_MODEL = 1024
ATTN_HEADS = 8
ATTN_HEAD_DIM = 64
ATTN_WIDTH = ATTN_HEADS * ATTN_HEAD_DIM
RET_HEADS = 4
RET_HEAD_DIM = 128
RET_WIDTH = RET_HEADS * RET_HEAD_DIM
IN_WIDTH = 3 * ATTN_WIDTH + 4 * RET_WIDTH
MOBA_BLOCK = 256
MOBA_TOP_K = 3
ROPE_THETA = 500000.0
ROPE_DIMS = ATTN_HEAD_DIM // 4
RET_CHUNK = 128
RET_ROT_BASE = 10000.0
N_MODS = 9
EPS = 1e-6
NEG_INF = -1e30

LANES = 128
SUBLANES = 8
VMEM_LIMIT = 56 * 1024 * 1024
MXU_TILE = 256
FF_CHUNK = MXU_TILE
MOBA_GROUP = 4
LOG2_E = 1.4426950408889634
SHIFT_LANE = 63
SHIFT_LIMIT = 60.0


def _dot(a, b):
    return jnp.dot(a, b, preferred_element_type=F32)


def _dot_nt(a, b):
    return lax.dot_general(a, b, (((1,), (1,)), ((), ())), preferred_element_type=F32)


def _split(a):
    hi = a.astype(BF16)
    lo = (a - hi.astype(F32)).astype(BF16)
    return hi, lo


def _dot3(a, b):
    ah, al = _split(a)
    bh, bl = _split(b)
    return _dot(ah, bh) + (_dot(ah, bl) + _dot(al, bh))


def _dot3_nt(a, b):
    ah, al = _split(a)
    bh, bl = _split(b)
    return _dot_nt(ah, bh) + (_dot_nt(ah, bl) + _dot_nt(al, bh))


def _sigmoid(x):
    return 1.0 / (1.0 + jnp.exp(-x))


def _silu(x):
    return x * _sigmoid(x)


def _rms(x):
    return x * lax.rsqrt(jnp.mean(x * x, axis=-1, keepdims=True) + EPS)


def _top_k_lanes(g, lane_f, k):
    sel = jnp.zeros(g.shape, jnp.bool_)
    for _ in range(k):
        m = jnp.max(g, axis=-1, keepdims=True)
        idx = jnp.min(jnp.where(g == m, lane_f, 1e9), axis=-1, keepdims=True)
        pick = lane_f == idx
        sel = jnp.logical_or(sel, pick)
        g = jnp.where(pick, -jnp.inf, g)
    return sel


def _fold_rows(op, s):
    while s.shape[0] > SUBLANES:
        half = s.shape[0] // 2
        s = op(s[:half], s[half:])
    return s


def _col_max(s):
    return jnp.max(_fold_rows(jnp.maximum, s), axis=0, keepdims=True)


def _col_min(s):
    return jnp.min(_fold_rows(jnp.minimum, s), axis=0, keepdims=True)


def _top_k_rows(g, row_f, k):
    sel = jnp.zeros(g.shape, jnp.bool_)
    for _ in range(k):
        m = _col_max(g)
        idx = _col_min(jnp.where(g == m, row_f, 1e9))
        pick = row_f == idx
        sel = jnp.logical_or(sel, pick)
        g = jnp.where(pick, -jnp.inf, g)
    return sel


def _mods_kernel(c_ref, w_ref, b_ref, o_ref):
    s = _silu(c_ref[...]).astype(BF16)
    o_ref[...] = _dot(s, w_ref[...].astype(BF16)) + b_ref[...]


def _mods(c, w_ada, b_ada):
    n = c.shape[0]
    n_pad = -(-n // SUBLANES) * SUBLANES
    c = jnp.pad(c, ((0, n_pad - n), (0, 0)))
    width = w_ada.shape[1]
    tn = 9 * LANES
    out = pl.pallas_call(
        _mods_kernel,
        grid=(width // tn,),
        in_specs=[pl.BlockSpec((n_pad, D_MODEL), lambda j: (0, 0)),
                  pl.BlockSpec((D_MODEL, tn), lambda j: (0, j)),
                  pl.BlockSpec((1, tn), lambda j: (0, j))],
        out_specs=pl.BlockSpec((n_pad, tn), lambda j: (0, j)),
        out_shape=jax.ShapeDtypeStruct((n_pad, width), F32),
        name="mods",
    )(c, w_ada, b_ada.reshape(1, width))
    return out[:n]


def _swiglu_acc(xb, wg_ref, wu_ref, wd_ref):
    d_ff = wg_ref.shape[1]
    acc = None
    for c in range(d_ff // FF_CHUNK):
        sl = slice(c * FF_CHUNK, (c + 1) * FF_CHUNK)
        g = _dot(xb, wg_ref[:, sl])
        u = _dot(xb, wu_ref[:, sl])
        part = _dot((_silu(g) * u).astype(BF16), wd_ref[sl, :])
        acc = part if acc is None else acc + part
    return acc


def _mod_rows(mod_ref, i, rows):
    m = mod_ref[0, i]
    reps = rows // m.shape[0]
    return m if m.shape[0] == 1 or reps == 1 else jnp.concatenate([m] * reps, axis=0)


def _const_spec(shape):
    nd = len(shape)
    return pl.BlockSpec(shape, lambda *_: (0,) * nd, pipeline_mode=pl.Buffered(1))


def _head_norm_gate(out, g):
    on = out * lax.rsqrt(jnp.mean(out * out, axis=-1, keepdims=True) + EPS)
    return on * _silu(g)


def _ret_chunks(q_ref, k_ref, v_ref, g_ref, dec_ref, qd_ref, kd_ref, cd_ref, o_ref, s_ref, fresh):
    c_len = RET_CHUNK
    n_chunks = q_ref.shape[0] // c_len
    pairs = [(c, h) for c in range(n_chunks) for h in range(RET_HEADS)]
    rows = lambda c: slice(c * c_len, (c + 1) * c_len)
    cols = lambda h: slice(h * LANES, (h + 1) * LANES)
    scores, updates = {}, {}
    for c, h in pairs:
        q, k, vb = q_ref[rows(c), cols(h)], k_ref[rows(c), cols(h)], v_ref[rows(c), cols(h)].astype(BF16)
        scores[c, h] = (_dot_nt(q.astype(BF16), k.astype(BF16)) * dec_ref[h]).astype(BF16)
        updates[c, h] = _dot((k * kd_ref[:, cols(h)]).T.astype(BF16), vb)
    states = {}
    for h in range(RET_HEADS):
        st = jnp.where(fresh, 0.0, s_ref[h])
        for c in range(n_chunks):
            states[c, h] = st
            st = cd_ref[h] * st + updates[c, h]
        s_ref[h] = st
    for c, h in pairs:
        q, vb = q_ref[rows(c), cols(h)], v_ref[rows(c), cols(h)].astype(BF16)
        out = _dot(scores[c, h], vb) + _dot((q * qd_ref[:, cols(h)]).astype(BF16), states[c, h].astype(BF16))
        o_ref[rows(c), cols(h)] = _head_norm_gate(out, g_ref[rows(c), cols(h)])


def _ret_tables(chunk_len):
    lg = np.log(1.0 - 2.0 ** (-5.0 - np.arange(RET_HEADS)))
    i = np.arange(RET_CHUNK, dtype=np.float64)
    diff = i[:, None] - i[None, :]
    decay = np.where(diff >= 0, np.exp(np.maximum(diff, 0.0)[None] * lg[:, None, None]), 0.0)
    qd = np.exp((i + 1.0)[None] * lg[:, None])
    kd = np.exp((chunk_len - 1.0 - i)[None] * lg[:, None])
    widen = lambda t: np.repeat(t.T, RET_HEAD_DIM, axis=1)
    cd = np.broadcast_to(np.exp(chunk_len * lg)[:, None, None], (RET_HEADS, 1, LANES))
    return tuple(jnp.asarray(t, F32) for t in (decay, widen(qd), widen(kd), cd))


def _angle_tables(refs):
    cb, sb, co, so, co1, so1, co2, so2 = (r[0] for r in refs)
    return cb * co - sb * so, sb * co1 + cb * so1, sb * co2 + cb * so2


def _ffn_proj_kernel(x_ref, mod_ref, n1_ref, wg_ref, wu_ref, wd_ref, nm_ref, win_ref, qn_ref, kn_ref, bd_ref,
                     rope_refs, rot_refs, *refs, tiles_per_seq):
    if tiles_per_seq is None:
        h_ref, ka_ref, va_ref, packed_ref = refs
        qa_ref, ka_row_ref, va_row_ref, qr_ref, kr_ref, vr_ref, gr_ref = (
            packed_ref.at[:, pl.ds(j * ATTN_WIDTH, ATTN_WIDTH)] for j in range(IN_WIDTH // ATTN_WIDTH))
    else:
        ka_row_ref = va_row_ref = None
        ret_tabs, (h_ref, qa_ref, ka_ref, va_ref, oret_ref, st_ref), (qr_ref, kr_ref, vr_ref, gr_ref, s_ref) = (
            refs[:4], refs[4:10], refs[10:])

        @pl.when(pl.program_id(0) == 0)
        def _init_state():
            s_ref[...] = jnp.zeros(s_ref.shape, F32)
    x = x_ref[...]
    mod = lambda i: _mod_rows(mod_ref, i, x.shape[0])
    xn = (_rms(x) * n1_ref[...]) * (1.0 + mod(1)) + mod(0)
    acc = _swiglu_acc(xn.astype(BF16), wg_ref, wu_ref, wd_ref)
    h = x + 0.5 * mod(2) * acc
    h_ref[...] = h
    hn = (_rms(h) * nm_ref[...]) * (1.0 + mod(4)) + mod(3)
    hb = hn.astype(BF16)

    def seg(i):
        return _dot(hb, win_ref[:, i * ATTN_WIDTH:(i + 1) * ATTN_WIDTH])

    ca, sa1, sa2 = _angle_tables(rope_refs)
    cr, sr1, sr2 = _angle_tables(rot_refs)
    bd = bd_ref[...]

    def store_t(o_ref, g, r):
        width = o_ref.shape[2]
        for t in range(o_ref.shape[0]):
            o_ref[t, g * LANES:(g + 1) * LANES, :] = r[t * width:(t + 1) * width, :].T

    def attn_head_norm_rope(p, w, row_ref, t_ref):
        hi, lo = _split(p * p)
        ms = jnp.concatenate([_dot(hi[:, c:c + MXU_TILE], bd) + _dot(lo[:, c:c + MXU_TILE], bd)
                              for c in range(0, ATTN_WIDTH, MXU_TILE)], axis=1)
        pn = (p * lax.rsqrt(ms + EPS)) * w
        for g in range(ATTN_WIDTH // LANES):
            xg = pn[:, g * LANES:(g + 1) * LANES]
            r = xg * ca + pltpu.roll(xg, LANES - ROPE_DIMS // 2, 1) * sa1 + pltpu.roll(xg, ROPE_DIMS // 2, 1) * sa2
            if t_ref is not None:
                store_t(t_ref, g, r)
            if row_ref is not None:
                row_ref[:, g * LANES:(g + 1) * LANES] = r

    def ret_rotate(p, o_ref, scale):
        for g in range(RET_HEADS):
            xg = p[:, g * LANES:(g + 1) * LANES]
            r = xg * cr + pltpu.roll(xg, LANES - 1, 1) * sr1 + pltpu.roll(xg, 1, 1) * sr2
            o_ref[:, g * LANES:(g + 1) * LANES] = r if scale is None else r * scale

    attn_head_norm_rope(seg(0), qn_ref[...], qa_ref, None)
    attn_head_norm_rope(seg(1), kn_ref[...], ka_row_ref, ka_ref)
    va = seg(2)
    for g in range(ATTN_WIDTH // LANES):
        store_t(va_ref, g, va[:, g * LANES:(g + 1) * LANES])
    if va_row_ref is not None:
        va_row_ref[...] = va
    ret_rotate(seg(3), qr_ref, None)
    ret_rotate(seg(4), kr_ref, RET_HEAD_DIM ** -0.5)
    vr_ref[...] = seg(5)
    gr_ref[...] = seg(6)
    if tiles_per_seq is not None:
        _ret_chunks(qr_ref, kr_ref, vr_ref, gr_ref, *ret_tabs, oret_ref, s_ref, pl.program_id(0) % tiles_per_seq == 0)
        st_ref[0] = s_ref[...]


def _ffn_proj(x2d, mods, tabs, lw, *, tm, mod_idx, tab_idx, kv_groups, kv_idx, kv_tiles=1, ret_seqs=None):
    n = x2d.shape[0]
    rm = mods.shape[2]
    d_ff = lw["wg1"].shape[1]
    row = lambda i: (i, 0)
    tab_specs = [[pl.BlockSpec((1, 1, LANES), lambda i: (tab_idx(i), 0, 0))] * 2 + [_const_spec((1, tm, LANES))] * 6] * 2
    in_specs = [
        pl.BlockSpec((tm, D_MODEL), row),
        pl.BlockSpec((1, N_MODS, rm, D_MODEL), lambda i: (mod_idx(i), 0, 0, 0)),
        _const_spec((1, D_MODEL)),
        _const_spec((D_MODEL, d_ff)), _const_spec((D_MODEL, d_ff)), _const_spec((d_ff, D_MODEL)),
        _const_spec((1, D_MODEL)),
        _const_spec((D_MODEL, IN_WIDTH)),
        _const_spec((1, ATTN_WIDTH)), _const_spec((1, ATTN_WIDTH)),
        _const_spec((MXU_TILE, MXU_TILE)),
    ] + tab_specs
    half = jax.ShapeDtypeStruct((n, ATTN_WIDTH), F32)
    half_t = jax.ShapeDtypeStruct((kv_groups, ATTN_WIDTH, n // kv_groups), F32)
    half_spec = pl.BlockSpec((tm, ATTN_WIDTH), row)
    half_t_spec = pl.BlockSpec((kv_tiles, ATTN_WIDTH, tm // kv_tiles), lambda i: kv_idx(i))
    out_shape = [jax.ShapeDtypeStruct((n, D_MODEL), F32), half, half_t, half_t]
    out_specs = [pl.BlockSpec((tm, D_MODEL), row), half_spec, half_t_spec, half_t_spec]
    operands = [x2d, mods, lw["n1"], lw["wg1"], lw["wu1"], lw["wd1"], lw["nm"], lw["win"], lw["qn"], lw["kn"], lw["bd"], *tabs]
    if ret_seqs is None:
        tiles_per_seq, scratch = None, []
        out_shape = [out_shape[0], half_t, half_t, jax.ShapeDtypeStruct((n, IN_WIDTH), F32)]
        out_specs = [out_specs[0], half_t_spec, half_t_spec, pl.BlockSpec((tm, IN_WIDTH), row)]
    else:
        tiles_per_seq = n // ret_seqs // tm
        state = (RET_HEADS, RET_HEAD_DIM, RET_HEAD_DIM)
        operands += list(_ret_tables(RET_CHUNK))
        in_specs += [_const_spec((RET_HEADS, RET_CHUNK, RET_CHUNK)), _const_spec((RET_CHUNK, RET_WIDTH)),
                     _const_spec((RET_CHUNK, RET_WIDTH)), _const_spec((RET_HEADS, 1, LANES))]
        out_shape += [half, jax.ShapeDtypeStruct((ret_seqs,) + state, F32)]
        out_specs += [half_spec, pl.BlockSpec((1,) + state, lambda i: (i // tiles_per_seq, 0, 0, 0))]
        scratch = [pltpu.VMEM((tm, RET_WIDTH), F32)] * 4 + [pltpu.VMEM(state, F32)]
    return pl.pallas_call(
        functools.partial(_ffn_proj_kernel, tiles_per_seq=tiles_per_seq),
        grid=(n // tm,),
        in_specs=in_specs,
        out_specs=out_specs,
        out_shape=out_shape,
        scratch_shapes=scratch,
        compiler_params=pltpu.CompilerParams(dimension_semantics=("arbitrary",), vmem_limit_bytes=VMEM_LIMIT),
        name="ffn_proj",
    )(*operands)


def _out_ffn_kernel(h_ref, oa_ref, or_ref, mod_ref, wo_ref, n2_ref, wg_ref, wu_ref, wd_ref, y_ref):
    mix = _dot(oa_ref[...].astype(BF16), wo_ref[:ATTN_WIDTH, :]) + _dot(or_ref[...].astype(BF16), wo_ref[ATTN_WIDTH:, :])
    mod = lambda i: _mod_rows(mod_ref, i, h_ref.shape[0])
    h = h_ref[...] + mod(5) * mix
    hn = (_rms(h) * n2_ref[...]) * (1.0 + mod(7)) + mod(6)
    acc = _swiglu_acc(hn.astype(BF16), wg_ref, wu_ref, wd_ref)
    y_ref[...] = h + 0.5 * mod(8) * acc


def _out_ffn(h2d, oa, o_ret, mods, lw, *, tm, mod_idx):
    n = h2d.shape[0]
    rm = mods.shape[2]
    d_ff = lw["wg2"].shape[1]
    row = lambda i: (i, 0)
    return pl.pallas_call(
        _out_ffn_kernel,
        grid=(n // tm,),
        in_specs=[
            pl.BlockSpec((tm, D_MODEL), row),
            pl.BlockSpec((tm, ATTN_WIDTH), row),
            pl.BlockSpec((tm, RET_WIDTH), row),
            pl.BlockSpec((1, N_MODS, rm, D_MODEL), lambda i: (mod_idx(i), 0, 0, 0)),
            _const_spec((D_MODEL, D_MODEL)),
            _const_spec((1, D_MODEL)),
            _const_spec((D_MODEL, d_ff)), _const_spec((D_MODEL, d_ff)), _const_spec((d_ff, D_MODEL)),
        ],
        out_specs=pl.BlockSpec((tm, D_MODEL), row),
        out_shape=jax.ShapeDtypeStruct((n, D_MODEL), F32),
        compiler_params=pltpu.CompilerParams(dimension_semantics=("arbitrary",), vmem_limit_bytes=VMEM_LIMIT),
        name="out_ffn",
    )(h2d, oa, o_ret, mods, lw["wo"], lw["n2"], lw["wg2"], lw["wu2"], lw["wd2"])


def _moba_prompt_kernel(q_ref, qn_ref, kt_ref, vt_in_ref, qn2_ref, o_ref,
                        kaug_ref, vt_ref, km_ref, bound_ref, flag_ref, acc_ref, shift_ref, bias_ref, *, nb):
    qi = pl.program_id(2)
    tq = MOBA_BLOCK
    hd = ATTN_HEAD_DIM
    grp = MOBA_GROUP
    lane = lax.broadcasted_iota(jnp.int32, (tq, LANES), 1)
    in_head = (lane < hd, lane >= hd)
    field_off = (hd, 0)

    @pl.when(qi == 0)
    def _prepare():
        km_ref[...] = jnp.zeros(km_ref.shape, F32)
        feat = lax.broadcasted_iota(jnp.int32, (LANES, tq), 0)
        feat_in_head = (feat < hd, feat >= hd)
        kn2 = [jnp.zeros((1, 1), F32)] * 2
        for j in range(nb):
            kb = kt_ref[0, :, j * tq:(j + 1) * tq].T
            vtb = vt_in_ref[0, :, j * tq:(j + 1) * tq]
            km_ref[hd + j:hd + j + 1, :] = jnp.mean(kb, axis=0, keepdims=True)
            sq = kb * kb
            for h in range(2):
                norm2 = jnp.sum(jnp.where(in_head[h], sq, 0.0), axis=1, keepdims=True)
                kn2[h] = jnp.maximum(kn2[h], jnp.max(norm2, axis=0, keepdims=True))
                off = field_off[h]
                field = jnp.where(jnp.logical_or(lane == off + j, lane == off + SHIFT_LANE), 1.0, 0.0)
                kaug_ref[h, j * tq:(j + 1) * tq, :] = jnp.where(in_head[h], kb, field).astype(BF16)
                vt_ref[h, j] = jnp.where(feat_in_head[h], vtb, 1.0).astype(BF16)
        bound = [jnp.sqrt(qn2_ref[0:1, 0:1] * kn2[h]) for h in range(2)]
        for h in range(2):
            bound_ref[h] = jnp.broadcast_to(bound[h], bound_ref.shape[1:])
        flag_ref[0] = (jnp.max(jnp.maximum(bound[0], bound[1])) <= SHIFT_LIMIT).astype(jnp.int32)

        bias_ref[...] = jnp.full(bias_ref.shape, NEG_INF, F32)

    q = q_ref[...]
    scale = hd ** -0.5 * LOG2_E
    qs = [jnp.where(in_head[h], q, 0.0) * scale for h in range(2)]

    field_row = lax.broadcasted_iota(jnp.int32, (LANES, tq), 0)
    field_row_f = field_row.astype(F32)

    def block_bias(q_tile, h, n_past):
        off = field_off[h]
        q_head = jnp.where(in_head[h], q_tile, 0.0)
        gate_t = _dot3_nt(km_ref[hd - off:hd - off + LANES, :], q_head)
        valid = jnp.logical_and(field_row >= off, field_row < off + n_past)
        sel = _top_k_rows(jnp.where(valid, gate_t, NEG_INF), field_row_f, MOBA_TOP_K)
        return jnp.where(jnp.logical_and(sel, valid), 0.0, NEG_INF).T

    key_i = lax.broadcasted_iota(jnp.int32, (tq, tq), 0)
    qry_i = lax.broadcasted_iota(jnp.int32, (tq, tq), 1)
    causal = key_i <= qry_i
    own_rows = pl.ds(pl.multiple_of(qi * tq, tq), tq)
    n_groups = (qi + grp - 1) // grp

    def group_rows(g):
        return pl.ds(pl.multiple_of(g * (grp * tq), grp * tq), grp * tq)

    bounded = flag_ref[0] == 1

    @pl.when(bounded)
    def _bound_shift():
        for h in range(2):
            shift_ref[h] = jnp.broadcast_to(bound_ref[h, 0:1, :], (tq, LANES))

    @pl.when(jnp.logical_not(bounded))
    def _exact_shift():
        for h in range(2):
            q_sel = jnp.where(in_head[h], qs[h], bias_ref[h]).astype(BF16)
            m = _col_max(jnp.where(causal, _dot_nt(kaug_ref[h, own_rows, :], qs[h].astype(BF16)), NEG_INF))
            m = lax.fori_loop(
                0, n_groups, lambda g, m: jnp.maximum(m, _col_max(_dot_nt(kaug_ref[h, group_rows(g), :], q_sel))), m)
            shift_ref[h] = jnp.broadcast_to(m, (LANES, tq)).T

    q_own, q_past = [], []
    for h in range(2):
        at_shift = lane == field_off[h] + SHIFT_LANE
        q_own.append(jnp.where(in_head[h], qs[h], jnp.where(at_shift, -shift_ref[h], 0.0)))
        q_past.append(jnp.where(jnp.logical_or(in_head[h], at_shift), q_own[h], bias_ref[h]).astype(BF16))
    acc_ref[...] = jnp.zeros(acc_ref.shape, F32)

    def add_groups(groups, with_own=False):
        scores = [[_dot_nt(kaug_ref[h, group_rows(g), :], q_past[h]) for h in range(2)] for g in groups]
        if with_own:
            own_scores = [_dot_nt(kaug_ref[h, own_rows, :], q_own[h].astype(BF16)) for h in range(2)]
            q_next = qn_ref[...]
            for h in range(2):
                bias_ref[h] = block_bias(q_next, h, qi + 1)
            for h in range(2):
                s = jnp.where(causal, own_scores[h], NEG_INF)
                acc_ref[h] += _dot(vt_ref[h, qi], jnp.exp2(s).astype(BF16))
        for g, sc in zip(groups, scores):
            for h in range(2):
                p = jnp.exp2(sc[h]).astype(BF16)
                pv = None
                for c in range(grp):
                    part = _dot(vt_ref[h, g * grp + c], p[c * tq:(c + 1) * tq, :])
                    pv = part if pv is None else pv + part
                acc_ref[h] += pv

    def pair_body(t, carry):
        add_groups([2 * t, 2 * t + 1])
        return carry

    has_pair = n_groups >= 2
    odd = n_groups % 2 == 1

    @pl.when(has_pair)
    def _first_pair():
        add_groups([0, 1], with_own=True)

    lax.fori_loop(1, n_groups // 2, pair_body, 0)

    @pl.when(jnp.logical_and(has_pair, odd))
    def _last_group():
        add_groups([n_groups - 1])

    @pl.when(n_groups == 1)
    def _only_group():
        add_groups([0], with_own=True)

    @pl.when(n_groups == 0)
    def _no_group():
        add_groups([], with_own=True)

    a0, a1 = acc_ref[0], acc_ref[1]
    o_t = jnp.concatenate([a0[:hd] * (1.0 / a0[hd:hd + 1]), a1[hd:] * (1.0 / a1[0:1])], axis=0)
    o_ref[...] = o_t.T


def _moba_prompt(qa, kt, vt, q_norm2_max, batch, seq):
    tq = MOBA_BLOCK
    assert seq % (tq * MOBA_GROUP) == 0
    nb = seq // tq
    assert nb <= SHIFT_LANE, "side field holds one lane per key block below the shift lane"
    n_pairs = ATTN_WIDTH // LANES
    kv_spec = pl.BlockSpec((1, LANES, seq), lambda b, hp, qi: (b, hp, 0))
    q_spec = pl.BlockSpec((tq, LANES), lambda b, hp, qi: (b * nb + qi, hp))
    q_next_spec = pl.BlockSpec((tq, LANES), lambda b, hp, qi: (b * nb + jnp.minimum(qi + 1, nb - 1), hp))
    return pl.pallas_call(
        functools.partial(_moba_prompt_kernel, nb=nb),
        grid=(batch, n_pairs, nb),
        in_specs=[q_spec, q_next_spec, kv_spec, kv_spec, _const_spec((1, LANES))],
        out_specs=q_spec,
        out_shape=jax.ShapeDtypeStruct(qa.shape, F32),
        scratch_shapes=[
            pltpu.VMEM((2, seq, LANES), BF16),
            pltpu.VMEM((2, nb, LANES, tq), BF16),
            pltpu.VMEM((ATTN_HEAD_DIM + LANES, LANES), F32),
            pltpu.VMEM((2, SUBLANES, LANES), F32),
            pltpu.SMEM((1,), jnp.int32),
            pltpu.VMEM((2, LANES, tq), F32),
            pltpu.VMEM((2, tq, LANES), F32),
            pltpu.VMEM((2, tq, LANES), F32),
        ],
        compiler_params=pltpu.CompilerParams(dimension_semantics=("arbitrary", "arbitrary", "arbitrary"),
                                             vmem_limit_bytes=VMEM_LIMIT),
        name="moba_prompt",
    )(qa, qa, kt, vt, jnp.broadcast_to(q_norm2_max, (1, LANES)).astype(F32))


def _ret_sample_kernel(q_ref, k_ref, v_ref, g_ref, st_ref, dec_ref, qd_ref, kd_ref, cd_ref, o_ref, ns_ref):
    tp = q_ref.shape[1]
    zeros = jnp.zeros((RET_CHUNK - tp, LANES), F32)
    pairs = [(s, h) for s in range(q_ref.shape[0]) for h in range(RET_HEADS)]
    cols = lambda h: slice(h * LANES, (h + 1) * LANES)
    scores, values = {}, {}
    for s, h in pairs:
        k = jnp.concatenate([k_ref[s, :, cols(h)], zeros], axis=0)
        vb = jnp.concatenate([v_ref[s, :, cols(h)], zeros], axis=0).astype(BF16)
        scores[s, h] = (_dot_nt(q_ref[s, :, cols(h)].astype(BF16), k.astype(BF16)) * dec_ref[h, :tp, :]).astype(BF16)
        ns_ref[s, h] = cd_ref[h] * st_ref[s, h] + _dot((k * kd_ref[:, cols(h)]).T.astype(BF16), vb)
        values[s, h] = vb
    for s, h in pairs:
        q_dec = (q_ref[s, :, cols(h)] * qd_ref[:tp, cols(h)]).astype(BF16)
        out = _dot(scores[s, h], values[s, h]) + _dot(q_dec, st_ref[s, h].astype(BF16))
        o_ref[s, :, cols(h)] = _head_norm_gate(out, g_ref[s, :, cols(h)])


def _ret_sample(packed, cols, state, step_len, *, ns):
    db, tp, _ = packed.shape
    tok = pl.BlockSpec((ns, tp, RET_WIDTH), lambda i: (i, 0, 0))
    tok_in = [pl.BlockSpec((ns, tp, RET_WIDTH), functools.partial(lambda i, c: (i, 0, c), c=c)) for c in cols]
    st = pl.BlockSpec((ns, RET_HEADS, RET_HEAD_DIM, RET_HEAD_DIM), lambda i: (i, 0, 0, 0))
    decay, qd, kd, cd = _ret_tables(step_len)
    return pl.pallas_call(
        _ret_sample_kernel,
        grid=(db // ns,),
        in_specs=tok_in + [st,
                           _const_spec((RET_HEADS, RET_CHUNK, RET_CHUNK)),
                           _const_spec((RET_CHUNK, RET_WIDTH)), _const_spec((RET_CHUNK, RET_WIDTH)),
                           _const_spec((RET_HEADS, 1, LANES))],
        out_specs=[tok, st],
        out_shape=[jax.ShapeDtypeStruct((db, tp, RET_WIDTH), F32), jax.ShapeDtypeStruct(state.shape, F32)],
        compiler_params=pltpu.CompilerParams(dimension_semantics=("arbitrary",)),
        name="ret_sample",
    )(packed, packed, packed, packed, state, decay, qd, kd, cd)


def _moba_sample_seq(q, kn, vn, k_refs, v_refs, *, page, step_len):
    n_pages = len(k_refs)
    hd = ATTN_HEAD_DIM
    ppb = MOBA_BLOCK // page
    n_full = n_pages // ppb
    rows = step_len * ATTN_HEADS
    qrep = jnp.concatenate([jnp.broadcast_to(q[t:t + 1, :], (ATTN_HEADS, ATTN_WIDTH)) for t in range(step_len)], axis=0)
    r_i = lax.broadcasted_iota(jnp.int32, (rows, ATTN_WIDTH), 0)
    c_i = lax.broadcasted_iota(jnp.int32, (rows, ATTN_WIDTH), 1)
    own_head = (c_i // hd) == (r_i % ATTN_HEADS)
    qbd = jnp.where(own_head, qrep, 0.0)
    qs = qbd * (hd ** -0.5)

    vt = lambda p: v_refs[p][...].reshape(ATTN_WIDTH, page)
    lane = lax.broadcasted_iota(jnp.int32, (rows, LANES), 1)
    gate = jnp.full((rows, LANES), NEG_INF, F32)
    k_pages = []
    for b in range(n_full):
        ksum = None
        for p in range(b * ppb, (b + 1) * ppb):
            kt = k_refs[p][...].reshape(ATTN_WIDTH, page)
            k_pages.append(kt.astype(BF16))
            ksum = kt if ksum is None else ksum + kt
        g = jnp.sum(_dot3(qbd, ksum), axis=-1, keepdims=True) * (1.0 / MOBA_BLOCK)
        gate = jnp.where(lane == b, g, gate)
    s_past = _dot(qs.astype(BF16), jnp.concatenate(k_pages, axis=1))
    sel = jnp.logical_and(_top_k_lanes(gate, lane.astype(F32), min(MOBA_TOP_K, n_full)), lane < n_full)
    bias = jnp.where(sel, 0.0, NEG_INF)
    bias_cols = [jnp.max(jnp.where(lane == b, bias, NEG_INF), axis=-1, keepdims=True) for b in range(n_full)]
    s_past = s_past + jnp.concatenate([jnp.broadcast_to(c, (rows, MOBA_BLOCK)) for c in bias_cols], axis=1)

    tok_of_row = lax.broadcasted_iota(jnp.int32, (rows, 1), 0) // ATTN_HEADS
    s_own = []
    for t in range(step_len):
        s_t = jnp.sum(qs * kn[t:t + 1, :], axis=-1, keepdims=True)
        s_own.append(jnp.where(tok_of_row >= t, s_t, NEG_INF))

    m = jnp.maximum(functools.reduce(jnp.maximum, s_own), jnp.max(s_past, axis=-1, keepdims=True))
    e_past = jnp.exp(s_past - m)
    l = jnp.sum(e_past, axis=-1, keepdims=True)
    o = _dot_nt(e_past.astype(BF16), jnp.concatenate([vt(p).astype(BF16) for p in range(n_pages)], axis=1))
    for t in range(step_len):
        e = jnp.exp(s_own[t] - m)
        l = l + e
        o = o + e * vn[t:t + 1, :]
    o = jnp.where(own_head, o, 0.0) * (1.0 / l)
    return [jnp.sum(o[t * ATTN_HEADS:(t + 1) * ATTN_HEADS, :], axis=0, keepdims=True) for t in range(step_len)]


def _moba_sample_kernel(pt_ref, q_ref, kn_ref, vn_ref, ck_hbm, cv_hbm, o_ref, kbuf, vbuf, sems, *, n_pages, page, step_len):
    i = pl.program_id(0)
    n_seq = q_ref.shape[0]

    def page_copies(step, slot):
        copies = []
        for j in range(n_seq * n_pages):
            phys = pt_ref[step * n_seq + j // n_pages, j % n_pages]
            copies.append(pltpu.make_async_copy(ck_hbm.at[phys], kbuf.at[slot, j], sems.at[0, slot]))
            copies.append(pltpu.make_async_copy(cv_hbm.at[phys], vbuf.at[slot, j], sems.at[1, slot]))
        return copies

    @pl.when(i == 0)
    def _first():
        for c in page_copies(0, 0):
            c.start()

    @pl.when(i + 1 < pl.num_programs(0))
    def _prefetch():
        for c in page_copies(i + 1, (i + 1) % 2):
            c.start()

    slot = i % 2
    for c in page_copies(i, slot):
        c.wait()

    o_ref[...] = jnp.zeros(o_ref.shape, F32)
    for s in range(n_seq):
        k_refs = [kbuf.at[slot, s * n_pages + p] for p in range(n_pages)]
        v_refs = [vbuf.at[slot, s * n_pages + p] for p in range(n_pages)]
        out_rows = _moba_sample_seq(q_ref[s], kn_ref[s], vn_ref[s], k_refs, v_refs, page=page, step_len=step_len)
        for t, row in enumerate(out_rows):
            o_ref[s, t:t + 1, :] = row


def _moba_sample(packed, cols, cache_kt, cache_vt, page_table, step_len, *, n_seq):
    db, tp, _ = packed.shape
    n_pages = page_table.shape[1]
    page = cache_kt.shape[-1]
    assert (n_pages * page) % MOBA_BLOCK == 0, "past length must be whole key blocks"
    assert MOBA_BLOCK % page == 0 and db % n_seq == 0
    tok = pl.BlockSpec((n_seq, tp, ATTN_WIDTH), lambda b, pt: (b, 0, 0))
    tok_in = [pl.BlockSpec((n_seq, tp, ATTN_WIDTH), functools.partial(lambda b, pt, c: (b, 0, c), c=c)) for c in cols]
    hbm = pl.BlockSpec(memory_space=pl.ANY)
    page_buf = pltpu.VMEM((2, n_seq * n_pages) + cache_kt.shape[1:], F32)
    grid_spec = pltpu.PrefetchScalarGridSpec(
        num_scalar_prefetch=1,
        grid=(db // n_seq,),
        in_specs=tok_in + [hbm, hbm],
        out_specs=tok,
        scratch_shapes=[page_buf, page_buf, pltpu.SemaphoreType.DMA((2, 2))],
    )
    return pl.pallas_call(
        functools.partial(_moba_sample_kernel, n_pages=n_pages, page=page, step_len=step_len),
        grid_spec=grid_spec,
        out_shape=jax.ShapeDtypeStruct((db, tp, ATTN_WIDTH), F32),
        compiler_params=pltpu.CompilerParams(dimension_semantics=("arbitrary",), vmem_limit_bytes=VMEM_LIMIT),
        name="moba_sample",
    )(page_table, packed, packed, packed, cache_kt, cache_vt)


def _angle_parts(inv, m1, m2, n_tiles, tile, start, repeat=1):
    base = (start + np.arange(n_tiles) * tile)[:, None, None] * inv
    off = np.repeat(np.arange(tile), repeat)[None, :, None] * inv
    co, so = np.cos(off), np.sin(off)
    return [jnp.asarray(t, F32) for t in (np.cos(base), np.sin(base), co, so, co * m1, so * m1, co * m2, so * m2)]


def _rope_parts(n_tiles, tile, start, repeat=1):
    half = ROPE_DIMS // 2
    j = np.arange(LANES) % ATTN_HEAD_DIM
    inv = np.where(j < ROPE_DIMS, ROPE_THETA ** (-(j % half) * 2.0 / ROPE_DIMS), 0.0)
    return _angle_parts(inv, -(j < half).astype(np.float64), np.logical_and(j >= half, j < ROPE_DIMS).astype(np.float64),
                        n_tiles, tile, start, repeat)


def _ret_rot_parts(n_tiles, tile, start, repeat=1):
    lane = np.arange(LANES)
    inv = (1.0 / RET_ROT_BASE ** np.linspace(0.0, 1.0, RET_HEAD_DIM // 2))[lane // 2]
    even = (lane % 2 == 0).astype(np.float64)
    return _angle_parts(inv, -even, 1.0 - even, n_tiles, tile, start, repeat)


def _layer_weights(l, norm_ffn1_w, ffn1_w_gate, ffn1_w_up, ffn1_w_down, norm_mix_w, w_in, q_norm_w, k_norm_w, w_out,
                   norm_ffn2_w, ffn2_w_gate, ffn2_w_up, ffn2_w_down):
    head = np.arange(MXU_TILE) // ATTN_HEAD_DIM
    return {
        "n1": norm_ffn1_w[l][None], "nm": norm_mix_w[l][None], "n2": norm_ffn2_w[l][None],
        "wg1": ffn1_w_gate[l].astype(BF16), "wu1": ffn1_w_up[l].astype(BF16), "wd1": ffn1_w_down[l].astype(BF16),
        "wg2": ffn2_w_gate[l].astype(BF16), "wu2": ffn2_w_up[l].astype(BF16), "wd2": ffn2_w_down[l].astype(BF16),
        "win": w_in[l].astype(BF16), "wo": w_out[l].astype(BF16),
        "qn": jnp.tile(q_norm_w[l], ATTN_HEADS)[None], "kn": jnp.tile(k_norm_w[l], ATTN_HEADS)[None],
        "bd": jnp.asarray(np.where(head[:, None] == head[None, :], 1.0 / ATTN_HEAD_DIM, 0.0), BF16),
    }


def kernel(x_prompt, x_sample, cache_k, cache_v, state_ret, page_table, c_prompt, c_sample, w_ada, b_ada, norm_ffn1_w, ffn1_w_gate, ffn1_w_up, ffn1_w_down, norm_mix_w, w_in, q_norm_w, k_norm_w, w_out, norm_ffn2_w, ffn2_w_gate, ffn2_w_up, ffn2_w_down):
    batch, seq, _ = x_prompt.shape
    db, step_len, _ = x_sample.shape
    depth = w_ada.shape[0]
    n_pages, page = page_table.shape[1], cache_k.shape[2]
    past_len = n_pages * page
    tm_p = 512
    tp = SUBLANES
    assert seq % tm_p == 0 and step_len <= tp and db % SUBLANES == 0

    tabs_p = [_rope_parts(seq // tm_p, tm_p, 0), _ret_rot_parts(seq // tm_p, tm_p, 0)]
    tabs_s = [_rope_parts(1, step_len, past_len, repeat=db), _ret_rot_parts(1, step_len, past_len, repeat=db)]

    hp = x_prompt.reshape(batch * seq, D_MODEL)
    hs = x_sample.transpose(1, 0, 2).reshape(step_len * db, D_MODEL)
    outs = [[] for _ in range(6)]
    for l in range(depth):
        lw = _layer_weights(l, norm_ffn1_w, ffn1_w_gate, ffn1_w_up, ffn1_w_down, norm_mix_w, w_in, q_norm_w, k_norm_w,
                            w_out, norm_ffn2_w, ffn2_w_gate, ffn2_w_up, ffn2_w_down)
        mods = _mods(jnp.concatenate([c_prompt, c_sample], axis=0), w_ada[l], b_ada[l])
        mods_p = mods[:batch].reshape(batch, N_MODS, 1, D_MODEL)
        mods_s = mods[batch:].reshape(db, N_MODS, D_MODEL).transpose(1, 0, 2)[None]

        tiles_per_seq = seq // tm_p
        h1, qa, kt, vt, o_ret, st_p = _ffn_proj(
            hp, mods_p, tabs_p, lw, tm=tm_p, mod_idx=lambda i: i // tiles_per_seq, tab_idx=lambda i: i % tiles_per_seq,
            kv_groups=batch, kv_idx=lambda i: (i // tiles_per_seq, 0, i % tiles_per_seq), ret_seqs=batch)
        q_norm2_max = jnp.max(jnp.square(q_norm_w[l])) * (LOG2_E * LOG2_E)
        oa = _moba_prompt(qa, kt, vt, q_norm2_max, batch, seq)
        hp = _out_ffn(h1, oa, o_ret, mods_p, lw, tm=tm_p, mod_idx=lambda i: i // tiles_per_seq)

        h1s, kts, vts, proj_s = _ffn_proj(
            hs, mods_s, tabs_s, lw, tm=step_len * db, mod_idx=lambda i: 0, tab_idx=lambda i: 0,
            kv_groups=step_len, kv_idx=lambda i: (0, 0, 0), kv_tiles=step_len)
        proj_s = jnp.pad(proj_s.reshape(step_len, db, IN_WIDTH).transpose(1, 0, 2), ((0, 0), (0, tp - step_len), (0, 0)))

        def token_major(t):
            return t[:, :step_len].transpose(1, 0, 2).reshape(step_len * db, ATTN_WIDTH)

        cache_kt = cache_k[l].transpose(0, 2, 3, 1)
        cache_vt = cache_v[l].transpose(0, 2, 3, 1)
        oas = _moba_sample(proj_s, (0, 1, 2), cache_kt, cache_vt, page_table, step_len, n_seq=2)
        o_rets, st_s = _ret_sample(proj_s, (3, 4, 5, 6), state_ret[l], step_len, ns=SUBLANES)
        hs = _out_ffn(h1s, token_major(oas), token_major(o_rets), mods_s, lw, tm=step_len * db, mod_idx=lambda i: 0)

        heads = (ATTN_HEADS, ATTN_HEAD_DIM)
        outs[0].append(kt.reshape(batch, *heads, seq).transpose(0, 3, 1, 2))
        outs[1].append(vt.reshape(batch, *heads, seq).transpose(0, 3, 1, 2))
        outs[2].append(kts.reshape(step_len, *heads, db).transpose(3, 0, 1, 2))
        outs[3].append(vts.reshape(step_len, *heads, db).transpose(3, 0, 1, 2))
        outs[4].append(st_p)
        outs[5].append(st_s)

    y_prompt = hp.reshape(batch, seq, D_MODEL)
    y_sample = hs.reshape(step_len, db, D_MODEL).transpose(1, 0, 2)
    return (y_prompt, y_sample) + tuple(jnp.stack(o) for o in outs)
```
